```python
import jax, jax.numpy as jnp
from jax import lax
import numpy as np


D_MODEL = 1024
BATCH = 4
SEQ = 8192
DEPTH = 1

MLA_HEADS = 8
MLA_NOPE = 64
MLA_ROPE = 32
MLA_QK = MLA_NOPE + MLA_ROPE
MLA_V = 64
MLA_W = MLA_HEADS * MLA_V
MLA_Q_RANK = 256
MLA_KV_RANK = 128
ROPE_THETA = 10000.0
Q_BLOCK = 128

RWKV_HEADS = 8
RWKV_N = 64
RWKV_W = RWKV_HEADS * RWKV_N
DECAY_LORA = 64
AAA_LORA = 64
GATE_LORA = 128

MIX_W = MLA_W + RWKV_W
MLA_IN = MLA_Q_RANK + MLA_KV_RANK + MLA_ROPE
RWKV_IN = 3 * RWKV_W + DECAY_LORA + AAA_LORA + GATE_LORA
IN_W = MLA_IN + RWKV_IN

N_GROUPS = 4
EXPERTS_PER_GROUP = 8
N_EXPERTS = N_GROUPS * EXPERTS_PER_GROUP
TOP_K = 2
D_EXPERT = 256

NORM_EPS = 1e-6
LNX_EPS = 64e-5

kernel_name = 'hymba_mla_rwkv7_hier_moe_block'


def rms_norm(x, g, eps=NORM_EPS):
    xf = x.astype(jnp.float32)
    y = xf * lax.rsqrt(jnp.mean(xf * xf, axis=-1, keepdims=True) + eps)
    return (y * g.astype(jnp.float32)).astype(x.dtype)


def rope_tables(positions):
    half = MLA_ROPE // 2
    inv_freq = 1.0 / (ROPE_THETA ** (jnp.arange(half, dtype=jnp.float32) / half))
    ang = positions.astype(jnp.float32)[..., None] * inv_freq
    return jnp.cos(ang)[:, :, None, :], jnp.sin(ang)[:, :, None, :]


def apply_rope(x, cos, sin):
    half = MLA_ROPE // 2
    xf = x.astype(jnp.float32)
    x1, x2 = xf[..., :half], xf[..., half:]
    return jnp.concatenate([x1 * cos - x2 * sin, x2 * cos + x1 * sin], axis=-1).astype(x.dtype)


def mla_group(c_q, c_kv, k_rope, positions, g_cq, w_uq, g_ckv, w_uk, w_uv, g_qn, g_kn):
    bsz, s, _ = c_q.shape
    q = jnp.einsum('bsr,rhd->bshd', rms_norm(c_q, g_cq), w_uq)
    ckv = rms_norm(c_kv, g_ckv)
    k_nope = jnp.einsum('bsr,rhd->bshd', ckv, w_uk)
    v = jnp.einsum('bsr,rhd->bshd', ckv, w_uv)
    k_rope_h = jnp.broadcast_to(k_rope[:, :, None, :], (bsz, s, MLA_HEADS, MLA_ROPE))
    k = jnp.concatenate([k_nope, k_rope_h], axis=-1)
    q = rms_norm(q, g_qn)
    k = rms_norm(k, g_kn)
    cos, sin = rope_tables(positions)
    q = jnp.concatenate([q[..., :MLA_NOPE], apply_rope(q[..., MLA_NOPE:], cos, sin)], axis=-1)
    k = jnp.concatenate([k[..., :MLA_NOPE], apply_rope(k[..., MLA_NOPE:], cos, sin)], axis=-1)
    scale = MLA_QK ** -0.5
    outs = []
    for blk in range(s // Q_BLOCK):
        q0 = blk * Q_BLOCK
        kend = q0 + Q_BLOCK
        scores = jnp.einsum('bqhd,bkhd->bhqk', q[:, q0:kend], k[:, :kend]).astype(jnp.float32) * scale
        causal = (q0 + jnp.arange(Q_BLOCK))[:, None] >= jnp.arange(kend)[None, :]
        scores = jnp.where(causal, scores, -jnp.inf)
        probs = jax.nn.softmax(scores, axis=-1).astype(v.dtype)
        outs.append(jnp.einsum('bhqk,bkhd->bqhd', probs, v[:, :kend]))
    return jnp.concatenate(outs, axis=1).reshape(bsz, s, MLA_W)


def rwkv7_group(feat, mu, w0, w2, a0, a2, g2, k_k, k_a, r_k, lnx_g, lnx_b):
    bsz, s, _ = feat.shape
    prev = jnp.pad(feat, ((0, 0), (1, 0), (0, 0)))[:, :-1]
    feat = feat + (prev - feat) * mu
    splits = np.cumsum([RWKV_W, RWKV_W, RWKV_W, DECAY_LORA, AAA_LORA]).tolist()
    r, k, v, wl, al, gl = jnp.split(feat, splits, axis=-1)
    w = -jax.nn.softplus(-(w0 + jnp.tanh(wl) @ w2)) - 0.5
    a = jax.nn.sigmoid(a0 + al @ a2)
    g = jax.nn.sigmoid(gl) @ g2
    heads = lambda t: t.reshape(bsz, s, RWKV_HEADS, RWKV_N).astype(jnp.float32)
    kk = heads(k * k_k)
    kk = kk * lax.rsqrt(jnp.sum(kk * kk, axis=-1, keepdims=True) + 1e-12)
    k = k * (1.0 + (a - 1.0) * k_a)
    rh, kh, vh, ah = heads(r), heads(k), heads(v), heads(a)
    decay = jnp.exp(-jnp.exp(heads(w)))

    def step(state, inp):
        r_t, k_t, v_t, kk_t, b_t, d_t = inp
        sa = jnp.einsum('bhvk,bhk->bhv', state, -kk_t)
        state = (state * d_t[:, :, None, :] + sa[..., None] * b_t[:, :, None, :]
                 + v_t[..., None] * k_t[:, :, None, :])
        return state, jnp.einsum('bhvk,bhk->bhv', state, r_t)

    tm = lambda t: jnp.swapaxes(t, 0, 1)
    state0 = jnp.zeros((bsz, RWKV_HEADS, RWKV_N, RWKV_N), jnp.float32)
    _, y = lax.scan(step, state0, (tm(rh), tm(kh), tm(vh), tm(kk), tm(kk * ah), tm(decay)))
    y = tm(y)
    mean = jnp.mean(y, axis=-1, keepdims=True)
    var = jnp.mean(jnp.square(y - mean), axis=-1, keepdims=True)
    y = ((y - mean) * lax.rsqrt(var + LNX_EPS)).reshape(bsz, s, RWKV_W)
    y = y * lnx_g.astype(jnp.float32) + lnx_b.astype(jnp.float32)
    bonus = jnp.sum(rh * kh * r_k.astype(jnp.float32), axis=-1, keepdims=True) * vh
    y = y + bonus.reshape(bsz, s, RWKV_W)
    return (y * g.astype(jnp.float32)).astype(feat.dtype)


def hier_moe(h, w_group, b_group, w_expert, b_expert, w_gate, w_up, w_down):
    bsz, s, d = h.shape
    t = h.reshape(bsz * s, d)
    n = t.shape[0]
    rows = jnp.arange(n)
    group_logits = (t @ w_group).astype(jnp.float32) + b_group.astype(jnp.float32)
    group_probs = jax.nn.softmax(group_logits, axis=-1)
    _, gsel = lax.top_k(group_logits, 1)
    gsel = gsel[:, 0]
    expert_logits = ((t @ w_expert).astype(jnp.float32) + b_expert.astype(jnp.float32)).reshape(
        n, N_GROUPS, EXPERTS_PER_GROUP)
    in_group = expert_logits[rows, gsel]
    top_vals, top_idx = lax.top_k(in_group, TOP_K)
    weights = jax.nn.softmax(top_vals, axis=-1) * group_probs[rows, gsel][:, None]
    expert_ids = gsel[:, None] * EXPERTS_PER_GROUP + top_idx
    combine = jnp.sum(jax.nn.one_hot(expert_ids, N_EXPERTS, dtype=jnp.float32) * weights[..., None],
                      axis=1).astype(t.dtype)
    y = jnp.zeros_like(t)
    for e in range(N_EXPERTS):
        hidden = jax.nn.silu(t @ w_gate[e]) * (t @ w_up[e])
        y = y + combine[:, e:e + 1] * (hidden @ w_down[e])
    return y.reshape(bsz, s, d)


def hybrid_layer(x, positions, g_mix, w_in, g_cq, w_uq, g_ckv, w_uk, w_uv, g_qn, g_kn, g_mla_out,
                 rw_mu, rw_w0, rw_w2, rw_a0, rw_a2, rw_g2, rw_k_k, rw_k_a, rw_r_k, rw_lnx_g, rw_lnx_b,
                 w_o, g_ffn, w_group, b_group, w_expert, b_expert, w_gate, w_up, w_down):
    h = rms_norm(x, g_mix)
    proj = h @ w_in
    c_q = proj[..., :MLA_Q_RANK]
    c_kv = proj[..., MLA_Q_RANK:MLA_Q_RANK + MLA_KV_RANK]
    k_rope = proj[..., MLA_Q_RANK + MLA_KV_RANK:MLA_IN]
    feat = proj[..., MLA_IN:]
    mla_out = rms_norm(mla_group(c_q, c_kv, k_rope, positions, g_cq, w_uq, g_ckv, w_uk, w_uv, g_qn, g_kn),
                       g_mla_out)
    rw_out = rwkv7_group(feat, rw_mu, rw_w0, rw_w2, rw_a0, rw_a2, rw_g2, rw_k_k, rw_k_a, rw_r_k,
                         rw_lnx_g, rw_lnx_b)
    x = x + jnp.concatenate([mla_out, rw_out], axis=-1) @ w_o
    x = x + hier_moe(rms_norm(x, g_ffn), w_group, b_group, w_expert, b_expert, w_gate, w_up, w_down)
    return x


def setup_inputs(seed: int = 0) -> dict:
    key = jax.random.key(seed)
    ks = iter(jax.random.split(key, 40))
    L = DEPTH

    def nrm(shape, scale):
        return jax.random.normal(next(ks), shape, jnp.float32) * scale

    def gain(shape):
        return 1.0 + nrm(shape, 0.02)

    x = nrm((BATCH, SEQ, D_MODEL), 1.0)
    positions = (jnp.arange(SEQ, dtype=jnp.int32)[None, :]
                 + jax.random.randint(next(ks), (BATCH, 1), 0, 1024, jnp.int32))
    return {
        'x': x,
        'positions': positions,
        'g_mix': gain((L, D_MODEL)),
        'w_in': nrm((L, D_MODEL, IN_W), D_MODEL ** -0.5),
        'g_cq': gain((L, MLA_Q_RANK)),
        'w_uq': nrm((L, MLA_Q_RANK, MLA_HEADS, MLA_QK), MLA_Q_RANK ** -0.5),
        'g_ckv': gain((L, MLA_KV_RANK)),
        'w_uk': nrm((L, MLA_KV_RANK, MLA_HEADS, MLA_NOPE), MLA_KV_RANK ** -0.5),
        'w_uv': nrm((L, MLA_KV_RANK, MLA_HEADS, MLA_V), MLA_KV_RANK ** -0.5),
        'g_qn': gain((L, MLA_QK)),
        'g_kn': gain((L, MLA_QK)),
        'g_mla_out': gain((L, MLA_W)),
        'rw_mu': jax.random.uniform(next(ks), (L, RWKV_IN), jnp.float32, 0.0, 1.0),
        'rw_w0': jax.random.uniform(next(ks), (L, RWKV_W), jnp.float32, -4.0, 2.0),
        'rw_w2': nrm((L, DECAY_LORA, RWKV_W), 0.5 * DECAY_LORA ** -0.5),
        'rw_a0': nrm((L, RWKV_W), 0.1),
        'rw_a2': nrm((L, AAA_LORA, RWKV_W), 0.5 * AAA_LORA ** -0.5),
        'rw_g2': nrm((L, GATE_LORA, RWKV_W), GATE_LORA ** -0.5),
        'rw_k_k': 0.85 + nrm((L, RWKV_W), 0.02),
        'rw_k_a': 1.0 + nrm((L, RWKV_W), 0.02),
        'rw_r_k': nrm((L, RWKV_HEADS, RWKV_N), 0.1),
        'rw_lnx_g': gain((L, RWKV_W)),
        'rw_lnx_b': nrm((L, RWKV_W), 0.01),
        'w_o': nrm((L, MIX_W, D_MODEL), MIX_W ** -0.5),
        'g_ffn': gain((L, D_MODEL)),
        'w_group': nrm((L, D_MODEL, N_GROUPS), D_MODEL ** -0.5),
        'b_group': nrm((L, N_GROUPS), 0.01),
        'w_expert': nrm((L, D_MODEL, N_EXPERTS), D_MODEL ** -0.5),
        'b_expert': nrm((L, N_EXPERTS), 0.01),
        'w_gate': nrm((L, N_EXPERTS, D_MODEL, D_EXPERT), D_MODEL ** -0.5),
        'w_up': nrm((L, N_EXPERTS, D_MODEL, D_EXPERT), D_MODEL ** -0.5),
        'w_down': nrm((L, N_EXPERTS, D_EXPERT, D_MODEL), D_EXPERT ** -0.5),
    }


def reference(x, positions, g_mix, w_in, g_cq, w_uq, g_ckv, w_uk, w_uv, g_qn, g_kn, g_mla_out,
              rw_mu, rw_w0, rw_w2, rw_a0, rw_a2, rw_g2, rw_k_k, rw_k_a, rw_r_k, rw_lnx_g, rw_lnx_b,
              w_o, g_ffn, w_group, b_group, w_expert, b_expert, w_gate, w_up, w_down):
    for l in range(DEPTH):
        x = hybrid_layer(x, positions, g_mix[l], w_in[l], g_cq[l], w_uq[l], g_ckv[l], w_uk[l], w_uv[l],
                         g_qn[l], g_kn[l], g_mla_out[l], rw_mu[l], rw_w0[l], rw_w2[l], rw_a0[l],
                         rw_a2[l], rw_g2[l], rw_k_k[l], rw_k_a[l], rw_r_k[l], rw_lnx_g[l], rw_lnx_b[l],
                         w_o[l], g_ffn[l], w_group[l], b_group[l], w_expert[l], b_expert[l],
                         w_gate[l], w_up[l], w_down[l])
    return x
```

```python
import functools

import numpy as np
import jax
import jax.numpy as jnp
from jax import lax
from jax.experimental import pallas as pl
from jax.experimental.pallas import tpu as pltpu

F32 = jnp.float32
BF16 = jnp.bfloat16

D_MODEL = 1024
MLA_HEADS = 8
MLA_NOPE = 64
MLA_ROPE = 32
MLA_QK = MLA_NOPE + MLA_ROPE
MLA_V = 64
MLA_W = MLA_HEADS * MLA_V
MLA_Q_RANK = 256
MLA_KV_RANK = 128
ROPE_THETA = 10000.0
RWKV_HEADS = 8
RWKV_N = 64
RWKV_W = RWKV_HEADS * RWKV_N
DECAY_LORA = 64
AAA_LORA = 64
GATE_LORA = 128
MLA_IN = MLA_Q_RANK + MLA_KV_RANK + MLA_ROPE
RWKV_IN = 3 * RWKV_W + DECAY_LORA + AAA_LORA + GATE_LORA
N_GROUPS = 4
EXPERTS_PER_GROUP = 8
N_EXPERTS = N_GROUPS * EXPERTS_PER_GROUP
D_EXPERT = 256
NORM_EPS = 1e-6
LNX_EPS = 64e-5

LANES = 128
QK_PAD = LANES
PM_W = MLA_Q_RANK + MLA_KV_RANK + 2 * LANES
VMEM_LIMIT = 56 * 1024 * 1024

HIGHEST = lax.Precision.HIGHEST


def _dot(a, b):
    return jnp.dot(a.astype(BF16), b.astype(BF16), preferred_element_type=F32)


def _dot_nt(a, b):
    return lax.dot_general(a.astype(BF16), b.astype(BF16), (((1,), (1,)), ((), ())),
                           preferred_element_type=F32)


def _dot_tn(a, b):
    return lax.dot_general(a.astype(BF16), b.astype(BF16), (((0,), (0,)), ((), ())),
                           preferred_element_type=F32)


def _params(sem):
    return pltpu.CompilerParams(dimension_semantics=sem, vmem_limit_bytes=VMEM_LIMIT)


def _inproj_kernel(x_ref, pos_ref, gmix_ref, wmla_ref, wrw_ref, freq_ref, gcq_ref, gckv_ref,
                   wq_ref, wqr_ref, wk_ref, wv_ref, gq_ref, gqr_ref, gk_ref, gkr_ref,
                   feat_ref, q_ref, k_ref, v_ref):
    x = x_ref[...]
    h = x * lax.rsqrt(jnp.mean(x * x, axis=-1, keepdims=True) + NORM_EPS) * gmix_ref[...]
    hb = h.astype(BF16)
    feat_ref[...] = jnp.dot(hb, wrw_ref[...], preferred_element_type=F32)
    pm = jnp.dot(hb, wmla_ref[...], preferred_element_type=F32)

    c_q = pm[:, :MLA_Q_RANK]
    c_kv = pm[:, MLA_Q_RANK:MLA_Q_RANK + MLA_KV_RANK]
    kr = pm[:, MLA_Q_RANK + MLA_KV_RANK:MLA_Q_RANK + MLA_KV_RANK + LANES]
    kr_rot = pm[:, MLA_Q_RANK + MLA_KV_RANK + LANES:PM_W]
    cqn = (c_q * lax.rsqrt(jnp.mean(c_q * c_q, axis=-1, keepdims=True) + NORM_EPS)
           * gcq_ref[...]).astype(BF16)
    ckvn = (c_kv * lax.rsqrt(jnp.mean(c_kv * c_kv, axis=-1, keepdims=True) + NORM_EPS)
            * gckv_ref[...]).astype(BF16)

    ang = pos_ref[...].astype(F32) * freq_ref[...]
    cosf = jnp.cos(ang)
    sinf = jnp.sin(ang)
    q_cos = gq_ref[...] * cosf * (MLA_QK ** -0.5)
    q_sin = gqr_ref[...] * sinf * (MLA_QK ** -0.5)
    k_cos = gk_ref[...] * cosf
    k_sin = kr_rot * (gkr_ref[...] * sinf)
    inv_qk = 1.0 / MLA_QK

    for hh in range(MLA_HEADS):
        q_raw = jnp.dot(cqn, wq_ref[hh], preferred_element_type=F32)
        q_rot = jnp.dot(cqn, wqr_ref[hh], preferred_element_type=F32)
        sq = lax.rsqrt(jnp.sum(q_raw * q_raw, axis=-1, keepdims=True) * inv_qk + NORM_EPS)
        q_ref[0, hh] = (sq * (q_raw * q_cos + q_rot * q_sin)).astype(BF16)
        k_raw = jnp.dot(ckvn, wk_ref[hh], preferred_element_type=F32) + kr
        sk = lax.rsqrt(jnp.sum(k_raw * k_raw, axis=-1, keepdims=True) * inv_qk + NORM_EPS)
        k_ref[0, hh] = (sk * (k_raw * k_cos + k_sin)).astype(BF16)
        v_ref[0, hh] = jnp.dot(ckvn, wv_ref[hh], preferred_element_type=F32).astype(BF16)


def _rot_cols(w):
    half = MLA_ROPE // 2
    z = jnp.zeros_like(w[..., :MLA_NOPE])
    return jnp.concatenate([z, -w[..., MLA_NOPE + half:MLA_QK], w[..., MLA_NOPE:MLA_NOPE + half]], axis=-1)


def _pad_last(w, n):
    return jnp.pad(w, [(0, 0)] * (w.ndim - 1) + [(0, n - w.shape[-1])])


def _inproj(x2, pos2, g_mix, w_in, g_cq, w_uq, g_ckv, w_uk, w_uv, g_qn, g_kn, bsz, seq, tm):
    t = x2.shape[0]
    nt = seq // tm
    half = MLA_ROPE // 2
    w_kr = w_in[:, MLA_Q_RANK + MLA_KV_RANK:MLA_IN]
    zeros64 = jnp.zeros((D_MODEL, MLA_NOPE), F32)
    zeros32 = jnp.zeros((D_MODEL, LANES - MLA_QK), F32)
    w_kr_rot = jnp.concatenate([-w_kr[:, half:], w_kr[:, :half]], axis=-1)
    w_mla = jnp.concatenate([w_in[:, :MLA_Q_RANK + MLA_KV_RANK], zeros64, w_kr, zeros32,
                             zeros64, w_kr_rot, zeros32], axis=-1).astype(BF16)
    w_rw = w_in[:, MLA_IN:].astype(BF16)
    wq = jnp.transpose(w_uq, (1, 0, 2))
    wq_rot = _pad_last(_rot_cols(wq), QK_PAD).astype(BF16)
    wq = _pad_last(wq, QK_PAD).astype(BF16)
    wk = _pad_last(jnp.transpose(w_uk, (1, 0, 2)), QK_PAD).astype(BF16)
    wv = jnp.transpose(w_uv, (1, 0, 2)).astype(BF16)
    inv_freq = 1.0 / (ROPE_THETA ** (jnp.arange(half, dtype=F32) / half))
    freq = jnp.concatenate([jnp.zeros((MLA_NOPE,), F32), inv_freq, inv_freq,
                            jnp.zeros((LANES - MLA_QK,), F32)])[None, :]
    swap = lambda g: jnp.concatenate([jnp.zeros((MLA_NOPE,), F32), g[MLA_NOPE + half:MLA_QK],
                                      g[MLA_NOPE:MLA_NOPE + half], jnp.zeros((LANES - MLA_QK,), F32)])[None, :]
    gq = _pad_last(g_qn[None, :], QK_PAD)
    gk = _pad_last(g_kn[None, :], QK_PAD)
    gq_rot = swap(g_qn)
    gk_rot = swap(g_kn)

    full = lambda a: pl.BlockSpec(a.shape, lambda i: (0,) * a.ndim)
    head_map = lambda i: (i // nt, 0, i % nt, 0)
    args = (x2, pos2, g_mix[None, :], w_mla, w_rw, freq, g_cq[None, :], g_ckv[None, :],
            wq, wq_rot, wk, wv, gq, gq_rot, gk, gk_rot)
    in_specs = [pl.BlockSpec((tm, D_MODEL), lambda i: (i, 0)),
                pl.BlockSpec((tm, 1), lambda i: (i, 0))] + [full(a) for a in args[2:]]
    return pl.pallas_call(
        _inproj_kernel,
        grid=(t // tm,),
        in_specs=in_specs,
        out_specs=[pl.BlockSpec((tm, RWKV_IN), lambda i: (i, 0)),
                   pl.BlockSpec((1, MLA_HEADS, tm, QK_PAD), head_map),
                   pl.BlockSpec((1, MLA_HEADS, tm, QK_PAD), head_map),
                   pl.BlockSpec((1, MLA_HEADS, tm, MLA_V), head_map)],
        out_shape=[jax.ShapeDtypeStruct((t, RWKV_IN), F32),
                   jax.ShapeDtypeStruct((bsz, MLA_HEADS, seq, QK_PAD), BF16),
                   jax.ShapeDtypeStruct((bsz, MLA_HEADS, seq, QK_PAD), BF16),
                   jax.ShapeDtypeStruct((bsz, MLA_HEADS, seq, MLA_V), BF16)],
        compiler_params=_params(("parallel",)),
        name="inproj",
    )(*args)


def _attn_kernel(q_ref, k_ref, v_ref, o_ref, *, tq):
    qi = pl.program_id(2)
    q = q_ref[0, 0]

    def block(j, carry, masked):
        m, l, acc = carry
        start = pl.multiple_of(j * tq, tq)
        kj = k_ref[0, 0, pl.ds(start, tq), :]
        vj = v_ref[0, 0, pl.ds(start, tq), :]
        s = lax.dot_general(q, kj, (((1,), (1,)), ((), ())), preferred_element_type=F32)
        if masked:
            row = lax.broadcasted_iota(jnp.int32, (tq, tq), 0)
            col = lax.broadcasted_iota(jnp.int32, (tq, tq), 1)
            s = jnp.where(row >= col, s, -jnp.inf)
        m_new = jnp.maximum(m, jnp.max(s, axis=-1, keepdims=True))
        alpha = jnp.exp(m - m_new)
        p = jnp.exp(s - m_new)
        l = alpha * l + jnp.sum(p, axis=-1, keepdims=True)
        acc = alpha * acc + jnp.dot(p.astype(BF16), vj, preferred_element_type=F32)
        return m_new, l, acc

    init = (jnp.full((tq, 1), -jnp.inf, F32), jnp.zeros((tq, 1), F32), jnp.zeros((tq, MLA_V), F32))
    carry = lax.fori_loop(0, qi, lambda j, c: block(j, c, False), init)
    m, l, acc = block(qi, carry, True)
    o_ref[0, 0] = acc / l


def _attn(q, k, v, tq):
    bsz, nh, seq, _ = q.shape
    return pl.pallas_call(
        functools.partial(_attn_kernel, tq=tq),
        grid=(bsz, nh, seq // tq),
        in_specs=[pl.BlockSpec((1, 1, tq, QK_PAD), lambda b, h, i: (b, h, i, 0)),
                  pl.BlockSpec((1, 1, seq, QK_PAD), lambda b, h, i: (b, h, 0, 0)),
                  pl.BlockSpec((1, 1, seq, MLA_V), lambda b, h, i: (b, h, 0, 0))],
        out_specs=pl.BlockSpec((1, 1, tq, MLA_V), lambda b, h, i: (b, h, i, 0)),
        out_shape=jax.ShapeDtypeStruct((bsz, nh, seq, MLA_V), F32),
        compiler_params=_params(("parallel", "parallel", "arbitrary")),
        name="attn",
    )(q, k, v)


def _rwprep_kernel(feat_ref, prev_ref, mu_ref, w0_ref, w2_ref, a0_ref, a2_ref, g2_ref, kk_ref, ka_ref,
                   bd_ref, r_o, k_o, v_o, kk_o, lr_o, ld_o, g_o, *, nt):
    i = pl.program_id(0)
    feat = feat_ref[...]
    tm = feat.shape[0]
    first = jnp.where(i % nt == 0, 0.0, 1.0)
    prev_row = prev_ref[7:8, :] * first
    rolled = pltpu.roll(feat, 1, axis=0)
    rid = lax.broadcasted_iota(jnp.int32, (tm, 1), 0)
    prev = jnp.where(rid == 0, prev_row, rolled)
    feat = feat + (prev - feat) * mu_ref[...]

    r = feat[:, :RWKV_W]
    k = feat[:, RWKV_W:2 * RWKV_W]
    v = feat[:, 2 * RWKV_W:3 * RWKV_W]
    wa = feat[:, 3 * RWKV_W:3 * RWKV_W + LANES]
    gl = feat[:, 3 * RWKV_W + LANES:]
    lane = lax.broadcasted_iota(jnp.int32, (tm, LANES), 1)
    wl_in = jnp.where(lane < DECAY_LORA, jnp.tanh(wa), 0.0)
    al_in = jnp.where(lane < DECAY_LORA, 0.0, wa)
    wpre = w0_ref[...] + _dot(wl_in, w2_ref[...])
    w = -(jnp.maximum(-wpre, 0.0) + jnp.log(1.0 + jnp.exp(-jnp.abs(wpre)))) - 0.5
    logd = -jnp.exp(w)
    a = jax.nn.sigmoid(a0_ref[...] + _dot(al_in, a2_ref[...]))
    g = _dot(jax.nn.sigmoid(gl), g2_ref[...])
    kk = k * kk_ref[...]
    ss = jnp.dot(kk * kk, bd_ref[...], preferred_element_type=F32, precision=HIGHEST)
    kk = kk * lax.rsqrt(ss + 1e-12)
    k2 = k * (1.0 + (a - 1.0) * ka_ref[...])
    for hh in range(RWKV_HEADS):
        sl = slice(hh * RWKV_N, (hh + 1) * RWKV_N)
        r_o[0, hh] = r[:, sl]
        k_o[0, hh] = k2[:, sl]
        v_o[0, hh] = v[:, sl]
        kk_o[0, hh] = kk[:, sl]
        lr_o[0, hh] = a[:, sl]
        ld_o[0, hh] = logd[:, sl]
        g_o[0, hh] = g[:, sl]


def _rwprep(feat, rw_mu, rw_w0, rw_w2, rw_a0, rw_a2, rw_g2, rw_k_k, rw_k_a, bsz, seq, tm):
    t = feat.shape[0]
    nt = seq // tm
    w2p = jnp.concatenate([rw_w2, jnp.zeros((LANES - DECAY_LORA, RWKV_W), F32)], axis=0).astype(BF16)
    a2p = jnp.concatenate([jnp.zeros((DECAY_LORA, RWKV_W), F32), rw_a2], axis=0).astype(BF16)
    hid = np.arange(RWKV_W) // RWKV_N
    bd = jnp.asarray((hid[:, None] == hid[None, :]).astype(np.float32))
    args = (feat, feat, rw_mu[None, :], rw_w0[None, :], w2p, rw_a0[None, :], a2p, rw_g2.astype(BF16),
            rw_k_k[None, :], rw_k_a[None, :], bd)
    full = lambda a: pl.BlockSpec(a.shape, lambda i: (0,) * a.ndim)
    head_map = lambda i: (i // nt, 0, i % nt, 0)
    rows8 = tm // 8
    in_specs = [pl.BlockSpec((tm, RWKV_IN), lambda i: (i, 0)),
                pl.BlockSpec((8, RWKV_IN), lambda i: (jnp.maximum(i * rows8 - 1, 0), 0))]
    in_specs += [full(a) for a in args[2:]]
    hshape = jax.ShapeDtypeStruct((bsz, RWKV_HEADS, seq, RWKV_N), F32)
    hspec = pl.BlockSpec((1, RWKV_HEADS, tm, RWKV_N), head_map)
    return pl.pallas_call(
        functools.partial(_rwprep_kernel, nt=nt),
        grid=(t // tm,),
        in_specs=in_specs,
        out_specs=[hspec] * 7,
        out_shape=[hshape] * 7,
        compiler_params=_params(("parallel",)),
        name="rwprep",
    )(*args)


def _rwscan_kernel(r_ref, k_ref, v_ref, kk_ref, lr_ref, ld_ref, g_ref, rk_ref, lng_ref, lnb_ref,
                   tri_ref, o_ref, state, *, chunk):
    c = pl.program_id(1)

    @pl.when(c == 0)
    def _():
        state[...] = jnp.zeros_like(state)

    tri = tri_ref[...]
    row = lax.broadcasted_iota(jnp.int32, (chunk, chunk), 0)
    col = lax.broadcasted_iota(jnp.int32, (chunk, chunk), 1)
    strict = row > col
    incl = row >= col
    eye = (row == col).astype(F32)
    mid = chunk // 2 - 1

    for hh in range(RWKV_HEADS):
        r = r_ref[0, hh]
        k = k_ref[0, hh]
        v = v_ref[0, hh]
        kk = kk_ref[0, hh]
        lr = lr_ref[0, hh]
        ld = ld_ref[0, hh]
        s0 = state[hh]

        cum = jnp.dot(tri, ld, preferred_element_type=F32, precision=HIGHEST)
        cmid = cum[mid:mid + 1, :]
        cend = cum[chunk - 1:chunk, :]
        e_pos = jnp.exp(cum - cmid)
        e_prev = jnp.exp(cum - ld - cmid)
        e_neg = jnp.exp(cmid - cum)
        g_mid = jnp.exp(cmid)
        g_end_rel = jnp.exp(cend - cmid)
        g_end = jnp.exp(cend)

        rt = r * e_pos
        at = -kk * e_prev
        bt = kk * lr * e_neg
        kt = k * e_neg
        a_ab = jnp.where(strict, _dot_nt(at, bt), 0.0)
        a_ak = jnp.where(strict, _dot_nt(at, kt), 0.0)
        a_rb = jnp.where(incl, _dot_nt(rt, bt), 0.0)
        a_rk = jnp.where(incl, _dot_nt(rt, kt), 0.0)

        tinv = eye + a_ab
        p = a_ab
        n = 2
        while n < chunk:
            p = _dot(p, p)
            tinv = tinv + _dot(tinv, p)
            n *= 2

        x = _dot_nt(at * g_mid, s0) + _dot(a_ak, v)
        u = _dot(tinv, x)
        y = _dot_nt(rt * g_mid, s0) + _dot(a_rb, u) + _dot(a_rk, v)
        state[hh] = s0 * g_end + _dot_tn(u, bt * g_end_rel) + _dot_tn(v, kt * g_end_rel)

        mean = jnp.mean(y, axis=-1, keepdims=True)
        yc = y - mean
        var = jnp.mean(yc * yc, axis=-1, keepdims=True)
        yn = yc * lax.rsqrt(var + LNX_EPS) * lng_ref[hh] + lnb_ref[hh]
        bonus = jnp.sum(r * k * rk_ref[hh], axis=-1, keepdims=True) * v
        o_ref[0, hh] = (yn + bonus) * g_ref[0, hh]


def _rwscan(r, k, v, kk, lr, ld, g, rw_r_k, rw_lnx_g, rw_lnx_b, chunk):
    bsz, nh, seq, n = r.shape
    tri = jnp.asarray(np.tril(np.ones((chunk, chunk), np.float32)))
    hspec = pl.BlockSpec((1, nh, chunk, n), lambda b, c: (b, 0, c, 0))
    per_head = lambda a: a.reshape(nh, 1, n)
    small = pl.BlockSpec((nh, 1, n), lambda b, c: (0, 0, 0))
    return pl.pallas_call(
        functools.partial(_rwscan_kernel, chunk=chunk),
        grid=(bsz, seq // chunk),
        in_specs=[hspec] * 7 + [small] * 3 + [pl.BlockSpec((chunk, chunk), lambda b, c: (0, 0))],
        out_specs=hspec,
        out_shape=jax.ShapeDtypeStruct((bsz, nh, seq, n), F32),
        scratch_shapes=[pltpu.VMEM((nh, n, n), F32)],
        compiler_params=_params(("parallel", "arbitrary")),
        name="rwscan",
    )(r, k, v, kk, lr, ld, g, per_head(rw_r_k), per_head(rw_lnx_g), per_head(rw_lnx_b), tri)


def _outproj_kernel(x_ref, mla_ref, rw_ref, gmo_ref, wo_ref, gffn_ref, wr_ref, br_ref,
                    x1_ref, xn_ref, comb_ref):
    tm = x_ref.shape[0]
    ss = jnp.zeros((tm, 1), F32)
    for hh in range(MLA_HEADS):
        m = mla_ref[0, hh]
        ss = ss + jnp.sum(m * m, axis=-1, keepdims=True)
    s = lax.rsqrt(ss * (1.0 / MLA_W) + NORM_EPS)
    acc = x_ref[...]
    for hh in range(MLA_HEADS):
        acc = acc + _dot(mla_ref[0, hh] * s * gmo_ref[hh], wo_ref[hh])
    for hh in range(RWKV_HEADS):
        acc = acc + _dot(rw_ref[0, hh], wo_ref[MLA_HEADS + hh])
    x1_ref[...] = acc
    xn = acc * lax.rsqrt(jnp.mean(acc * acc, axis=-1, keepdims=True) + NORM_EPS) * gffn_ref[...]
    xn_ref[...] = xn.astype(BF16)

    logits = jnp.dot(xn, wr_ref[...], preferred_element_type=F32, precision=HIGHEST) + br_ref[...]
    lane = lax.broadcasted_iota(jnp.int32, (tm, LANES), 1)
    neg = -jnp.inf
    big = jnp.int32(1 << 20)
    is_g = (lane >= N_EXPERTS) & (lane < N_EXPERTS + N_GROUPS)
    glog = jnp.where(is_g, logits, neg)
    gmax = jnp.max(glog, axis=-1, keepdims=True)
    gsel = jnp.min(jnp.where(glog == gmax, lane, big), axis=-1, keepdims=True) - N_EXPERTS
    gprob = 1.0 / jnp.sum(jnp.exp(glog - gmax), axis=-1, keepdims=True)
    lo = gsel * EXPERTS_PER_GROUP
    in_group = (lane >= lo) & (lane < lo + EXPERTS_PER_GROUP)
    elog = jnp.where(in_group, logits, neg)
    v1 = jnp.max(elog, axis=-1, keepdims=True)
    i1 = jnp.min(jnp.where(elog == v1, lane, big), axis=-1, keepdims=True)
    elog2 = jnp.where(lane == i1, neg, elog)
    v2 = jnp.max(elog2, axis=-1, keepdims=True)
    i2 = jnp.min(jnp.where(elog2 == v2, lane, big), axis=-1, keepdims=True)
    e2 = jnp.exp(v2 - v1)
    w1 = gprob / (1.0 + e2)
    w2 = gprob * e2 / (1.0 + e2)
    comb_ref[...] = jnp.where(lane == i1, w1, 0.0) + jnp.where(lane == i2, w2, 0.0)


def _outproj(x2, mla, rw, g_mla_out, w_o, g_ffn, w_group, b_group, w_expert, b_expert, seq, tm):
    t = x2.shape[0]
    nt = seq // tm
    nh = MLA_HEADS + RWKV_HEADS
    wo = w_o.reshape(nh, MLA_V, D_MODEL).astype(BF16)
    pad = LANES - N_EXPERTS - N_GROUPS
    wr = jnp.concatenate([w_expert, w_group, jnp.zeros((D_MODEL, pad), F32)], axis=-1)
    br = jnp.concatenate([b_expert, b_group, jnp.zeros((pad,), F32)])[None, :]
    gmo = g_mla_out.reshape(MLA_HEADS, 1, MLA_V)
    head_map = lambda i: (i // nt, 0, i % nt, 0)
    full = lambda a: pl.BlockSpec(a.shape, lambda i: (0,) * a.ndim)
    row = pl.BlockSpec((tm, D_MODEL), lambda i: (i, 0))
    return pl.pallas_call(
        _outproj_kernel,
        grid=(t // tm,),
        in_specs=[row,
                  pl.BlockSpec((1, MLA_HEADS, tm, MLA_V), head_map),
                  pl.BlockSpec((1, RWKV_HEADS, tm, RWKV_N), head_map),
                  full(gmo), full(wo), pl.BlockSpec((1, D_MODEL), lambda i: (0, 0)), full(wr), full(br)],
        out_specs=[row, row, pl.BlockSpec((tm, LANES), lambda i: (i, 0))],
        out_shape=[jax.ShapeDtypeStruct((t, D_MODEL), F32),
                   jax.ShapeDtypeStruct((t, D_MODEL), BF16),
                   jax.ShapeDtypeStruct((t, LANES), F32)],
        compiler_params=_params(("parallel",)),
        name="outproj",
    )(x2, mla, rw, gmo, wo, g_ffn[None, :], wr, br)


def _moe_kernel(x1_ref, xn_ref, comb_ref, wg_ref, wu_ref, wd_ref, o_ref):
    e = pl.program_id(1)

    @pl.when(e == 0)
    def _():
        o_ref[...] = x1_ref[...]

    xn = xn_ref[...]
    hg = jnp.dot(xn, wg_ref[0], preferred_element_type=F32)
    hu = jnp.dot(xn, wu_ref[0], preferred_element_type=F32)
    hid = (hg * jax.nn.sigmoid(hg) * hu).astype(BF16)
    y = jnp.dot(hid, wd_ref[0], preferred_element_type=F32)
    lane = lax.broadcasted_iota(jnp.int32, comb_ref.shape, 1)
    ce = jnp.sum(jnp.where(lane == e, comb_ref[...], 0.0), axis=-1, keepdims=True)
    o_ref[...] += ce * y


def _moe(x1, xn, comb, w_gate, w_up, w_down, tm):
    t = x1.shape[0]
    row = lambda w: pl.BlockSpec((tm, w), lambda i, e: (i, 0))
    return pl.pallas_call(
        _moe_kernel,
        grid=(t // tm, N_EXPERTS),
        in_specs=[row(D_MODEL), row(D_MODEL), row(LANES),
                  pl.BlockSpec((1, D_MODEL, D_EXPERT), lambda i, e: (e, 0, 0)),
                  pl.BlockSpec((1, D_MODEL, D_EXPERT), lambda i, e: (e, 0, 0)),
                  pl.BlockSpec((1, D_EXPERT, D_MODEL), lambda i, e: (e, 0, 0))],
        out_specs=row(D_MODEL),
        out_shape=jax.ShapeDtypeStruct((t, D_MODEL), F32),
        compiler_params=_params(("parallel", "arbitrary")),
        name="moe",
    )(x1, xn, comb, w_gate.astype(BF16), w_up.astype(BF16), w_down.astype(BF16))


def _tile(seq, want):
    tm = min(seq, want)
    assert seq % tm == 0 and tm % 8 == 0
    return tm


def _layer(x, positions, g_mix, w_in, g_cq, w_uq, g_ckv, w_uk, w_uv, g_qn, g_kn, g_mla_out,
           rw_mu, rw_w0, rw_w2, rw_a0, rw_a2, rw_g2, rw_k_k, rw_k_a, rw_r_k, rw_lnx_g, rw_lnx_b,
           w_o, g_ffn, w_group, b_group, w_expert, b_expert, w_gate, w_up, w_down):
    bsz, seq, d = x.shape
    t = bsz * seq
    x2 = x.reshape(t, d)
    pos2 = positions.reshape(t, 1)
    feat, q, k, v = _inproj(x2, pos2, g_mix, w_in, g_cq, w_uq, g_ckv, w_uk, w_uv, g_qn, g_kn,
                            bsz, seq, _tile(seq, 512))
    mla = _attn(q, k, v, _tile(seq, 512))
    r, k2, vv, kk, lr, ld, g = _rwprep(feat, rw_mu, rw_w0, rw_w2, rw_a0, rw_a2, rw_g2, rw_k_k, rw_k_a,
                                       bsz, seq, _tile(seq, 512))
    rw = _rwscan(r, k2, vv, kk, lr, ld, g, rw_r_k, rw_lnx_g, rw_lnx_b, _tile(seq, 64))
    x1, xn, comb = _outproj(x2, mla, rw, g_mla_out, w_o, g_ffn, w_group, b_group, w_expert, b_expert,
                            seq, _tile(seq, 512))
    out = _moe(x1, xn, comb, w_gate, w_up, w_down, _tile(t, 1024))
    return out.reshape(bsz, seq, d)


def kernel(x, positions, g_mix, w_in, g_cq, w_uq, g_ckv, w_uk, w_uv, g_qn, g_kn, g_mla_out, rw_mu, rw_w0, rw_w2, rw_a0, rw_a2, rw_g2, rw_k_k, rw_k_a, rw_r_k, rw_lnx_g, rw_lnx_b, w_o, g_ffn, w_group, b_group, w_expert, b_expert, w_gate, w_up, w_down):
    for l in range(g_mix.shape[0]):
        x = _layer(x, positions, g_mix[l], w_in[l], g_cq[l], w_uq[l], g_ckv[l], w_uk[l], w_uv[l],
                   g_qn[l], g_kn[l], g_mla_out[l], rw_mu[l], rw_w0[l], rw_w2[l], rw_a0[l], rw_a2[l],
                   rw_g2[l], rw_k_k[l], rw_k_a[l], rw_r_k[l], rw_lnx_g[l], rw_lnx_b[l], w_o[l],
                   g_ffn[l], w_group[l], b_group[l], w_expert[l], b_expert[l], w_gate[l], w_up[l],
                   w_down[l])
    return x
```

```python
import functools

import numpy as np
import jax
import jax.numpy as jnp
from jax import lax
from jax.experimental import pallas as pl
from jax.experimental.pallas import tpu as pltpu

F32 = jnp.float32
BF16 = jnp.bfloat16

D_MODEL = 1024
MLA_HEADS = 8
MLA_NOPE = 64
MLA_ROPE = 32
MLA_QK = MLA_NOPE + MLA_ROPE
MLA_V = 64
MLA_W = MLA_HEADS * MLA_V
MLA_Q_RANK = 256
MLA_KV_RANK = 128
ROPE_THETA = 10000.0
RWKV_HEADS = 8
RWKV_N = 64
RWKV_W = RWKV_HEADS * RWKV_N
DECAY_LORA = 64
AAA_LORA = 64
GATE_LORA = 128
MLA_IN = MLA_Q_RANK + MLA_KV_RANK + MLA_ROPE
RWKV_IN = 3 * RWKV_W + DECAY_LORA + AAA_LORA + GATE_LORA
N_GROUPS = 4
EXPERTS_PER_GROUP = 8
N_EXPERTS = N_GROUPS * EXPERTS_PER_GROUP
D_EXPERT = 256
NORM_EPS = 1e-6
LNX_EPS = 64e-5

LANES = 128
QK_PAD = LANES
PM_W = MLA_Q_RANK + MLA_KV_RANK + 2 * LANES
VMEM_LIMIT = 56 * 1024 * 1024

HIGHEST = lax.Precision.HIGHEST


def _dot(a, b):
    return jnp.dot(a.astype(BF16), b.astype(BF16), preferred_element_type=F32)


def _dot_nt(a, b):
    return lax.dot_general(a.astype(BF16), b.astype(BF16), (((1,), (1,)), ((), ())),
                           preferred_element_type=F32)


def _dot_tn(a, b):
    return lax.dot_general(a.astype(BF16), b.astype(BF16), (((0,), (0,)), ((), ())),
                           preferred_element_type=F32)


def _dot_split(a, b, split_lhs, passes=2):
    s = a if split_lhs else b
    out = None
    for _ in range(passes):
        piece = s.astype(BF16)
        term = (jnp.dot(piece, b, preferred_element_type=F32) if split_lhs
                else jnp.dot(a, piece, preferred_element_type=F32))
        out = term if out is None else out + term
        s = s - piece.astype(F32)
    return out


def _head_indicator(width):
    hid = np.arange(width) // RWKV_N
    return jnp.asarray((hid[:, None] == hid[None, :]).astype(np.float32)).astype(BF16)


def _params(sem):
    return pltpu.CompilerParams(dimension_semantics=sem, vmem_limit_bytes=VMEM_LIMIT)


def _inproj_kernel(x_ref, pos_ref, gmix_ref, wmla_ref, wrw_ref, freq_ref, gcq_ref, gckv_ref,
                   wq_ref, wqr_ref, wk_ref, wv_ref, gq_ref, gqr_ref, gk_ref, gkr_ref,
                   feat_ref, q_ref, k_ref, v_ref):
    x = x_ref[...]
    h = x * lax.rsqrt(jnp.mean(x * x, axis=-1, keepdims=True) + NORM_EPS) * gmix_ref[...]
    hb = h.astype(BF16)
    feat_ref[...] = jnp.dot(hb, wrw_ref[...], preferred_element_type=F32)
    pm = jnp.dot(hb, wmla_ref[...], preferred_element_type=F32)

    c_q = pm[:, :MLA_Q_RANK]
    c_kv = pm[:, MLA_Q_RANK:MLA_Q_RANK + MLA_KV_RANK]
    kr = pm[:, MLA_Q_RANK + MLA_KV_RANK:MLA_Q_RANK + MLA_KV_RANK + LANES]
    kr_rot = pm[:, MLA_Q_RANK + MLA_KV_RANK + LANES:PM_W]
    cqn = (c_q * lax.rsqrt(jnp.mean(c_q * c_q, axis=-1, keepdims=True) + NORM_EPS)
           * gcq_ref[...]).astype(BF16)
    ckvn = (c_kv * lax.rsqrt(jnp.mean(c_kv * c_kv, axis=-1, keepdims=True) + NORM_EPS)
            * gckv_ref[...]).astype(BF16)

    ang = pos_ref[...].astype(F32) * freq_ref[...]
    cosf = jnp.cos(ang)
    sinf = jnp.sin(ang)
    q_cos = gq_ref[...] * cosf * (MLA_QK ** -0.5)
    q_sin = gqr_ref[...] * sinf * (MLA_QK ** -0.5)
    k_cos = gk_ref[...] * cosf
    k_sin = kr_rot * (gkr_ref[...] * sinf)
    inv_qk = 1.0 / MLA_QK

    for hh in range(MLA_HEADS):
        q_raw = jnp.dot(cqn, wq_ref[hh], preferred_element_type=F32)
        q_rot = jnp.dot(cqn, wqr_ref[hh], preferred_element_type=F32)
        sq = lax.rsqrt(jnp.sum(q_raw * q_raw, axis=-1, keepdims=True) * inv_qk + NORM_EPS)
        q_ref[0, hh] = (sq * (q_raw * q_cos + q_rot * q_sin)).astype(BF16)
        k_raw = jnp.dot(ckvn, wk_ref[hh], preferred_element_type=F32) + kr
        sk = lax.rsqrt(jnp.sum(k_raw * k_raw, axis=-1, keepdims=True) * inv_qk + NORM_EPS)
        k_ref[0, hh] = (sk * (k_raw * k_cos + k_sin)).astype(BF16)
        v_ref[0, hh] = jnp.dot(ckvn, wv_ref[hh], preferred_element_type=F32).astype(BF16)


def _rot_cols(w):
    half = MLA_ROPE // 2
    z = jnp.zeros_like(w[..., :MLA_NOPE])
    return jnp.concatenate([z, -w[..., MLA_NOPE + half:MLA_QK], w[..., MLA_NOPE:MLA_NOPE + half]], axis=-1)


def _pad_last(w, n):
    return jnp.pad(w, [(0, 0)] * (w.ndim - 1) + [(0, n - w.shape[-1])])


def _inproj(x2, pos2, g_mix, w_in, g_cq, w_uq, g_ckv, w_uk, w_uv, g_qn, g_kn, bsz, seq, tm):
    t = x2.shape[0]
    nt = seq // tm
    half = MLA_ROPE // 2
    w_kr = w_in[:, MLA_Q_RANK + MLA_KV_RANK:MLA_IN]
    zeros64 = jnp.zeros((D_MODEL, MLA_NOPE), F32)
    zeros32 = jnp.zeros((D_MODEL, LANES - MLA_QK), F32)
    w_kr_rot = jnp.concatenate([-w_kr[:, half:], w_kr[:, :half]], axis=-1)
    w_mla = jnp.concatenate([w_in[:, :MLA_Q_RANK + MLA_KV_RANK], zeros64, w_kr, zeros32,
                             zeros64, w_kr_rot, zeros32], axis=-1).astype(BF16)
    w_rw = w_in[:, MLA_IN:].astype(BF16)
    wq = jnp.transpose(w_uq, (1, 0, 2))
    wq_rot = _pad_last(_rot_cols(wq), QK_PAD).astype(BF16)
    wq = _pad_last(wq, QK_PAD).astype(BF16)
    wk = _pad_last(jnp.transpose(w_uk, (1, 0, 2)), QK_PAD).astype(BF16)
    wv = jnp.transpose(w_uv, (1, 0, 2)).astype(BF16)
    inv_freq = 1.0 / (ROPE_THETA ** (jnp.arange(half, dtype=F32) / half))
    freq = jnp.concatenate([jnp.zeros((MLA_NOPE,), F32), inv_freq, inv_freq,
                            jnp.zeros((LANES - MLA_QK,), F32)])[None, :]
    swap = lambda g: jnp.concatenate([jnp.zeros((MLA_NOPE,), F32), g[MLA_NOPE + half:MLA_QK],
                                      g[MLA_NOPE:MLA_NOPE + half], jnp.zeros((LANES - MLA_QK,), F32)])[None, :]
    gq = _pad_last(g_qn[None, :], QK_PAD)
    gk = _pad_last(g_kn[None, :], QK_PAD)
    gq_rot = swap(g_qn)
    gk_rot = swap(g_kn)

    full = lambda a: pl.BlockSpec(a.shape, lambda i: (0,) * a.ndim)
    head_map = lambda i: (i // nt, 0, i % nt, 0)
    args = (x2, pos2, g_mix[None, :], w_mla, w_rw, freq, g_cq[None, :], g_ckv[None, :],
            wq, wq_rot, wk, wv, gq, gq_rot, gk, gk_rot)
    in_specs = [pl.BlockSpec((tm, D_MODEL), lambda i: (i, 0)),
                pl.BlockSpec((tm, 1), lambda i: (i, 0))] + [full(a) for a in args[2:]]
    return pl.pallas_call(
        _inproj_kernel,
        grid=(t // tm,),
        in_specs=in_specs,
        out_specs=[pl.BlockSpec((tm, RWKV_IN), lambda i: (i, 0)),
                   pl.BlockSpec((1, MLA_HEADS, tm, QK_PAD), head_map),
                   pl.BlockSpec((1, MLA_HEADS, tm, QK_PAD), head_map),
                   pl.BlockSpec((1, MLA_HEADS, tm, MLA_V), head_map)],
        out_shape=[jax.ShapeDtypeStruct((t, RWKV_IN), F32),
                   jax.ShapeDtypeStruct((bsz, MLA_HEADS, seq, QK_PAD), BF16),
                   jax.ShapeDtypeStruct((bsz, MLA_HEADS, seq, QK_PAD), BF16),
                   jax.ShapeDtypeStruct((bsz, MLA_HEADS, seq, MLA_V), BF16)],
        compiler_params=_params(("parallel",)),
        name="inproj",
    )(*args)


def _attn_kernel(q_ref, k_ref, v_ref, o_ref, *, tq):
    qi = pl.program_id(2)
    q = q_ref[0, 0]

    def block(j, carry, masked):
        m, l, acc = carry
        start = pl.multiple_of(j * tq, tq)
        kj = k_ref[0, 0, pl.ds(start, tq), :]
        vj = v_ref[0, 0, pl.ds(start, tq), :]
        s = lax.dot_general(q, kj, (((1,), (1,)), ((), ())), preferred_element_type=F32)
        if masked:
            row = lax.broadcasted_iota(jnp.int32, (tq, tq), 0)
            col = lax.broadcasted_iota(jnp.int32, (tq, tq), 1)
            s = jnp.where(row >= col, s, -jnp.inf)
        m_new = jnp.maximum(m, jnp.max(s, axis=-1, keepdims=True))
        alpha = jnp.exp(m - m_new)
        p = jnp.exp(s - m_new)
        l = alpha * l + jnp.sum(p, axis=-1, keepdims=True)
        acc = alpha * acc + jnp.dot(p.astype(BF16), vj, preferred_element_type=F32)
        return m_new, l, acc

    init = (jnp.full((tq, 1), -jnp.inf, F32), jnp.zeros((tq, 1), F32), jnp.zeros((tq, MLA_V), F32))
    carry = lax.fori_loop(0, qi, lambda j, c: block(j, c, False), init)
    m, l, acc = block(qi, carry, True)
    o_ref[0, 0] = acc / l


def _attn(q, k, v, tq):
    bsz, nh, seq, _ = q.shape
    return pl.pallas_call(
        functools.partial(_attn_kernel, tq=tq),
        grid=(bsz, nh, seq // tq),
        in_specs=[pl.BlockSpec((1, 1, tq, QK_PAD), lambda b, h, i: (b, h, i, 0)),
                  pl.BlockSpec((1, 1, seq, QK_PAD), lambda b, h, i: (b, h, 0, 0)),
                  pl.BlockSpec((1, 1, seq, MLA_V), lambda b, h, i: (b, h, 0, 0))],
        out_specs=pl.BlockSpec((1, 1, tq, MLA_V), lambda b, h, i: (b, h, i, 0)),
        out_shape=jax.ShapeDtypeStruct((bsz, nh, seq, MLA_V), F32),
        compiler_params=_params(("parallel", "parallel", "arbitrary")),
        name="attn",
    )(q, k, v)


def _rwprep_kernel(feat_ref, prev_ref, mu_ref, w0_ref, w2_ref, a0_ref, a2_ref, g2_ref, kk_ref, ka_ref,
                   bd_ref, r_o, k_o, v_o, kk_o, lr_o, ld_o, g_o, *, nt):
    i = pl.program_id(0)
    feat = feat_ref[...]
    tm = feat.shape[0]
    first = jnp.where(i % nt == 0, 0.0, 1.0)
    prev_row = prev_ref[7:8, :] * first
    rolled = pltpu.roll(feat, 1, axis=0)
    rid = lax.broadcasted_iota(jnp.int32, (tm, 1), 0)
    prev = jnp.where(rid == 0, prev_row, rolled)
    feat = feat + (prev - feat) * mu_ref[...]

    r = feat[:, :RWKV_W]
    k = feat[:, RWKV_W:2 * RWKV_W]
    v = feat[:, 2 * RWKV_W:3 * RWKV_W]
    wa = feat[:, 3 * RWKV_W:3 * RWKV_W + LANES]
    gl = feat[:, 3 * RWKV_W + LANES:]
    lane = lax.broadcasted_iota(jnp.int32, (tm, LANES), 1)
    wl_in = jnp.where(lane < DECAY_LORA, jnp.tanh(wa), 0.0)
    al_in = jnp.where(lane < DECAY_LORA, 0.0, wa)
    wpre = w0_ref[...] + _dot(wl_in, w2_ref[...])
    w = -(jnp.maximum(-wpre, 0.0) + jnp.log(1.0 + jnp.exp(-jnp.abs(wpre)))) - 0.5
    logd = -jnp.exp(w)
    a = jax.nn.sigmoid(a0_ref[...] + _dot(al_in, a2_ref[...]))
    g = _dot(jax.nn.sigmoid(gl), g2_ref[...])
    kk = k * kk_ref[...]
    ss = _dot_split(kk * kk, bd_ref[...], split_lhs=True)
    r_o[...] = r
    k_o[...] = k * (1.0 + (a - 1.0) * ka_ref[...])
    v_o[...] = v
    kk_o[...] = kk * lax.rsqrt(ss + 1e-12)
    lr_o[...] = a
    ld_o[...] = logd
    g_o[...] = g


def _rwprep(feat, rw_mu, rw_w0, rw_w2, rw_a0, rw_a2, rw_g2, rw_k_k, rw_k_a, bsz, seq, tm):
    t = feat.shape[0]
    nt = seq // tm
    w2p = jnp.concatenate([rw_w2, jnp.zeros((LANES - DECAY_LORA, RWKV_W), F32)], axis=0).astype(BF16)
    a2p = jnp.concatenate([jnp.zeros((DECAY_LORA, RWKV_W), F32), rw_a2], axis=0).astype(BF16)
    args = (feat, feat, rw_mu[None, :], rw_w0[None, :], w2p, rw_a0[None, :], a2p, rw_g2.astype(BF16),
            rw_k_k[None, :], rw_k_a[None, :], _head_indicator(RWKV_W))
    full = lambda a: pl.BlockSpec(a.shape, lambda i: (0,) * a.ndim)
    rows8 = tm // 8
    in_specs = [pl.BlockSpec((tm, RWKV_IN), lambda i: (i, 0)),
                pl.BlockSpec((8, RWKV_IN), lambda i: (jnp.maximum(i * rows8 - 1, 0), 0))]
    in_specs += [full(a) for a in args[2:]]
    hshape = jax.ShapeDtypeStruct((t, RWKV_W), F32)
    hspec = pl.BlockSpec((tm, RWKV_W), lambda i: (i, 0))
    return pl.pallas_call(
        functools.partial(_rwprep_kernel, nt=nt),
        grid=(t // tm,),
        in_specs=in_specs,
        out_specs=[hspec] * 7,
        out_shape=[hshape] * 7,
        compiler_params=_params(("parallel",)),
        name="rwprep",
    )(*args)


def _rwscan_kernel(r_ref, k_ref, v_ref, kk_ref, lr_ref, ld_ref, g_ref, rk_ref, lng_ref, lnb_ref,
                   tri_ref, bd_ref, o_ref, state, *, chunk):
    c = pl.program_id(1)

    @pl.when(c == 0)
    def _():
        state[...] = jnp.zeros_like(state)

    C = chunk
    npairs = RWKV_W // LANES
    r = r_ref[0]
    k = k_ref[0]
    v = v_ref[0]
    kk = kk_ref[0]
    ld = ld_ref[0]
    bd = bd_ref[...]

    cum = _dot_split(tri_ref[...], ld, split_lhs=False)
    mid = C // 2 - 1
    cmid = cum[mid:mid + 1, :]
    cend = cum[C - 1:C, :]
    e_pos = jnp.exp(cum - cmid)
    e_prev = jnp.exp(cum - ld - cmid)
    e_neg = jnp.exp(cmid - cum)
    g_mid = jnp.exp(cmid)
    g_end_rel = jnp.exp(cend - cmid)
    g_end = jnp.exp(cend)
    rt = r * e_pos
    at = -kk * e_prev
    bt = kk * lr_ref[0] * e_neg
    kt = k * e_neg
    at_g = (at * g_mid).astype(BF16)
    rt_g = (rt * g_mid).astype(BF16)
    art = jnp.concatenate([at, rt], axis=0)
    bk = jnp.concatenate([bt, kt], axis=0).astype(BF16)
    bkg = jnp.concatenate([bt * g_end_rel, kt * g_end_rel], axis=0).astype(BF16)
    vb = v.astype(BF16)

    row = lax.broadcasted_iota(jnp.int32, (C, C), 0)
    col = lax.broadcasted_iota(jnp.int32, (C, C), 1)
    strict = row > col
    eye = (row == col).astype(F32)
    row2 = lax.broadcasted_iota(jnp.int32, (C, 2 * C), 0)
    col2 = lax.broadcasted_iota(jnp.int32, (C, 2 * C), 1)
    incl2 = row2 >= jnp.where(col2 >= C, col2 - C, col2)
    low2 = lax.broadcasted_iota(jnp.int32, (2 * C, LANES), 1) < RWKV_N
    low = lax.broadcasted_iota(jnp.int32, (C, LANES), 1) < RWKV_N
    srow = lax.broadcasted_iota(jnp.int32, (LANES, LANES), 0)
    scol = lax.broadcasted_iota(jnp.int32, (LANES, LANES), 1)
    same_head = (srow < RWKV_N) == (scol < RWKV_N)
    diag = srow == scol
    pair = lambda a, p: a[:, p * LANES:(p + 1) * LANES]

    a_ab, a_ak, a_rbk = [], [], []
    for p in range(npairs):
        art_p = pair(art, p)
        bk_p = pair(bk, p)
        for half in range(2):
            lhs = jnp.where(low2, art_p, 0.0) if half == 0 else jnp.where(low2, 0.0, art_p)
            big = _dot_nt(lhs, bk_p)
            a_ab.append(jnp.where(strict, big[:C, :C], 0.0))
            a_ak.append(jnp.where(strict, big[:C, C:], 0.0).astype(BF16))
            a_rbk.append(jnp.where(incl2, big[C:, :], 0.0).astype(BF16))

    ps = [a.astype(BF16) for a in a_ab]
    tinv = [eye + a for a in a_ab]
    n = 2
    while n < C:
        sq = [jnp.dot(pb, pb, preferred_element_type=F32) for pb in ps]
        ps = [s.astype(BF16) for s in sq]
        tinv = [t + jnp.dot(t.astype(BF16), pb, preferred_element_type=F32) for t, pb in zip(tinv, ps)]
        n *= 2
    tinv = [t.astype(BF16) for t in tinv]

    s0 = [state[p] for p in range(npairs)]
    s0b = [s.astype(BF16) for s in s0]
    sel = lambda h0, h1: jnp.where(low, h0, h1)
    mm = lambda a, b: jnp.dot(a, b, preferred_element_type=F32)
    xs = [mm(pair(at_g, p), s0b[p]) + sel(mm(a_ak[2 * p], pair(vb, p)), mm(a_ak[2 * p + 1], pair(vb, p)))
          for p in range(npairs)]
    xb = [x.astype(BF16) for x in xs]
    us = [sel(mm(tinv[2 * p], xb[p]), mm(tinv[2 * p + 1], xb[p])) for p in range(npairs)]
    uv = [jnp.concatenate([us[p].astype(BF16), pair(vb, p)], axis=0) for p in range(npairs)]
    ys = [mm(pair(rt_g, p), s0b[p]) + sel(mm(a_rbk[2 * p], uv[p]), mm(a_rbk[2 * p + 1], uv[p]))
          for p in range(npairs)]
    for p in range(npairs):
        upd = lax.dot_general(pair(bkg, p), uv[p], (((0,), (0,)), ((), ())), preferred_element_type=F32)
        g_col = jnp.sum(jnp.where(diag, pair(g_end, p), 0.0), axis=-1, keepdims=True)
        state[p] = s0[p] * g_col + jnp.where(same_head, upd, 0.0)

    inv_n = 1.0 / RWKV_N
    for p in range(npairs):
        sl = slice(p * LANES, (p + 1) * LANES)
        y = ys[p]
        mean = _dot_split(y, bd, split_lhs=True) * inv_n
        yc = y - mean
        var = _dot_split(yc * yc, bd, split_lhs=True) * inv_n
        yn = yc * lax.rsqrt(var + LNX_EPS) * lng_ref[:, sl] + lnb_ref[:, sl]
        bonus = _dot_split(pair(r, p) * pair(k, p) * rk_ref[:, sl], bd, split_lhs=True) * pair(v, p)
        o_ref[0, :, sl] = (yn + bonus) * g_ref[0, :, sl]


def _rwscan(r, k, v, kk, lr, ld, g, rw_r_k, rw_lnx_g, rw_lnx_b, bsz, seq, chunk):
    width = r.shape[-1]
    tri = jnp.asarray(np.tril(np.ones((chunk, chunk), np.float32))).astype(BF16)
    bd = _head_indicator(LANES)
    seq3 = lambda a: a.reshape(bsz, seq, width)
    cspec = pl.BlockSpec((1, chunk, width), lambda b, c: (b, c, 0))
    small = pl.BlockSpec((1, width), lambda b, c: (0, 0))
    return pl.pallas_call(
        functools.partial(_rwscan_kernel, chunk=chunk),
        grid=(bsz, seq // chunk),
        in_specs=[cspec] * 7 + [small] * 3 + [pl.BlockSpec((chunk, chunk), lambda b, c: (0, 0)),
                                              pl.BlockSpec((LANES, LANES), lambda b, c: (0, 0))],
        out_specs=cspec,
        out_shape=jax.ShapeDtypeStruct((bsz, seq, width), F32),
        scratch_shapes=[pltpu.VMEM((width // LANES, LANES, LANES), F32)],
        compiler_params=_params(("parallel", "arbitrary")),
        name="rwscan",
    )(seq3(r), seq3(k), seq3(v), seq3(kk), seq3(lr), seq3(ld), seq3(g), rw_r_k.reshape(1, width),
      rw_lnx_g[None, :], rw_lnx_b[None, :], tri, bd)


def _outproj_kernel(x_ref, mla_ref, rw_ref, gmo_ref, wo_ref, worw_ref, gffn_ref, wr_ref, br_ref,
                    x1_ref, xn_ref, comb_ref):
    tm = x_ref.shape[0]
    ss = jnp.zeros((tm, 1), F32)
    for hh in range(MLA_HEADS):
        m = mla_ref[0, hh]
        ss = ss + jnp.sum(m * m, axis=-1, keepdims=True)
    s = lax.rsqrt(ss * (1.0 / MLA_W) + NORM_EPS)
    acc = x_ref[...]
    for hh in range(MLA_HEADS):
        acc = acc + _dot(mla_ref[0, hh] * s * gmo_ref[hh], wo_ref[hh])
    acc = acc + _dot(rw_ref[...], worw_ref[...])
    x1_ref[...] = acc
    xn = acc * lax.rsqrt(jnp.mean(acc * acc, axis=-1, keepdims=True) + NORM_EPS) * gffn_ref[...]
    xn_ref[...] = xn.astype(BF16)

    logits = jnp.dot(xn, wr_ref[...], preferred_element_type=F32, precision=HIGHEST) + br_ref[...]
    lane = lax.broadcasted_iota(jnp.int32, (tm, LANES), 1)
    neg = -jnp.inf
    big = jnp.int32(1 << 20)
    is_g = (lane >= N_EXPERTS) & (lane < N_EXPERTS + N_GROUPS)
    glog = jnp.where(is_g, logits, neg)
    gmax = jnp.max(glog, axis=-1, keepdims=True)
    gsel = jnp.min(jnp.where(glog == gmax, lane, big), axis=-1, keepdims=True) - N_EXPERTS
    gprob = 1.0 / jnp.sum(jnp.exp(glog - gmax), axis=-1, keepdims=True)
    lo = gsel * EXPERTS_PER_GROUP
    in_group = (lane >= lo) & (lane < lo + EXPERTS_PER_GROUP)
    elog = jnp.where(in_group, logits, neg)
    v1 = jnp.max(elog, axis=-1, keepdims=True)
    i1 = jnp.min(jnp.where(elog == v1, lane, big), axis=-1, keepdims=True)
    elog2 = jnp.where(lane == i1, neg, elog)
    v2 = jnp.max(elog2, axis=-1, keepdims=True)
    i2 = jnp.min(jnp.where(elog2 == v2, lane, big), axis=-1, keepdims=True)
    e2 = jnp.exp(v2 - v1)
    w1 = gprob / (1.0 + e2)
    w2 = gprob * e2 / (1.0 + e2)
    comb_ref[...] = jnp.where(lane == i1, w1, 0.0) + jnp.where(lane == i2, w2, 0.0)


def _outproj(x2, mla, rw, g_mla_out, w_o, g_ffn, w_group, b_group, w_expert, b_expert, seq, tm):
    t = x2.shape[0]
    nt = seq // tm
    nh = MLA_HEADS + RWKV_HEADS
    wo = w_o[:MLA_W].reshape(MLA_HEADS, MLA_V, D_MODEL).astype(BF16)
    wo_rw = w_o[MLA_W:].astype(BF16)
    pad = LANES - N_EXPERTS - N_GROUPS
    wr = jnp.concatenate([w_expert, w_group, jnp.zeros((D_MODEL, pad), F32)], axis=-1)
    br = jnp.concatenate([b_expert, b_group, jnp.zeros((pad,), F32)])[None, :]
    gmo = g_mla_out.reshape(MLA_HEADS, 1, MLA_V)
    head_map = lambda i: (i // nt, 0, i % nt, 0)
    full = lambda a: pl.BlockSpec(a.shape, lambda i: (0,) * a.ndim)
    row = pl.BlockSpec((tm, D_MODEL), lambda i: (i, 0))
    return pl.pallas_call(
        _outproj_kernel,
        grid=(t // tm,),
        in_specs=[row,
                  pl.BlockSpec((1, MLA_HEADS, tm, MLA_V), head_map),
                  pl.BlockSpec((tm, RWKV_W), lambda i: (i, 0)),
                  full(gmo), full(wo), full(wo_rw), pl.BlockSpec((1, D_MODEL), lambda i: (0, 0)), full(wr), full(br)],
        out_specs=[row, row, pl.BlockSpec((tm, LANES), lambda i: (i, 0))],
        out_shape=[jax.ShapeDtypeStruct((t, D_MODEL), F32),
                   jax.ShapeDtypeStruct((t, D_MODEL), BF16),
                   jax.ShapeDtypeStruct((t, LANES), F32)],
        compiler_params=_params(("parallel",)),
        name="outproj",
    )(x2, mla, rw, gmo, wo, wo_rw, g_ffn[None, :], wr, br)


def _moe_kernel(x1_ref, xn_ref, comb_ref, wg_ref, wu_ref, wd_ref, o_ref):
    e = pl.program_id(1)

    @pl.when(e == 0)
    def _():
        o_ref[...] = x1_ref[...]

    xn = xn_ref[...]
    hg = jnp.dot(xn, wg_ref[0], preferred_element_type=F32)
    hu = jnp.dot(xn, wu_ref[0], preferred_element_type=F32)
    hid = (hg * jax.nn.sigmoid(hg) * hu).astype(BF16)
    y = jnp.dot(hid, wd_ref[0], preferred_element_type=F32)
    lane = lax.broadcasted_iota(jnp.int32, comb_ref.shape, 1)
    ce = jnp.sum(jnp.where(lane == e, comb_ref[...], 0.0), axis=-1, keepdims=True)
    o_ref[...] += ce * y


def _moe(x1, xn, comb, w_gate, w_up, w_down, tm):
    t = x1.shape[0]
    row = lambda w: pl.BlockSpec((tm, w), lambda i, e: (i, 0))
    return pl.pallas_call(
        _moe_kernel,
        grid=(t // tm, N_EXPERTS),
        in_specs=[row(D_MODEL), row(D_MODEL), row(LANES),
                  pl.BlockSpec((1, D_MODEL, D_EXPERT), lambda i, e: (e, 0, 0)),
                  pl.BlockSpec((1, D_MODEL, D_EXPERT), lambda i, e: (e, 0, 0)),
                  pl.BlockSpec((1, D_EXPERT, D_MODEL), lambda i, e: (e, 0, 0))],
        out_specs=row(D_MODEL),
        out_shape=jax.ShapeDtypeStruct((t, D_MODEL), F32),
        compiler_params=_params(("parallel", "arbitrary")),
        name="moe",
    )(x1, xn, comb, w_gate.astype(BF16), w_up.astype(BF16), w_down.astype(BF16))


def _tile(seq, want):
    tm = min(seq, want)
    assert seq % tm == 0 and tm % 8 == 0
    return tm


def _layer(x, positions, g_mix, w_in, g_cq, w_uq, g_ckv, w_uk, w_uv, g_qn, g_kn, g_mla_out,
           rw_mu, rw_w0, rw_w2, rw_a0, rw_a2, rw_g2, rw_k_k, rw_k_a, rw_r_k, rw_lnx_g, rw_lnx_b,
           w_o, g_ffn, w_group, b_group, w_expert, b_expert, w_gate, w_up, w_down):
    bsz, seq, d = x.shape
    t = bsz * seq
    x2 = x.reshape(t, d)
    pos2 = positions.reshape(t, 1)
    feat, q, k, v = _inproj(x2, pos2, g_mix, w_in, g_cq, w_uq, g_ckv, w_uk, w_uv, g_qn, g_kn,
                            bsz, seq, _tile(seq, 512))
    mla = _attn(q, k, v, _tile(seq, 512))
    r, k2, vv, kk, lr, ld, g = _rwprep(feat, rw_mu, rw_w0, rw_w2, rw_a0, rw_a2, rw_g2, rw_k_k, rw_k_a,
                                       bsz, seq, _tile(seq, 512))
    rw = _rwscan(r, k2, vv, kk, lr, ld, g, rw_r_k, rw_lnx_g, rw_lnx_b, bsz, seq, _tile(seq, 128))
    rw = rw.reshape(t, RWKV_W)
    x1, xn, comb = _outproj(x2, mla, rw, g_mla_out, w_o, g_ffn, w_group, b_group, w_expert, b_expert,
                            seq, _tile(seq, 512))
    out = _moe(x1, xn, comb, w_gate, w_up, w_down, _tile(t, 1024))
    return out.reshape(bsz, seq, d)


def kernel(x, positions, g_mix, w_in, g_cq, w_uq, g_ckv, w_uk, w_uv, g_qn, g_kn, g_mla_out, rw_mu, rw_w0, rw_w2, rw_a0, rw_a2, rw_g2, rw_k_k, rw_k_a, rw_r_k, rw_lnx_g, rw_lnx_b, w_o, g_ffn, w_group, b_group, w_expert, b_expert, w_gate, w_up, w_down):
    for l in range(g_mix.shape[0]):
        x = _layer(x, positions, g_mix[l], w_in[l], g_cq[l], w_uq[l], g_ckv[l], w_uk[l], w_uv[l],
                   g_qn[l], g_kn[l], g_mla_out[l], rw_mu[l], rw_w0[l], rw_w2[l], rw_a0[l], rw_a2[l],
                   rw_g2[l], rw_k_k[l], rw_k_a[l], rw_r_k[l], rw_lnx_g[l], rw_lnx_b[l], w_o[l],
                   g_ffn[l], w_group[l], b_group[l], w_expert[l], b_expert[l], w_gate[l], w_up[l],
                   w_down[l])
    return x
```

```python
import functools

import numpy as np
import jax
import jax.numpy as jnp
from jax import lax
from jax.experimental import pallas as pl
from jax.experimental.pallas import tpu as pltpu

F32 = jnp.float32
BF16 = jnp.bfloat16

D_MODEL = 1024
MLA_HEADS = 8
MLA_NOPE = 64
MLA_ROPE = 32
MLA_QK = MLA_NOPE + MLA_ROPE
MLA_V = 64
MLA_W = MLA_HEADS * MLA_V
MLA_Q_RANK = 256
MLA_KV_RANK = 128
ROPE_THETA = 10000.0
RWKV_HEADS = 8
RWKV_N = 64
RWKV_W = RWKV_HEADS * RWKV_N
DECAY_LORA = 64
AAA_LORA = 64
GATE_LORA = 128
MLA_IN = MLA_Q_RANK + MLA_KV_RANK + MLA_ROPE
RWKV_IN = 3 * RWKV_W + DECAY_LORA + AAA_LORA + GATE_LORA
N_GROUPS = 4
EXPERTS_PER_GROUP = 8
N_EXPERTS = N_GROUPS * EXPERTS_PER_GROUP
D_EXPERT = 256
NORM_EPS = 1e-6
LNX_EPS = 64e-5

LANES = 128
QK_PAD = LANES
Q_BLK = 512
KV_BLK = 512
ATT_HEADS = 4
VT_ROWS = MLA_V + 16
Q_SCALE = MLA_QK ** -0.5 * float(np.log2(np.e))
PM_W = MLA_Q_RANK + MLA_KV_RANK + 2 * LANES
VMEM_LIMIT = 56 * 1024 * 1024


def _dot_bf16x3(a, b):
    a_hi = a.astype(BF16)
    a_lo = (a - a_hi.astype(F32)).astype(BF16)
    b_hi = b.astype(BF16)
    b_lo = (b - b_hi.astype(F32)).astype(BF16)
    d = lambda p, q: jnp.dot(p, q, preferred_element_type=F32)
    return d(a_hi, b_hi) + (d(a_hi, b_lo) + d(a_lo, b_hi))


def _dot(a, b):
    return jnp.dot(a.astype(BF16), b.astype(BF16), preferred_element_type=F32)


def _dot_nt(a, b):
    return lax.dot_general(a.astype(BF16), b.astype(BF16), (((1,), (1,)), ((), ())),
                           preferred_element_type=F32)


def _dot_tn(a, b):
    return lax.dot_general(a.astype(BF16), b.astype(BF16), (((0,), (0,)), ((), ())),
                           preferred_element_type=F32)


def _dot_split(a, b, split_lhs, passes=2):
    s = a if split_lhs else b
    out = None
    for _ in range(passes):
        piece = s.astype(BF16)
        term = (jnp.dot(piece, b, preferred_element_type=F32) if split_lhs
                else jnp.dot(a, piece, preferred_element_type=F32))
        out = term if out is None else out + term
        s = s - piece.astype(F32)
    return out


def _head_indicator(width):
    hid = np.arange(width) // RWKV_N
    return jnp.asarray((hid[:, None] == hid[None, :]).astype(np.float32)).astype(BF16)


def _params(sem):
    return pltpu.CompilerParams(dimension_semantics=sem, vmem_limit_bytes=VMEM_LIMIT)


def _inproj_kernel(x_ref, pos_ref, gmix_ref, wmla_ref, wrw_ref, freq_ref, gcq_ref, gckv_ref,
                   wq_ref, wqr_ref, wk_ref, wv_ref, gq_ref, gqr_ref, gk_ref, gkr_ref,
                   feat_ref, q_ref, k_ref, v_ref):
    x = x_ref[...]
    h = x * lax.rsqrt(jnp.mean(x * x, axis=-1, keepdims=True) + NORM_EPS) * gmix_ref[...]
    hb = h.astype(BF16)
    feat_ref[...] = jnp.dot(hb, wrw_ref[...], preferred_element_type=F32)
    pm = jnp.dot(hb, wmla_ref[...], preferred_element_type=F32)

    c_q = pm[:, :MLA_Q_RANK]
    c_kv = pm[:, MLA_Q_RANK:MLA_Q_RANK + MLA_KV_RANK]
    kr = pm[:, MLA_Q_RANK + MLA_KV_RANK:MLA_Q_RANK + MLA_KV_RANK + LANES]
    kr_rot = pm[:, MLA_Q_RANK + MLA_KV_RANK + LANES:PM_W]
    cqn = (c_q * lax.rsqrt(jnp.mean(c_q * c_q, axis=-1, keepdims=True) + NORM_EPS)
           * gcq_ref[...]).astype(BF16)
    ckvn = (c_kv * lax.rsqrt(jnp.mean(c_kv * c_kv, axis=-1, keepdims=True) + NORM_EPS)
            * gckv_ref[...]).astype(BF16)

    ang = pos_ref[...].astype(F32) * freq_ref[...]
    cosf = jnp.cos(ang)
    sinf = jnp.sin(ang)
    q_cos = gq_ref[...] * cosf * Q_SCALE
    q_sin = gqr_ref[...] * sinf * Q_SCALE
    k_cos = gk_ref[...] * cosf
    k_sin = kr_rot * (gkr_ref[...] * sinf)
    inv_qk = 1.0 / MLA_QK
    ones_rows = jnp.where(lax.broadcasted_iota(jnp.int32, (VT_ROWS - MLA_V, x.shape[0]), 0) == 0, 1.0, 0.0)

    for hh in range(MLA_HEADS):
        q_raw = jnp.dot(cqn, wq_ref[hh], preferred_element_type=F32)
        q_rot = jnp.dot(cqn, wqr_ref[hh], preferred_element_type=F32)
        sq = lax.rsqrt(jnp.sum(q_raw * q_raw, axis=-1, keepdims=True) * inv_qk + NORM_EPS)
        q_ref[0, hh] = (sq * (q_raw * q_cos + q_rot * q_sin)).astype(BF16)
        k_raw = jnp.dot(ckvn, wk_ref[hh], preferred_element_type=F32) + kr
        sk = lax.rsqrt(jnp.sum(k_raw * k_raw, axis=-1, keepdims=True) * inv_qk + NORM_EPS)
        k_ref[0, hh] = (sk * (k_raw * k_cos + k_sin)).astype(BF16)
        vt = lax.dot_general(wv_ref[hh], ckvn, (((1,), (1,)), ((), ())), preferred_element_type=F32)
        v_ref[0, hh, 0] = jnp.concatenate([vt, ones_rows], axis=0).astype(BF16)


def _rot_cols(w):
    half = MLA_ROPE // 2
    z = jnp.zeros_like(w[..., :MLA_NOPE])
    return jnp.concatenate([z, -w[..., MLA_NOPE + half:MLA_QK], w[..., MLA_NOPE:MLA_NOPE + half]], axis=-1)


def _pad_last(w, n):
    return jnp.pad(w, [(0, 0)] * (w.ndim - 1) + [(0, n - w.shape[-1])])


def _inproj(x2, pos2, g_mix, w_in, g_cq, w_uq, g_ckv, w_uk, w_uv, g_qn, g_kn, bsz, seq, tm, kv_blk):
    t = x2.shape[0]
    nt = seq // tm
    per_kv = kv_blk // tm
    half = MLA_ROPE // 2
    w_kr = w_in[:, MLA_Q_RANK + MLA_KV_RANK:MLA_IN]
    zeros64 = jnp.zeros((D_MODEL, MLA_NOPE), F32)
    zeros32 = jnp.zeros((D_MODEL, LANES - MLA_QK), F32)
    w_kr_rot = jnp.concatenate([-w_kr[:, half:], w_kr[:, :half]], axis=-1)
    w_mla = jnp.concatenate([w_in[:, :MLA_Q_RANK + MLA_KV_RANK], zeros64, w_kr, zeros32,
                             zeros64, w_kr_rot, zeros32], axis=-1).astype(BF16)
    w_rw = w_in[:, MLA_IN:].astype(BF16)
    wq = jnp.transpose(w_uq, (1, 0, 2))
    wq_rot = _pad_last(_rot_cols(wq), QK_PAD).astype(BF16)
    wq = _pad_last(wq, QK_PAD).astype(BF16)
    wk = _pad_last(jnp.transpose(w_uk, (1, 0, 2)), QK_PAD).astype(BF16)
    wv = jnp.transpose(w_uv, (1, 2, 0)).astype(BF16)
    inv_freq = 1.0 / (ROPE_THETA ** (jnp.arange(half, dtype=F32) / half))
    freq = jnp.concatenate([jnp.zeros((MLA_NOPE,), F32), inv_freq, inv_freq,
                            jnp.zeros((LANES - MLA_QK,), F32)])[None, :]
    swap = lambda g: jnp.concatenate([jnp.zeros((MLA_NOPE,), F32), g[MLA_NOPE + half:MLA_QK],
                                      g[MLA_NOPE:MLA_NOPE + half], jnp.zeros((LANES - MLA_QK,), F32)])[None, :]
    gq = _pad_last(g_qn[None, :], QK_PAD)
    gk = _pad_last(g_kn[None, :], QK_PAD)
    gq_rot = swap(g_qn)
    gk_rot = swap(g_kn)

    full = lambda a: pl.BlockSpec(a.shape, lambda i: (0,) * a.ndim)
    head_map = lambda i: (i // nt, 0, i % nt, 0)
    args = (x2, pos2, g_mix[None, :], w_mla, w_rw, freq, g_cq[None, :], g_ckv[None, :],
            wq, wq_rot, wk, wv, gq, gq_rot, gk, gk_rot)
    in_specs = [pl.BlockSpec((tm, D_MODEL), lambda i: (i, 0)),
                pl.BlockSpec((tm, 1), lambda i: (i, 0))] + [full(a) for a in args[2:]]
    return pl.pallas_call(
        _inproj_kernel,
        grid=(t // tm,),
        in_specs=in_specs,
        out_specs=[pl.BlockSpec((tm, RWKV_IN), lambda i: (i, 0)),
                   pl.BlockSpec((1, MLA_HEADS, tm, QK_PAD), head_map),
                   pl.BlockSpec((1, MLA_HEADS, tm, QK_PAD), head_map),
                   pl.BlockSpec((1, MLA_HEADS, 1, VT_ROWS, tm),
                                lambda i: (i // nt, 0, (i % nt) // per_kv, 0, (i % nt) % per_kv))],
        out_shape=[jax.ShapeDtypeStruct((t, RWKV_IN), F32),
                   jax.ShapeDtypeStruct((bsz, MLA_HEADS, seq, QK_PAD), BF16),
                   jax.ShapeDtypeStruct((bsz, MLA_HEADS, seq, QK_PAD), BF16),
                   jax.ShapeDtypeStruct((bsz, MLA_HEADS, seq // kv_blk, VT_ROWS, kv_blk), BF16)],
        compiler_params=_params(("parallel",)),
        name="inproj",
    )(*args)


def _attn_kernel(q_ref, k_ref, vt_ref, o_ref):
    qi = pl.program_id(2)
    nh = q_ref.shape[1]
    qb = q_ref.shape[2]
    kvb = vt_ref.shape[4]
    qs = [q_ref[0, h] for h in range(nh)]

    def group(g, carry, masked):
        start = pl.multiple_of(g * kvb, kvb)
        ss = [lax.dot_general(k_ref[0, h, pl.ds(start, kvb), :], qs[h], (((1,), (1,)), ((), ())),
                              preferred_element_type=F32) for h in range(nh)]
        if masked:
            key = g * kvb + lax.broadcasted_iota(jnp.int32, (kvb, qb), 0)
            qry = qi * qb + lax.broadcasted_iota(jnp.int32, (kvb, qb), 1)
            ss = [jnp.where(key <= qry, s, -jnp.inf) for s in ss]
        ms = [jnp.maximum(carry[h][0], jnp.max(ss[h], axis=0, keepdims=True)) for h in range(nh)]
        ps = [jnp.exp2((ss[h] - ms[h]).astype(BF16)) for h in range(nh)]
        out = []
        for h in range(nh):
            m, acc = carry[h]
            alpha = jnp.exp2(m - ms[h])
            out.append((ms[h], alpha * acc + jnp.dot(vt_ref[0, h, g], ps[h], preferred_element_type=F32)))
        return tuple(out)

    init = tuple((jnp.full((1, qb), -1e30, F32), jnp.zeros((VT_ROWS, qb), F32)) for _ in range(nh))
    full_groups = (qi * qb) // kvb
    carry = lax.fori_loop(0, full_groups, lambda g, c: group(g, c, False), init)
    carry = group(full_groups, carry, True)
    ot = jnp.concatenate([acc[:MLA_V] * (1.0 / acc[MLA_V:MLA_V + 1]) for _, acc in carry], axis=0)
    o_ref[0] = ot.T


def _attn(q, k, vt):
    bsz, nh, seq, _ = q.shape
    per_step = ATT_HEADS
    ngrp, kvb = vt.shape[2], vt.shape[4]
    qb = min(Q_BLK, kvb)
    return pl.pallas_call(
        _attn_kernel,
        grid=(bsz, nh // per_step, seq // qb),
        in_specs=[pl.BlockSpec((1, per_step, qb, QK_PAD), lambda b, h, i: (b, h, i, 0)),
                  pl.BlockSpec((1, per_step, seq, QK_PAD), lambda b, h, i: (b, h, 0, 0)),
                  pl.BlockSpec((1, per_step, ngrp, VT_ROWS, kvb), lambda b, h, i: (b, h, 0, 0, 0))],
        out_specs=pl.BlockSpec((1, qb, per_step * MLA_V), lambda b, h, i: (b, i, h)),
        out_shape=jax.ShapeDtypeStruct((bsz, seq, nh * MLA_V), F32),
        compiler_params=_params(("parallel", "parallel", "arbitrary")),
        name="attn",
    )(q, k, vt)


def _rwprep_kernel(feat_ref, prev_ref, mu_ref, w0_ref, w2_ref, a0_ref, a2_ref, g2_ref, kk_ref, ka_ref,
                   bd_ref, r_o, k_o, v_o, kk_o, lr_o, ld_o, g_o, *, nt):
    i = pl.program_id(0)
    feat = feat_ref[...]
    tm = feat.shape[0]
    first = jnp.where(i % nt == 0, 0.0, 1.0)
    prev_row = prev_ref[7:8, :] * first
    rolled = pltpu.roll(feat, 1, axis=0)
    rid = lax.broadcasted_iota(jnp.int32, (tm, 1), 0)
    prev = jnp.where(rid == 0, prev_row, rolled)
    feat = feat + (prev - feat) * mu_ref[...]

    r = feat[:, :RWKV_W]
    k = feat[:, RWKV_W:2 * RWKV_W]
    v = feat[:, 2 * RWKV_W:3 * RWKV_W]
    wa = feat[:, 3 * RWKV_W:3 * RWKV_W + LANES]
    gl = feat[:, 3 * RWKV_W + LANES:]
    lane = lax.broadcasted_iota(jnp.int32, (tm, LANES), 1)
    wl_in = jnp.where(lane < DECAY_LORA, jnp.tanh(wa), 0.0)
    al_in = jnp.where(lane < DECAY_LORA, 0.0, wa)
    wpre = w0_ref[...] + _dot(wl_in, w2_ref[...])
    w = -(jnp.maximum(-wpre, 0.0) + jnp.log(1.0 + jnp.exp(-jnp.abs(wpre)))) - 0.5
    logd = -jnp.exp(w)
    a = jax.nn.sigmoid(a0_ref[...] + _dot(al_in, a2_ref[...]))
    g = _dot(jax.nn.sigmoid(gl), g2_ref[...])
    kk = k * kk_ref[...]
    ss = _dot_split(kk * kk, bd_ref[...], split_lhs=True)
    r_o[...] = r
    k_o[...] = k * (1.0 + (a - 1.0) * ka_ref[...])
    v_o[...] = v
    kk_o[...] = kk * lax.rsqrt(ss + 1e-12)
    lr_o[...] = a
    ld_o[...] = logd
    g_o[...] = g


def _rwprep(feat, rw_mu, rw_w0, rw_w2, rw_a0, rw_a2, rw_g2, rw_k_k, rw_k_a, bsz, seq, tm):
    t = feat.shape[0]
    nt = seq // tm
    w2p = jnp.concatenate([rw_w2, jnp.zeros((LANES - DECAY_LORA, RWKV_W), F32)], axis=0).astype(BF16)
    a2p = jnp.concatenate([jnp.zeros((DECAY_LORA, RWKV_W), F32), rw_a2], axis=0).astype(BF16)
    args = (feat, feat, rw_mu[None, :], rw_w0[None, :], w2p, rw_a0[None, :], a2p, rw_g2.astype(BF16),
            rw_k_k[None, :], rw_k_a[None, :], _head_indicator(RWKV_W))
    full = lambda a: pl.BlockSpec(a.shape, lambda i: (0,) * a.ndim)
    rows8 = tm // 8
    in_specs = [pl.BlockSpec((tm, RWKV_IN), lambda i: (i, 0)),
                pl.BlockSpec((8, RWKV_IN), lambda i: (jnp.maximum(i * rows8 - 1, 0), 0))]
    in_specs += [full(a) for a in args[2:]]
    hshape = jax.ShapeDtypeStruct((t, RWKV_W), F32)
    hspec = pl.BlockSpec((tm, RWKV_W), lambda i: (i, 0))
    return pl.pallas_call(
        functools.partial(_rwprep_kernel, nt=nt),
        grid=(t // tm,),
        in_specs=in_specs,
        out_specs=[hspec] * 7,
        out_shape=[hshape] * 7,
        compiler_params=_params(("parallel",)),
        name="rwprep",
    )(*args)


def _rwscan_kernel(r_ref, k_ref, v_ref, kk_ref, lr_ref, ld_ref, g_ref, rk_ref, lng_ref, lnb_ref,
                   tri_ref, bd_ref, o_ref, state, *, chunk):
    c = pl.program_id(1)

    @pl.when(c == 0)
    def _():
        state[...] = jnp.zeros_like(state)

    C = chunk
    npairs = RWKV_W // LANES
    r = r_ref[0]
    k = k_ref[0]
    v = v_ref[0]
    kk = kk_ref[0]
    ld = ld_ref[0]
    bd = bd_ref[...]

    cum = _dot_split(tri_ref[...], ld, split_lhs=False)
    mid = C // 2 - 1
    cmid = cum[mid:mid + 1, :]
    cend = cum[C - 1:C, :]
    e_pos = jnp.exp(cum - cmid)
    e_prev = jnp.exp(cum - ld - cmid)
    e_neg = jnp.exp(cmid - cum)
    g_mid = jnp.exp(cmid)
    g_end_rel = jnp.exp(cend - cmid)
    g_end = jnp.exp(cend)
    rt = r * e_pos
    at = -kk * e_prev
    bt = kk * lr_ref[0] * e_neg
    kt = k * e_neg
    at_g = (at * g_mid).astype(BF16)
    rt_g = (rt * g_mid).astype(BF16)
    art = jnp.concatenate([at, rt], axis=0)
    bk = jnp.concatenate([bt, kt], axis=0).astype(BF16)
    bkg = jnp.concatenate([bt * g_end_rel, kt * g_end_rel], axis=0).astype(BF16)
    vb = v.astype(BF16)

    row = lax.broadcasted_iota(jnp.int32, (C, C), 0)
    col = lax.broadcasted_iota(jnp.int32, (C, C), 1)
    strict = row > col
    eye = (row == col).astype(F32)
    row2 = lax.broadcasted_iota(jnp.int32, (C, 2 * C), 0)
    col2 = lax.broadcasted_iota(jnp.int32, (C, 2 * C), 1)
    incl2 = row2 >= jnp.where(col2 >= C, col2 - C, col2)
    low2 = lax.broadcasted_iota(jnp.int32, (2 * C, LANES), 1) < RWKV_N
    low = lax.broadcasted_iota(jnp.int32, (C, LANES), 1) < RWKV_N
    srow = lax.broadcasted_iota(jnp.int32, (LANES, LANES), 0)
    scol = lax.broadcasted_iota(jnp.int32, (LANES, LANES), 1)
    same_head = (srow < RWKV_N) == (scol < RWKV_N)
    diag = srow == scol
    pair = lambda a, p: a[:, p * LANES:(p + 1) * LANES]

    a_ab, a_ak, a_rbk = [], [], []
    for p in range(npairs):
        art_p = pair(art, p)
        bk_p = pair(bk, p)
        for half in range(2):
            lhs = jnp.where(low2, art_p, 0.0) if half == 0 else jnp.where(low2, 0.0, art_p)
            big = _dot_nt(lhs, bk_p)
            a_ab.append(jnp.where(strict, big[:C, :C], 0.0))
            a_ak.append(jnp.where(strict, big[:C, C:], 0.0).astype(BF16))
            a_rbk.append(jnp.where(incl2, big[C:, :], 0.0).astype(BF16))

    ps = [a.astype(BF16) for a in a_ab]
    tinv = [eye + a for a in a_ab]
    n = 2
    while n < C:
        sq = [jnp.dot(pb, pb, preferred_element_type=F32) for pb in ps]
        ps = [s.astype(BF16) for s in sq]
        tinv = [t + jnp.dot(t.astype(BF16), pb, preferred_element_type=F32) for t, pb in zip(tinv, ps)]
        n *= 2
    tinv = [t.astype(BF16) for t in tinv]

    s0 = [state[p] for p in range(npairs)]
    s0b = [s.astype(BF16) for s in s0]
    sel = lambda h0, h1: jnp.where(low, h0, h1)
    mm = lambda a, b: jnp.dot(a, b, preferred_element_type=F32)
    xs = [mm(pair(at_g, p), s0b[p]) + sel(mm(a_ak[2 * p], pair(vb, p)), mm(a_ak[2 * p + 1], pair(vb, p)))
          for p in range(npairs)]
    xb = [x.astype(BF16) for x in xs]
    us = [sel(mm(tinv[2 * p], xb[p]), mm(tinv[2 * p + 1], xb[p])) for p in range(npairs)]
    uv = [jnp.concatenate([us[p].astype(BF16), pair(vb, p)], axis=0) for p in range(npairs)]
    ys = [mm(pair(rt_g, p), s0b[p]) + sel(mm(a_rbk[2 * p], uv[p]), mm(a_rbk[2 * p + 1], uv[p]))
          for p in range(npairs)]
    for p in range(npairs):
        upd = lax.dot_general(pair(bkg, p), uv[p], (((0,), (0,)), ((), ())), preferred_element_type=F32)
        g_col = jnp.sum(jnp.where(diag, pair(g_end, p), 0.0), axis=-1, keepdims=True)
        state[p] = s0[p] * g_col + jnp.where(same_head, upd, 0.0)

    inv_n = 1.0 / RWKV_N
    for p in range(npairs):
        sl = slice(p * LANES, (p + 1) * LANES)
        y = ys[p]
        mean = _dot_split(y, bd, split_lhs=True) * inv_n
        yc = y - mean
        var = _dot_split(yc * yc, bd, split_lhs=True) * inv_n
        yn = yc * lax.rsqrt(var + LNX_EPS) * lng_ref[:, sl] + lnb_ref[:, sl]
        bonus = _dot_split(pair(r, p) * pair(k, p) * rk_ref[:, sl], bd, split_lhs=True) * pair(v, p)
        o_ref[0, :, sl] = (yn + bonus) * g_ref[0, :, sl]


def _rwscan(r, k, v, kk, lr, ld, g, rw_r_k, rw_lnx_g, rw_lnx_b, bsz, seq, chunk):
    width = r.shape[-1]
    tri = jnp.asarray(np.tril(np.ones((chunk, chunk), np.float32))).astype(BF16)
    bd = _head_indicator(LANES)
    seq3 = lambda a: a.reshape(bsz, seq, width)
    cspec = pl.BlockSpec((1, chunk, width), lambda b, c: (b, c, 0))
    small = pl.BlockSpec((1, width), lambda b, c: (0, 0))
    return pl.pallas_call(
        functools.partial(_rwscan_kernel, chunk=chunk),
        grid=(bsz, seq // chunk),
        in_specs=[cspec] * 7 + [small] * 3 + [pl.BlockSpec((chunk, chunk), lambda b, c: (0, 0)),
                                              pl.BlockSpec((LANES, LANES), lambda b, c: (0, 0))],
        out_specs=cspec,
        out_shape=jax.ShapeDtypeStruct((bsz, seq, width), F32),
        scratch_shapes=[pltpu.VMEM((width // LANES, LANES, LANES), F32)],
        compiler_params=_params(("parallel", "arbitrary")),
        name="rwscan",
    )(seq3(r), seq3(k), seq3(v), seq3(kk), seq3(lr), seq3(ld), seq3(g), rw_r_k.reshape(1, width),
      rw_lnx_g[None, :], rw_lnx_b[None, :], tri, bd)


def _outproj_kernel(x_ref, mla_ref, rw_ref, gmo_ref, wo_ref, gffn_ref, wr_ref, br_ref,
                    x1_ref, xn_ref, comb_ref):
    tm = x_ref.shape[0]
    m = mla_ref[...]
    mn = m * lax.rsqrt(jnp.mean(m * m, axis=-1, keepdims=True) + NORM_EPS) * gmo_ref[...]
    mix = jnp.concatenate([mn.astype(BF16), rw_ref[...].astype(BF16)], axis=-1)
    acc = x_ref[...] + jnp.dot(mix, wo_ref[...], preferred_element_type=F32)
    x1_ref[...] = acc
    xn = acc * lax.rsqrt(jnp.mean(acc * acc, axis=-1, keepdims=True) + NORM_EPS) * gffn_ref[...]
    xn_ref[...] = xn.astype(BF16)

    logits = _dot_bf16x3(xn, wr_ref[...]) + br_ref[...]
    lane = lax.broadcasted_iota(jnp.int32, (tm, LANES), 1)
    neg = -jnp.inf
    big = jnp.int32(1 << 20)
    is_g = (lane >= N_EXPERTS) & (lane < N_EXPERTS + N_GROUPS)
    glog = jnp.where(is_g, logits, neg)
    gmax = jnp.max(glog, axis=-1, keepdims=True)
    gsel = jnp.min(jnp.where(glog == gmax, lane, big), axis=-1, keepdims=True) - N_EXPERTS
    gprob = 1.0 / jnp.sum(jnp.exp(glog - gmax), axis=-1, keepdims=True)
    lo = gsel * EXPERTS_PER_GROUP
    in_group = (lane >= lo) & (lane < lo + EXPERTS_PER_GROUP)
    elog = jnp.where(in_group, logits, neg)
    v1 = jnp.max(elog, axis=-1, keepdims=True)
    i1 = jnp.min(jnp.where(elog == v1, lane, big), axis=-1, keepdims=True)
    elog2 = jnp.where(lane == i1, neg, elog)
    v2 = jnp.max(elog2, axis=-1, keepdims=True)
    i2 = jnp.min(jnp.where(elog2 == v2, lane, big), axis=-1, keepdims=True)
    e2 = jnp.exp(v2 - v1)
    w1 = gprob / (1.0 + e2)
    w2 = gprob * e2 / (1.0 + e2)
    comb_ref[...] = jnp.where(lane == i1, w1, 0.0) + jnp.where(lane == i2, w2, 0.0)


def _outproj(x2, mla, rw, g_mla_out, w_o, g_ffn, w_group, b_group, w_expert, b_expert, seq, tm):
    t = x2.shape[0]
    wo = w_o.astype(BF16)
    pad = LANES - N_EXPERTS - N_GROUPS
    wr = jnp.concatenate([w_expert, w_group, jnp.zeros((D_MODEL, pad), F32)], axis=-1)
    br = jnp.concatenate([b_expert, b_group, jnp.zeros((pad,), F32)])[None, :]
    gmo = g_mla_out[None, :]
    full = lambda a: pl.BlockSpec(a.shape, lambda i: (0,) * a.ndim)
    row = pl.BlockSpec((tm, D_MODEL), lambda i: (i, 0))
    return pl.pallas_call(
        _outproj_kernel,
        grid=(t // tm,),
        in_specs=[row,
                  pl.BlockSpec((tm, MLA_W), lambda i: (i, 0)),
                  pl.BlockSpec((tm, RWKV_W), lambda i: (i, 0)),
                  full(gmo), full(wo), pl.BlockSpec((1, D_MODEL), lambda i: (0, 0)), full(wr), full(br)],
        out_specs=[row, row, pl.BlockSpec((tm, LANES), lambda i: (i, 0))],
        out_shape=[jax.ShapeDtypeStruct((t, D_MODEL), F32),
                   jax.ShapeDtypeStruct((t, D_MODEL), BF16),
                   jax.ShapeDtypeStruct((t, LANES), F32)],
        compiler_params=_params(("parallel",)),
        name="outproj",
    )(x2, mla, rw, gmo, wo, g_ffn[None, :], wr, br)


def _moe_kernel(x1_ref, xn_ref, comb_ref, wg_ref, wu_ref, wd_ref, o_ref):
    e = pl.program_id(1)

    @pl.when(e == 0)
    def _():
        o_ref[...] = x1_ref[...]

    xn = xn_ref[...]
    hg = jnp.dot(xn, wg_ref[0], preferred_element_type=F32)
    hu = jnp.dot(xn, wu_ref[0], preferred_element_type=F32)
    hid = (hg * jax.nn.sigmoid(hg) * hu).astype(BF16)
    y = jnp.dot(hid, wd_ref[0], preferred_element_type=F32)
    lane = lax.broadcasted_iota(jnp.int32, comb_ref.shape, 1)
    ce = jnp.sum(jnp.where(lane == e, comb_ref[...], 0.0), axis=-1, keepdims=True)
    o_ref[...] += ce * y


def _moe(x1, xn, comb, w_gate, w_up, w_down, tm):
    t = x1.shape[0]
    row = lambda w: pl.BlockSpec((tm, w), lambda i, e: (i, 0))
    return pl.pallas_call(
        _moe_kernel,
        grid=(t // tm, N_EXPERTS),
        in_specs=[row(D_MODEL), row(D_MODEL), row(LANES),
                  pl.BlockSpec((1, D_MODEL, D_EXPERT), lambda i, e: (e, 0, 0)),
                  pl.BlockSpec((1, D_MODEL, D_EXPERT), lambda i, e: (e, 0, 0)),
                  pl.BlockSpec((1, D_EXPERT, D_MODEL), lambda i, e: (e, 0, 0))],
        out_specs=row(D_MODEL),
        out_shape=jax.ShapeDtypeStruct((t, D_MODEL), F32),
        compiler_params=_params(("parallel", "arbitrary")),
        name="moe",
    )(x1, xn, comb, w_gate.astype(BF16), w_up.astype(BF16), w_down.astype(BF16))


def _tile(seq, want):
    tm = min(seq, want)
    assert seq % tm == 0 and tm % 8 == 0
    return tm


def _layer(x, positions, g_mix, w_in, g_cq, w_uq, g_ckv, w_uk, w_uv, g_qn, g_kn, g_mla_out,
           rw_mu, rw_w0, rw_w2, rw_a0, rw_a2, rw_g2, rw_k_k, rw_k_a, rw_r_k, rw_lnx_g, rw_lnx_b,
           w_o, g_ffn, w_group, b_group, w_expert, b_expert, w_gate, w_up, w_down):
    bsz, seq, d = x.shape
    t = bsz * seq
    x2 = x.reshape(t, d)
    pos2 = positions.reshape(t, 1)
    kv_blk = _tile(seq, KV_BLK)
    feat, q, k, v = _inproj(x2, pos2, g_mix, w_in, g_cq, w_uq, g_ckv, w_uk, w_uv, g_qn, g_kn,
                            bsz, seq, _tile(kv_blk, 512), kv_blk)
    mla = _attn(q, k, v).reshape(t, MLA_W)
    r, k2, vv, kk, lr, ld, g = _rwprep(feat, rw_mu, rw_w0, rw_w2, rw_a0, rw_a2, rw_g2, rw_k_k, rw_k_a,
                                       bsz, seq, _tile(seq, 512))
    rw = _rwscan(r, k2, vv, kk, lr, ld, g, rw_r_k, rw_lnx_g, rw_lnx_b, bsz, seq, _tile(seq, 128))
    rw = rw.reshape(t, RWKV_W)
    x1, xn, comb = _outproj(x2, mla, rw, g_mla_out, w_o, g_ffn, w_group, b_group, w_expert, b_expert,
                            seq, _tile(seq, 512))
    out = _moe(x1, xn, comb, w_gate, w_up, w_down, _tile(t, 1024))
    return out.reshape(bsz, seq, d)


def kernel(x, positions, g_mix, w_in, g_cq, w_uq, g_ckv, w_uk, w_uv, g_qn, g_kn, g_mla_out, rw_mu, rw_w0, rw_w2, rw_a0, rw_a2, rw_g2, rw_k_k, rw_k_a, rw_r_k, rw_lnx_g, rw_lnx_b, w_o, g_ffn, w_group, b_group, w_expert, b_expert, w_gate, w_up, w_down):
    for l in range(g_mix.shape[0]):
        x = _layer(x, positions, g_mix[l], w_in[l], g_cq[l], w_uq[l], g_ckv[l], w_uk[l], w_uv[l],
                   g_qn[l], g_kn[l], g_mla_out[l], rw_mu[l], rw_w0[l], rw_w2[l], rw_a0[l], rw_a2[l],
                   rw_g2[l], rw_k_k[l], rw_k_a[l], rw_r_k[l], rw_lnx_g[l], rw_lnx_b[l], w_o[l],
                   g_ffn[l], w_group[l], b_group[l], w_expert[l], b_expert[l], w_gate[l], w_up[l],
                   w_down[l])
    return x
```

```python
import functools

import numpy as np
import jax
import jax.numpy as jnp
from jax import lax
from jax.experimental import pallas as pl
from jax.experimental.pallas import tpu as pltpu

F32 = jnp.float32
BF16 = jnp.bfloat16

D_MODEL = 1024
MLA_HEADS = 8
MLA_NOPE = 64
MLA_ROPE = 32
MLA_QK = MLA_NOPE + MLA_ROPE
MLA_V = 64
MLA_W = MLA_HEADS * MLA_V
MLA_Q_RANK = 256
MLA_KV_RANK = 128
ROPE_THETA = 10000.0
RWKV_HEADS = 8
RWKV_N = 64
RWKV_W = RWKV_HEADS * RWKV_N
DECAY_LORA = 64
AAA_LORA = 64
GATE_LORA = 128
MLA_IN = MLA_Q_RANK + MLA_KV_RANK + MLA_ROPE
RWKV_IN = 3 * RWKV_W + DECAY_LORA + AAA_LORA + GATE_LORA
N_GROUPS = 4
EXPERTS_PER_GROUP = 8
N_EXPERTS = N_GROUPS * EXPERTS_PER_GROUP
D_EXPERT = 256
NORM_EPS = 1e-6
LNX_EPS = 64e-5

LANES = 128
QK_PAD = LANES
Q_BLK = 512
KV_BLK = 512
ATT_HEADS = 4
MOE_TB = 1024
MOE_RB = 96
MOE_EXPERTS_PER_STEP = 2
VT_ROWS = MLA_V + 16
Q_SCALE = MLA_QK ** -0.5 * float(np.log2(np.e))
PM_W = MLA_Q_RANK + MLA_KV_RANK + 2 * LANES
VMEM_LIMIT = 56 * 1024 * 1024


def _dot_bf16x3(a, b):
    a_hi = a.astype(BF16)
    a_lo = (a - a_hi.astype(F32)).astype(BF16)
    b_hi = b.astype(BF16)
    b_lo = (b - b_hi.astype(F32)).astype(BF16)
    d = lambda p, q: jnp.dot(p, q, preferred_element_type=F32)
    return d(a_hi, b_hi) + (d(a_hi, b_lo) + d(a_lo, b_hi))


def _dot(a, b):
    return jnp.dot(a.astype(BF16), b.astype(BF16), preferred_element_type=F32)


def _dot_nt(a, b):
    return lax.dot_general(a.astype(BF16), b.astype(BF16), (((1,), (1,)), ((), ())),
                           preferred_element_type=F32)


def _dot_tn(a, b):
    return lax.dot_general(a.astype(BF16), b.astype(BF16), (((0,), (0,)), ((), ())),
                           preferred_element_type=F32)


def _dot_split(a, b, split_lhs, passes=2):
    s = a if split_lhs else b
    out = None
    for _ in range(passes):
        piece = s.astype(BF16)
        term = (jnp.dot(piece, b, preferred_element_type=F32) if split_lhs
                else jnp.dot(a, piece, preferred_element_type=F32))
        out = term if out is None else out + term
        s = s - piece.astype(F32)
    return out


def _head_indicator(width):
    hid = np.arange(width) // RWKV_N
    return jnp.asarray((hid[:, None] == hid[None, :]).astype(np.float32)).astype(BF16)


def _params(sem):
    return pltpu.CompilerParams(dimension_semantics=sem, vmem_limit_bytes=VMEM_LIMIT)


def _inproj_kernel(x_ref, pos_ref, gmix_ref, wmla_ref, wrw_ref, freq_ref, gcq_ref, gckv_ref,
                   wq_ref, wqr_ref, wk_ref, wv_ref, gq_ref, gqr_ref, gk_ref, gkr_ref,
                   feat_ref, q_ref, k_ref, v_ref):
    x = x_ref[...]
    h = x * lax.rsqrt(jnp.mean(x * x, axis=-1, keepdims=True) + NORM_EPS) * gmix_ref[...]
    hb = h.astype(BF16)
    feat_ref[...] = jnp.dot(hb, wrw_ref[...], preferred_element_type=F32)
    pm = jnp.dot(hb, wmla_ref[...], preferred_element_type=F32)

    c_q = pm[:, :MLA_Q_RANK]
    c_kv = pm[:, MLA_Q_RANK:MLA_Q_RANK + MLA_KV_RANK]
    kr = pm[:, MLA_Q_RANK + MLA_KV_RANK:MLA_Q_RANK + MLA_KV_RANK + LANES]
    kr_rot = pm[:, MLA_Q_RANK + MLA_KV_RANK + LANES:PM_W]
    cqn = (c_q * lax.rsqrt(jnp.mean(c_q * c_q, axis=-1, keepdims=True) + NORM_EPS)
           * gcq_ref[...]).astype(BF16)
    ckvn = (c_kv * lax.rsqrt(jnp.mean(c_kv * c_kv, axis=-1, keepdims=True) + NORM_EPS)
            * gckv_ref[...]).astype(BF16)

    ang = pos_ref[...].astype(F32) * freq_ref[...]
    cosf = jnp.cos(ang)
    sinf = jnp.sin(ang)
    q_cos = gq_ref[...] * cosf * Q_SCALE
    q_sin = gqr_ref[...] * sinf * Q_SCALE
    k_cos = gk_ref[...] * cosf
    k_sin = kr_rot * (gkr_ref[...] * sinf)
    inv_qk = 1.0 / MLA_QK
    ones_rows = jnp.where(lax.broadcasted_iota(jnp.int32, (VT_ROWS - MLA_V, x.shape[0]), 0) == 0, 1.0, 0.0)

    for hh in range(MLA_HEADS):
        q_raw = jnp.dot(cqn, wq_ref[hh], preferred_element_type=F32)
        q_rot = jnp.dot(cqn, wqr_ref[hh], preferred_element_type=F32)
        sq = lax.rsqrt(jnp.sum(q_raw * q_raw, axis=-1, keepdims=True) * inv_qk + NORM_EPS)
        q_ref[0, hh] = (sq * (q_raw * q_cos + q_rot * q_sin)).astype(BF16)
        k_raw = jnp.dot(ckvn, wk_ref[hh], preferred_element_type=F32) + kr
        sk = lax.rsqrt(jnp.sum(k_raw * k_raw, axis=-1, keepdims=True) * inv_qk + NORM_EPS)
        k_ref[0, hh] = (sk * (k_raw * k_cos + k_sin)).astype(BF16)
        vt = lax.dot_general(wv_ref[hh], ckvn, (((1,), (1,)), ((), ())), preferred_element_type=F32)
        v_ref[0, hh, 0] = jnp.concatenate([vt, ones_rows], axis=0).astype(BF16)


def _rot_cols(w):
    half = MLA_ROPE // 2
    z = jnp.zeros_like(w[..., :MLA_NOPE])
    return jnp.concatenate([z, -w[..., MLA_NOPE + half:MLA_QK], w[..., MLA_NOPE:MLA_NOPE + half]], axis=-1)


def _pad_last(w, n):
    return jnp.pad(w, [(0, 0)] * (w.ndim - 1) + [(0, n - w.shape[-1])])


def _inproj(x2, pos2, g_mix, w_in, g_cq, w_uq, g_ckv, w_uk, w_uv, g_qn, g_kn, bsz, seq, tm, kv_blk):
    t = x2.shape[0]
    nt = seq // tm
    per_kv = kv_blk // tm
    half = MLA_ROPE // 2
    w_kr = w_in[:, MLA_Q_RANK + MLA_KV_RANK:MLA_IN]
    zeros64 = jnp.zeros((D_MODEL, MLA_NOPE), F32)
    zeros32 = jnp.zeros((D_MODEL, LANES - MLA_QK), F32)
    w_kr_rot = jnp.concatenate([-w_kr[:, half:], w_kr[:, :half]], axis=-1)
    w_mla = jnp.concatenate([w_in[:, :MLA_Q_RANK + MLA_KV_RANK], zeros64, w_kr, zeros32,
                             zeros64, w_kr_rot, zeros32], axis=-1).astype(BF16)
    w_rw = w_in[:, MLA_IN:].astype(BF16)
    wq = jnp.transpose(w_uq, (1, 0, 2))
    wq_rot = _pad_last(_rot_cols(wq), QK_PAD).astype(BF16)
    wq = _pad_last(wq, QK_PAD).astype(BF16)
    wk = _pad_last(jnp.transpose(w_uk, (1, 0, 2)), QK_PAD).astype(BF16)
    wv = jnp.transpose(w_uv, (1, 2, 0)).astype(BF16)
    inv_freq = 1.0 / (ROPE_THETA ** (jnp.arange(half, dtype=F32) / half))
    freq = jnp.concatenate([jnp.zeros((MLA_NOPE,), F32), inv_freq, inv_freq,
                            jnp.zeros((LANES - MLA_QK,), F32)])[None, :]
    swap = lambda g: jnp.concatenate([jnp.zeros((MLA_NOPE,), F32), g[MLA_NOPE + half:MLA_QK],
                                      g[MLA_NOPE:MLA_NOPE + half], jnp.zeros((LANES - MLA_QK,), F32)])[None, :]
    gq = _pad_last(g_qn[None, :], QK_PAD)
    gk = _pad_last(g_kn[None, :], QK_PAD)
    gq_rot = swap(g_qn)
    gk_rot = swap(g_kn)

    full = lambda a: pl.BlockSpec(a.shape, lambda i: (0,) * a.ndim)
    head_map = lambda i: (i // nt, 0, i % nt, 0)
    args = (x2, pos2, g_mix[None, :], w_mla, w_rw, freq, g_cq[None, :], g_ckv[None, :],
            wq, wq_rot, wk, wv, gq, gq_rot, gk, gk_rot)
    in_specs = [pl.BlockSpec((tm, D_MODEL), lambda i: (i, 0)),
                pl.BlockSpec((tm, 1), lambda i: (i, 0))] + [full(a) for a in args[2:]]
    return pl.pallas_call(
        _inproj_kernel,
        grid=(t // tm,),
        in_specs=in_specs,
        out_specs=[pl.BlockSpec((tm, RWKV_IN), lambda i: (i, 0)),
                   pl.BlockSpec((1, MLA_HEADS, tm, QK_PAD), head_map),
                   pl.BlockSpec((1, MLA_HEADS, tm, QK_PAD), head_map),
                   pl.BlockSpec((1, MLA_HEADS, 1, VT_ROWS, tm),
                                lambda i: (i // nt, 0, (i % nt) // per_kv, 0, (i % nt) % per_kv))],
        out_shape=[jax.ShapeDtypeStruct((t, RWKV_IN), F32),
                   jax.ShapeDtypeStruct((bsz, MLA_HEADS, seq, QK_PAD), BF16),
                   jax.ShapeDtypeStruct((bsz, MLA_HEADS, seq, QK_PAD), BF16),
                   jax.ShapeDtypeStruct((bsz, MLA_HEADS, seq // kv_blk, VT_ROWS, kv_blk), BF16)],
        compiler_params=_params(("parallel",)),
        name="inproj",
    )(*args)


def _attn_kernel(q_ref, k_ref, vt_ref, o_ref):
    qi = pl.program_id(2)
    nh = q_ref.shape[1]
    qb = q_ref.shape[2]
    kvb = vt_ref.shape[4]
    qs = [q_ref[0, h] for h in range(nh)]

    def group(g, carry, masked):
        start = pl.multiple_of(g * kvb, kvb)
        ss = [lax.dot_general(k_ref[0, h, pl.ds(start, kvb), :], qs[h], (((1,), (1,)), ((), ())),
                              preferred_element_type=F32) for h in range(nh)]
        if masked:
            key = g * kvb + lax.broadcasted_iota(jnp.int32, (kvb, qb), 0)
            qry = qi * qb + lax.broadcasted_iota(jnp.int32, (kvb, qb), 1)
            ss = [jnp.where(key <= qry, s, -jnp.inf) for s in ss]
        ms = [jnp.maximum(carry[h][0], jnp.max(ss[h], axis=0, keepdims=True)) for h in range(nh)]
        ps = [jnp.exp2((ss[h] - ms[h]).astype(BF16)) for h in range(nh)]
        out = []
        for h in range(nh):
            m, acc = carry[h]
            alpha = jnp.exp2(m - ms[h])
            out.append((ms[h], alpha * acc + jnp.dot(vt_ref[0, h, g], ps[h], preferred_element_type=F32)))
        return tuple(out)

    init = tuple((jnp.full((1, qb), -1e30, F32), jnp.zeros((VT_ROWS, qb), F32)) for _ in range(nh))
    full_groups = (qi * qb) // kvb
    carry = lax.fori_loop(0, full_groups, lambda g, c: group(g, c, False), init)
    carry = group(full_groups, carry, True)
    ot = jnp.concatenate([acc[:MLA_V] * (1.0 / acc[MLA_V:MLA_V + 1]) for _, acc in carry], axis=0)
    o_ref[0] = ot.T


def _attn(q, k, vt):
    bsz, nh, seq, _ = q.shape
    per_step = ATT_HEADS
    ngrp, kvb = vt.shape[2], vt.shape[4]
    qb = min(Q_BLK, kvb)
    return pl.pallas_call(
        _attn_kernel,
        grid=(bsz, nh // per_step, seq // qb),
        in_specs=[pl.BlockSpec((1, per_step, qb, QK_PAD), lambda b, h, i: (b, h, i, 0)),
                  pl.BlockSpec((1, per_step, seq, QK_PAD), lambda b, h, i: (b, h, 0, 0)),
                  pl.BlockSpec((1, per_step, ngrp, VT_ROWS, kvb), lambda b, h, i: (b, h, 0, 0, 0))],
        out_specs=pl.BlockSpec((1, qb, per_step * MLA_V), lambda b, h, i: (b, i, h)),
        out_shape=jax.ShapeDtypeStruct((bsz, seq, nh * MLA_V), F32),
        compiler_params=_params(("parallel", "parallel", "arbitrary")),
        name="attn",
    )(q, k, vt)


def _rwprep_kernel(feat_ref, prev_ref, mu_ref, w0_ref, w2_ref, a0_ref, a2_ref, g2_ref, kk_ref, ka_ref,
                   bd_ref, r_o, k_o, v_o, kk_o, lr_o, ld_o, g_o, *, nt):
    i = pl.program_id(0)
    feat = feat_ref[...]
    tm = feat.shape[0]
    first = jnp.where(i % nt == 0, 0.0, 1.0)
    prev_row = prev_ref[7:8, :] * first
    rolled = pltpu.roll(feat, 1, axis=0)
    rid = lax.broadcasted_iota(jnp.int32, (tm, 1), 0)
    prev = jnp.where(rid == 0, prev_row, rolled)
    feat = feat + (prev - feat) * mu_ref[...]

    r = feat[:, :RWKV_W]
    k = feat[:, RWKV_W:2 * RWKV_W]
    v = feat[:, 2 * RWKV_W:3 * RWKV_W]
    wa = feat[:, 3 * RWKV_W:3 * RWKV_W + LANES]
    gl = feat[:, 3 * RWKV_W + LANES:]
    lane = lax.broadcasted_iota(jnp.int32, (tm, LANES), 1)
    wl_in = jnp.where(lane < DECAY_LORA, jnp.tanh(wa), 0.0)
    al_in = jnp.where(lane < DECAY_LORA, 0.0, wa)
    wpre = w0_ref[...] + _dot(wl_in, w2_ref[...])
    w = -(jnp.maximum(-wpre, 0.0) + jnp.log(1.0 + jnp.exp(-jnp.abs(wpre)))) - 0.5
    logd = -jnp.exp(w)
    a = jax.nn.sigmoid(a0_ref[...] + _dot(al_in, a2_ref[...]))
    g = _dot(jax.nn.sigmoid(gl), g2_ref[...])
    kk = k * kk_ref[...]
    ss = _dot_split(kk * kk, bd_ref[...], split_lhs=True)
    r_o[...] = r
    k_o[...] = k * (1.0 + (a - 1.0) * ka_ref[...])
    v_o[...] = v
    kk_o[...] = kk * lax.rsqrt(ss + 1e-12)
    lr_o[...] = a
    ld_o[...] = logd
    g_o[...] = g


def _rwprep(feat, rw_mu, rw_w0, rw_w2, rw_a0, rw_a2, rw_g2, rw_k_k, rw_k_a, bsz, seq, tm):
    t = feat.shape[0]
    nt = seq // tm
    w2p = jnp.concatenate([rw_w2, jnp.zeros((LANES - DECAY_LORA, RWKV_W), F32)], axis=0).astype(BF16)
    a2p = jnp.concatenate([jnp.zeros((DECAY_LORA, RWKV_W), F32), rw_a2], axis=0).astype(BF16)
    args = (feat, feat, rw_mu[None, :], rw_w0[None, :], w2p, rw_a0[None, :], a2p, rw_g2.astype(BF16),
            rw_k_k[None, :], rw_k_a[None, :], _head_indicator(RWKV_W))
    full = lambda a: pl.BlockSpec(a.shape, lambda i: (0,) * a.ndim)
    rows8 = tm // 8
    in_specs = [pl.BlockSpec((tm, RWKV_IN), lambda i: (i, 0)),
                pl.BlockSpec((8, RWKV_IN), lambda i: (jnp.maximum(i * rows8 - 1, 0), 0))]
    in_specs += [full(a) for a in args[2:]]
    hshape = jax.ShapeDtypeStruct((t, RWKV_W), F32)
    hspec = pl.BlockSpec((tm, RWKV_W), lambda i: (i, 0))
    return pl.pallas_call(
        functools.partial(_rwprep_kernel, nt=nt),
        grid=(t // tm,),
        in_specs=in_specs,
        out_specs=[hspec] * 7,
        out_shape=[hshape] * 7,
        compiler_params=_params(("parallel",)),
        name="rwprep",
    )(*args)


def _rwscan_kernel(r_ref, k_ref, v_ref, kk_ref, lr_ref, ld_ref, g_ref, rk_ref, lng_ref, lnb_ref,
                   tri_ref, bd_ref, o_ref, state, *, chunk):
    c = pl.program_id(1)

    @pl.when(c == 0)
    def _():
        state[...] = jnp.zeros_like(state)

    C = chunk
    npairs = RWKV_W // LANES
    r = r_ref[0]
    k = k_ref[0]
    v = v_ref[0]
    kk = kk_ref[0]
    ld = ld_ref[0]
    bd = bd_ref[...]

    cum = _dot_split(tri_ref[...], ld, split_lhs=False)
    mid = C // 2 - 1
    cmid = cum[mid:mid + 1, :]
    cend = cum[C - 1:C, :]
    e_pos = jnp.exp(cum - cmid)
    e_prev = jnp.exp(cum - ld - cmid)
    e_neg = jnp.exp(cmid - cum)
    g_mid = jnp.exp(cmid)
    g_end_rel = jnp.exp(cend - cmid)
    g_end = jnp.exp(cend)
    rt = r * e_pos
    at = -kk * e_prev
    bt = kk * lr_ref[0] * e_neg
    kt = k * e_neg
    at_g = (at * g_mid).astype(BF16)
    rt_g = (rt * g_mid).astype(BF16)
    art = jnp.concatenate([at, rt], axis=0)
    bk = jnp.concatenate([bt, kt], axis=0).astype(BF16)
    bkg = jnp.concatenate([bt * g_end_rel, kt * g_end_rel], axis=0).astype(BF16)
    vb = v.astype(BF16)

    row = lax.broadcasted_iota(jnp.int32, (C, C), 0)
    col = lax.broadcasted_iota(jnp.int32, (C, C), 1)
    strict = row > col
    eye = (row == col).astype(F32)
    row2 = lax.broadcasted_iota(jnp.int32, (C, 2 * C), 0)
    col2 = lax.broadcasted_iota(jnp.int32, (C, 2 * C), 1)
    incl2 = row2 >= jnp.where(col2 >= C, col2 - C, col2)
    low2 = lax.broadcasted_iota(jnp.int32, (2 * C, LANES), 1) < RWKV_N
    low = lax.broadcasted_iota(jnp.int32, (C, LANES), 1) < RWKV_N
    srow = lax.broadcasted_iota(jnp.int32, (LANES, LANES), 0)
    scol = lax.broadcasted_iota(jnp.int32, (LANES, LANES), 1)
    same_head = (srow < RWKV_N) == (scol < RWKV_N)
    diag = srow == scol
    pair = lambda a, p: a[:, p * LANES:(p + 1) * LANES]

    a_ab, a_ak, a_rbk = [], [], []
    for p in range(npairs):
        art_p = pair(art, p)
        bk_p = pair(bk, p)
        for half in range(2):
            lhs = jnp.where(low2, art_p, 0.0) if half == 0 else jnp.where(low2, 0.0, art_p)
            big = _dot_nt(lhs, bk_p)
            a_ab.append(jnp.where(strict, big[:C, :C], 0.0))
            a_ak.append(jnp.where(strict, big[:C, C:], 0.0).astype(BF16))
            a_rbk.append(jnp.where(incl2, big[C:, :], 0.0).astype(BF16))

    ps = [a.astype(BF16) for a in a_ab]
    tinv = [eye + a for a in a_ab]
    n = 2
    while n < C:
        sq = [jnp.dot(pb, pb, preferred_element_type=F32) for pb in ps]
        ps = [s.astype(BF16) for s in sq]
        tinv = [t + jnp.dot(t.astype(BF16), pb, preferred_element_type=F32) for t, pb in zip(tinv, ps)]
        n *= 2
    tinv = [t.astype(BF16) for t in tinv]

    s0 = [state[p] for p in range(npairs)]
    s0b = [s.astype(BF16) for s in s0]
    sel = lambda h0, h1: jnp.where(low, h0, h1)
    mm = lambda a, b: jnp.dot(a, b, preferred_element_type=F32)
    xs = [mm(pair(at_g, p), s0b[p]) + sel(mm(a_ak[2 * p], pair(vb, p)), mm(a_ak[2 * p + 1], pair(vb, p)))
          for p in range(npairs)]
    xb = [x.astype(BF16) for x in xs]
    us = [sel(mm(tinv[2 * p], xb[p]), mm(tinv[2 * p + 1], xb[p])) for p in range(npairs)]
    uv = [jnp.concatenate([us[p].astype(BF16), pair(vb, p)], axis=0) for p in range(npairs)]
    ys = [mm(pair(rt_g, p), s0b[p]) + sel(mm(a_rbk[2 * p], uv[p]), mm(a_rbk[2 * p + 1], uv[p]))
          for p in range(npairs)]
    for p in range(npairs):
        upd = lax.dot_general(pair(bkg, p), uv[p], (((0,), (0,)), ((), ())), preferred_element_type=F32)
        g_col = jnp.sum(jnp.where(diag, pair(g_end, p), 0.0), axis=-1, keepdims=True)
        state[p] = s0[p] * g_col + jnp.where(same_head, upd, 0.0)

    inv_n = 1.0 / RWKV_N
    for p in range(npairs):
        sl = slice(p * LANES, (p + 1) * LANES)
        y = ys[p]
        mean = _dot_split(y, bd, split_lhs=True) * inv_n
        yc = y - mean
        var = _dot_split(yc * yc, bd, split_lhs=True) * inv_n
        yn = yc * lax.rsqrt(var + LNX_EPS) * lng_ref[:, sl] + lnb_ref[:, sl]
        bonus = _dot_split(pair(r, p) * pair(k, p) * rk_ref[:, sl], bd, split_lhs=True) * pair(v, p)
        o_ref[0, :, sl] = (yn + bonus) * g_ref[0, :, sl]


def _rwscan(r, k, v, kk, lr, ld, g, rw_r_k, rw_lnx_g, rw_lnx_b, bsz, seq, chunk):
    width = r.shape[-1]
    tri = jnp.asarray(np.tril(np.ones((chunk, chunk), np.float32))).astype(BF16)
    bd = _head_indicator(LANES)
    seq3 = lambda a: a.reshape(bsz, seq, width)
    cspec = pl.BlockSpec((1, chunk, width), lambda b, c: (b, c, 0))
    small = pl.BlockSpec((1, width), lambda b, c: (0, 0))
    return pl.pallas_call(
        functools.partial(_rwscan_kernel, chunk=chunk),
        grid=(bsz, seq // chunk),
        in_specs=[cspec] * 7 + [small] * 3 + [pl.BlockSpec((chunk, chunk), lambda b, c: (0, 0)),
                                              pl.BlockSpec((LANES, LANES), lambda b, c: (0, 0))],
        out_specs=cspec,
        out_shape=jax.ShapeDtypeStruct((bsz, seq, width), F32),
        scratch_shapes=[pltpu.VMEM((width // LANES, LANES, LANES), F32)],
        compiler_params=_params(("parallel", "arbitrary")),
        name="rwscan",
    )(seq3(r), seq3(k), seq3(v), seq3(kk), seq3(lr), seq3(ld), seq3(g), rw_r_k.reshape(1, width),
      rw_lnx_g[None, :], rw_lnx_b[None, :], tri, bd)


def _outproj_kernel(x_ref, mla_ref, rw_ref, gmo_ref, wo_ref, gffn_ref, wr_ref, br_ref,
                    x1_ref, comb_ref):
    tm = x_ref.shape[0]
    m = mla_ref[...]
    mn = m * lax.rsqrt(jnp.mean(m * m, axis=-1, keepdims=True) + NORM_EPS) * gmo_ref[...]
    mix = jnp.concatenate([mn.astype(BF16), rw_ref[...].astype(BF16)], axis=-1)
    acc = x_ref[...] + jnp.dot(mix, wo_ref[...], preferred_element_type=F32)
    x1_ref[...] = acc
    xn = acc * lax.rsqrt(jnp.mean(acc * acc, axis=-1, keepdims=True) + NORM_EPS) * gffn_ref[...]

    logits = _dot_bf16x3(xn, wr_ref[...]) + br_ref[...]
    lane = lax.broadcasted_iota(jnp.int32, (tm, LANES), 1)
    neg = -jnp.inf
    big = jnp.int32(1 << 20)
    is_g = (lane >= N_EXPERTS) & (lane < N_EXPERTS + N_GROUPS)
    glog = jnp.where(is_g, logits, neg)
    gmax = jnp.max(glog, axis=-1, keepdims=True)
    gsel = jnp.min(jnp.where(glog == gmax, lane, big), axis=-1, keepdims=True) - N_EXPERTS
    gprob = 1.0 / jnp.sum(jnp.exp(glog - gmax), axis=-1, keepdims=True)
    lo = gsel * EXPERTS_PER_GROUP
    in_group = (lane >= lo) & (lane < lo + EXPERTS_PER_GROUP)
    elog = jnp.where(in_group, logits, neg)
    v1 = jnp.max(elog, axis=-1, keepdims=True)
    i1 = jnp.min(jnp.where(elog == v1, lane, big), axis=-1, keepdims=True)
    elog2 = jnp.where(lane == i1, neg, elog)
    v2 = jnp.max(elog2, axis=-1, keepdims=True)
    i2 = jnp.min(jnp.where(elog2 == v2, lane, big), axis=-1, keepdims=True)
    e2 = jnp.exp(v2 - v1)
    w1 = gprob / (1.0 + e2)
    w2 = gprob * e2 / (1.0 + e2)
    comb_ref[...] = jnp.where(lane == 0, i1.astype(F32), jnp.where(lane == 1, i2.astype(F32),
                              jnp.where(lane == 2, w1, jnp.where(lane == 3, w2, 0.0))))


def _outproj(x2, mla, rw, g_mla_out, w_o, g_ffn, w_group, b_group, w_expert, b_expert, seq, tm):
    t = x2.shape[0]
    wo = w_o.astype(BF16)
    pad = LANES - N_EXPERTS - N_GROUPS
    wr = jnp.concatenate([w_expert, w_group, jnp.zeros((D_MODEL, pad), F32)], axis=-1)
    br = jnp.concatenate([b_expert, b_group, jnp.zeros((pad,), F32)])[None, :]
    gmo = g_mla_out[None, :]
    full = lambda a: pl.BlockSpec(a.shape, lambda i: (0,) * a.ndim)
    row = pl.BlockSpec((tm, D_MODEL), lambda i: (i, 0))
    return pl.pallas_call(
        _outproj_kernel,
        grid=(t // tm,),
        in_specs=[row,
                  pl.BlockSpec((tm, MLA_W), lambda i: (i, 0)),
                  pl.BlockSpec((tm, RWKV_W), lambda i: (i, 0)),
                  full(gmo), full(wo), pl.BlockSpec((1, D_MODEL), lambda i: (0, 0)), full(wr), full(br)],
        out_specs=[row, pl.BlockSpec((tm, LANES), lambda i: (i, 0))],
        out_shape=[jax.ShapeDtypeStruct((t, D_MODEL), F32),
                   jax.ShapeDtypeStruct((t, LANES), F32)],
        compiler_params=_params(("parallel",)),
        name="outproj",
    )(x2, mla, rw, gmo, wo, g_ffn[None, :], wr, br)


def _moeplan_kernel(route_ref, tri_ref, triu_ref, posw_ref, nb_ref, roff_ref):
    rt = route_ref[...]
    tb = rt.shape[0]
    lane = lax.broadcasted_iota(jnp.int32, (tb, LANES), 1)
    oh1 = lane == rt[:, 0:1].astype(jnp.int32)
    oh2 = lane == rt[:, 1:2].astype(jnp.int32)
    oh = jnp.where(oh1, 1.0, jnp.where(oh2, 1.0, 0.0))
    excl = jnp.dot(tri_ref[...], oh.astype(BF16), preferred_element_type=F32)
    cnt = jnp.sum(oh, axis=0, keepdims=True)
    nb = jnp.floor((cnt + (MOE_RB - 0.5)) * (1.0 / MOE_RB))
    boff = jnp.dot(jnp.broadcast_to(nb, (8, LANES)).astype(BF16), triu_ref[...],
                   preferred_element_type=F32)[0:1]
    roff = boff * MOE_RB
    base = roff + excl
    pos1 = jnp.sum(jnp.where(oh1, base, 0.0), axis=-1, keepdims=True)
    pos2 = jnp.sum(jnp.where(oh2, base, 0.0), axis=-1, keepdims=True)
    posw_ref[...] = jnp.where(lane == 0, pos1, jnp.where(lane == 1, pos2, rt))
    nb_ref[0] = nb.astype(jnp.int32)
    roff_ref[0] = roff.astype(jnp.int32)


def _moeplan(route, tb):
    t = route.shape[0]
    nt = t // tb
    tri = jnp.asarray(np.tril(np.ones((tb, tb), np.float32), -1)).astype(BF16)
    triu = jnp.asarray(np.triu(np.ones((LANES, LANES), np.float32), 1)).astype(BF16)
    small = pl.BlockSpec((1, 1, LANES), lambda i: (i, 0, 0))
    return pl.pallas_call(
        _moeplan_kernel,
        grid=(nt,),
        in_specs=[pl.BlockSpec((tb, LANES), lambda i: (i, 0)),
                  pl.BlockSpec((tb, tb), lambda i: (0, 0)),
                  pl.BlockSpec((LANES, LANES), lambda i: (0, 0))],
        out_specs=[pl.BlockSpec((tb, LANES), lambda i: (i, 0)), small, small],
        out_shape=[jax.ShapeDtypeStruct((t, LANES), F32),
                   jax.ShapeDtypeStruct((nt, 1, LANES), jnp.int32),
                   jax.ShapeDtypeStruct((nt, 1, LANES), jnp.int32)],
        compiler_params=_params(("parallel",)),
        name="moeplan",
    )(route, tri, triu)


def _moe_kernel(pos_ref, nb_ref, roff_ref, x1_ref, posw_ref, gffn_ref, wg_hbm, wu_hbm, wd_hbm, o_ref,
                xn, xs, g1, g2, wg_buf, wu_buf, wd_buf, sem):
    i = pl.program_id(0)
    nt = pl.num_programs(0)
    tb = x1_ref.shape[0]
    chunk = g1.shape[0]
    dummy_r0 = xs.shape[0] - MOE_RB
    group = MOE_EXPERTS_PER_STEP
    ngroups = N_EXPERTS // group

    def weight_copies(e, slot):
        return (pltpu.make_async_copy(wg_hbm.at[e], wg_buf.at[slot], sem.at[slot, 0]),
                pltpu.make_async_copy(wu_hbm.at[e], wu_buf.at[slot], sem.at[slot, 1]),
                pltpu.make_async_copy(wd_hbm.at[e], wd_buf.at[slot], sem.at[slot, 2]))

    @pl.when(i == 0)
    def _():
        for j in range(group):
            for c in weight_copies(j, j):
                c.start()

    xs[...] = jnp.zeros_like(xs)
    for c in range(tb // chunk):
        sl = slice(c * chunk, (c + 1) * chunk)
        x1 = x1_ref[sl, :]
        xn[sl, :] = x1 * lax.rsqrt(jnp.mean(x1 * x1, axis=-1, keepdims=True) + NORM_EPS) * gffn_ref[...]

    def put(tok, carry):
        row = xn[pl.ds(tok, 1), :]
        xs[pl.ds(pos_ref[2 * tok], 1), :] = row
        xs[pl.ds(pos_ref[2 * tok + 1], 1), :] = row
        return carry

    lax.fori_loop(0, tb, put, 0)

    def expert_group(p, carry):
        base = lax.rem(p, 2) * group
        nxt = p + 1

        @pl.when(jnp.logical_or(nxt < ngroups, i + 1 < nt))
        def _():
            first = jnp.where(nxt < ngroups, nxt, 0) * group
            for j in range(group):
                for c in weight_copies(first + j, group - base + j):
                    c.start()

        for j in range(group):
            for c in weight_copies(p * group + j, base + j):
                c.wait()
        nbs = [nb_ref[0, 0, p * group + j] for j in range(group)]
        roffs = [roff_ref[0, 0, p * group + j] for j in range(group)]
        trips = functools.reduce(jnp.maximum, nbs)

        def block(b, c2):
            r0s = [pl.multiple_of(jnp.where(b < nbs[j], roffs[j] + b * MOE_RB, dummy_r0), 8)
                   for j in range(group)]
            x = [xs[pl.ds(r0, MOE_RB), :].astype(BF16) for r0 in r0s]
            hg = [jnp.dot(x[j], wg_buf[base + j], preferred_element_type=F32) for j in range(group)]
            hu = [jnp.dot(x[j], wu_buf[base + j], preferred_element_type=F32) for j in range(group)]
            hid = [(hg[j] * jax.nn.sigmoid(hg[j]) * hu[j]).astype(BF16) for j in range(group)]
            y = [jnp.dot(hid[j], wd_buf[base + j], preferred_element_type=F32) for j in range(group)]
            for j in range(group):
                xs[pl.ds(r0s[j], MOE_RB), :] = y[j]
            return c2

        lax.fori_loop(0, trips, block, 0)
        return carry

    lax.fori_loop(0, ngroups, expert_group, 0)

    for c in range(tb // chunk):
        def take(j, carry, c=c):
            tok = c * chunk + j
            g1[pl.ds(j, 1), :] = xs[pl.ds(pos_ref[2 * tok], 1), :]
            g2[pl.ds(j, 1), :] = xs[pl.ds(pos_ref[2 * tok + 1], 1), :]
            return carry

        lax.fori_loop(0, chunk, take, 0)
        sl = slice(c * chunk, (c + 1) * chunk)
        o_ref[sl, :] = x1_ref[sl, :] + (posw_ref[sl, 2:3] * g1[...] + posw_ref[sl, 3:4] * g2[...])


def _moe(x1, route, g_ffn, w_gate, w_up, w_down, tb):
    t = x1.shape[0]
    nt = t // tb
    chunk = min(tb, 256)
    rows = 2 * tb + N_EXPERTS * MOE_RB + MOE_RB
    nslot = 2 * MOE_EXPERTS_PER_STEP
    posw, nb, roff = _moeplan(route, tb)
    pos = posw[:, 0:2].astype(jnp.int32).reshape(2 * t)
    smem3 = lambda: pl.BlockSpec((1, 1, LANES), lambda i: (i, 0, 0), memory_space=pltpu.SMEM)
    hbm = lambda: pl.BlockSpec(memory_space=pl.ANY)
    return pl.pallas_call(
        _moe_kernel,
        grid=(nt,),
        in_specs=[pl.BlockSpec((2 * tb,), lambda i: (i,), memory_space=pltpu.SMEM), smem3(), smem3(),
                  pl.BlockSpec((tb, D_MODEL), lambda i: (i, 0)),
                  pl.BlockSpec((tb, LANES), lambda i: (i, 0)),
                  pl.BlockSpec((1, D_MODEL), lambda i: (0, 0)),
                  hbm(), hbm(), hbm()],
        out_specs=pl.BlockSpec((tb, D_MODEL), lambda i: (i, 0)),
        out_shape=jax.ShapeDtypeStruct((t, D_MODEL), F32),
        scratch_shapes=[pltpu.VMEM((tb, D_MODEL), F32), pltpu.VMEM((rows, D_MODEL), F32),
                        pltpu.VMEM((chunk, D_MODEL), F32), pltpu.VMEM((chunk, D_MODEL), F32),
                        pltpu.VMEM((nslot, D_MODEL, D_EXPERT), BF16), pltpu.VMEM((nslot, D_MODEL, D_EXPERT), BF16),
                        pltpu.VMEM((nslot, D_EXPERT, D_MODEL), BF16), pltpu.SemaphoreType.DMA((nslot, 3))],
        compiler_params=_params(("arbitrary",)),
        name="moe",
    )(pos, nb, roff, x1, posw, g_ffn[None, :], w_gate.astype(BF16), w_up.astype(BF16), w_down.astype(BF16))


def _tile(seq, want):
    tm = min(seq, want)
    assert seq % tm == 0 and tm % 8 == 0
    return tm


def _layer(x, positions, g_mix, w_in, g_cq, w_uq, g_ckv, w_uk, w_uv, g_qn, g_kn, g_mla_out,
           rw_mu, rw_w0, rw_w2, rw_a0, rw_a2, rw_g2, rw_k_k, rw_k_a, rw_r_k, rw_lnx_g, rw_lnx_b,
           w_o, g_ffn, w_group, b_group, w_expert, b_expert, w_gate, w_up, w_down):
    bsz, seq, d = x.shape
    t = bsz * seq
    x2 = x.reshape(t, d)
    pos2 = positions.reshape(t, 1)
    kv_blk = _tile(seq, KV_BLK)
    feat, q, k, v = _inproj(x2, pos2, g_mix, w_in, g_cq, w_uq, g_ckv, w_uk, w_uv, g_qn, g_kn,
                            bsz, seq, _tile(kv_blk, 512), kv_blk)
    mla = _attn(q, k, v).reshape(t, MLA_W)
    r, k2, vv, kk, lr, ld, g = _rwprep(feat, rw_mu, rw_w0, rw_w2, rw_a0, rw_a2, rw_g2, rw_k_k, rw_k_a,
                                       bsz, seq, _tile(seq, 512))
    rw = _rwscan(r, k2, vv, kk, lr, ld, g, rw_r_k, rw_lnx_g, rw_lnx_b, bsz, seq, _tile(seq, 128))
    rw = rw.reshape(t, RWKV_W)
    x1, route = _outproj(x2, mla, rw, g_mla_out, w_o, g_ffn, w_group, b_group, w_expert, b_expert,
                         seq, _tile(seq, 512))
    out = _moe(x1, route, g_ffn, w_gate, w_up, w_down, _tile(t, MOE_TB))
    return out.reshape(bsz, seq, d)


def kernel(x, positions, g_mix, w_in, g_cq, w_uq, g_ckv, w_uk, w_uv, g_qn, g_kn, g_mla_out, rw_mu, rw_w0, rw_w2, rw_a0, rw_a2, rw_g2, rw_k_k, rw_k_a, rw_r_k, rw_lnx_g, rw_lnx_b, w_o, g_ffn, w_group, b_group, w_expert, b_expert, w_gate, w_up, w_down):
    for l in range(g_mix.shape[0]):
        x = _layer(x, positions, g_mix[l], w_in[l], g_cq[l], w_uq[l], g_ckv[l], w_uk[l], w_uv[l],
                   g_qn[l], g_kn[l], g_mla_out[l], rw_mu[l], rw_w0[l], rw_w2[l], rw_a0[l], rw_a2[l],
                   rw_g2[l], rw_k_k[l], rw_k_a[l], rw_r_k[l], rw_lnx_g[l], rw_lnx_b[l], w_o[l],
                   g_ffn[l], w_group[l], b_group[l], w_expert[l], b_expert[l], w_gate[l], w_up[l],
                   w_down[l])
    return x
```

```python
import functools

import numpy as np
import jax
import jax.numpy as jnp
from jax import lax
from jax.experimental import pallas as pl
from jax.experimental.pallas import tpu as pltpu

F32 = jnp.float32
BF16 = jnp.bfloat16

D_MODEL = 1024
MLA_HEADS = 8
MLA_NOPE = 64
MLA_ROPE = 32
MLA_QK = MLA_NOPE + MLA_ROPE
MLA_V = 64
MLA_W = MLA_HEADS * MLA_V
MLA_Q_RANK = 256
MLA_KV_RANK = 128
ROPE_THETA = 10000.0
RWKV_HEADS = 8
RWKV_N = 64
RWKV_W = RWKV_HEADS * RWKV_N
DECAY_LORA = 64
AAA_LORA = 64
GATE_LORA = 128
MLA_IN = MLA_Q_RANK + MLA_KV_RANK + MLA_ROPE
RWKV_IN = 3 * RWKV_W + DECAY_LORA + AAA_LORA + GATE_LORA
N_GROUPS = 4
EXPERTS_PER_GROUP = 8
N_EXPERTS = N_GROUPS * EXPERTS_PER_GROUP
D_EXPERT = 256
NORM_EPS = 1e-6
LNX_EPS = 64e-5

LANES = 128
QK_PAD = LANES
Q_BLK = 512
KV_BLK = 512
ATT_HEADS = 4
MOE_TB = 1024
MOE_RB = 96
MOE_EXPERTS_PER_STEP = 2
VT_ROWS = MLA_V + 16
Q_SCALE = MLA_QK ** -0.5 * float(np.log2(np.e))
PM_W = MLA_Q_RANK + MLA_KV_RANK + 2 * LANES
VMEM_LIMIT = 56 * 1024 * 1024


def _dot_bf16x3(a, b):
    a_hi = a.astype(BF16)
    a_lo = (a - a_hi.astype(F32)).astype(BF16)
    b_hi = b.astype(BF16)
    b_lo = (b - b_hi.astype(F32)).astype(BF16)
    d = lambda p, q: jnp.dot(p, q, preferred_element_type=F32)
    return d(a_hi, b_hi) + (d(a_hi, b_lo) + d(a_lo, b_hi))


def _dot(a, b):
    return jnp.dot(a.astype(BF16), b.astype(BF16), preferred_element_type=F32)


def _dot_nt(a, b):
    return lax.dot_general(a.astype(BF16), b.astype(BF16), (((1,), (1,)), ((), ())),
                           preferred_element_type=F32)


def _dot_tn(a, b):
    return lax.dot_general(a.astype(BF16), b.astype(BF16), (((0,), (0,)), ((), ())),
                           preferred_element_type=F32)


def _dot_split(a, b, split_lhs, passes=2):
    s = a if split_lhs else b
    out = None
    for _ in range(passes):
        piece = s.astype(BF16)
        term = (jnp.dot(piece, b, preferred_element_type=F32) if split_lhs
                else jnp.dot(a, piece, preferred_element_type=F32))
        out = term if out is None else out + term
        s = s - piece.astype(F32)
    return out


def _head_indicator(width):
    hid = np.arange(width) // RWKV_N
    return jnp.asarray((hid[:, None] == hid[None, :]).astype(np.float32)).astype(BF16)


def _params(sem):
    return pltpu.CompilerParams(dimension_semantics=sem, vmem_limit_bytes=VMEM_LIMIT)


def _inproj_kernel(x_ref, pos_ref, gmix_ref, wmla_ref, wrw_ref, freq_ref, gcq_ref, gckv_ref,
                   wq_ref, wqr_ref, wk_ref, wv_ref, gq_ref, gqr_ref, gk_ref, gkr_ref,
                   feat_ref, q_ref, k_ref, v_ref):
    x = x_ref[...]
    h = x * lax.rsqrt(jnp.mean(x * x, axis=-1, keepdims=True) + NORM_EPS) * gmix_ref[...]
    hb = h.astype(BF16)
    feat_ref[...] = jnp.dot(hb, wrw_ref[...], preferred_element_type=F32)
    pm = jnp.dot(hb, wmla_ref[...], preferred_element_type=F32)

    c_q = pm[:, :MLA_Q_RANK]
    c_kv = pm[:, MLA_Q_RANK:MLA_Q_RANK + MLA_KV_RANK]
    kr = pm[:, MLA_Q_RANK + MLA_KV_RANK:MLA_Q_RANK + MLA_KV_RANK + LANES]
    kr_rot = pm[:, MLA_Q_RANK + MLA_KV_RANK + LANES:PM_W]
    cqn = (c_q * lax.rsqrt(jnp.mean(c_q * c_q, axis=-1, keepdims=True) + NORM_EPS)
           * gcq_ref[...]).astype(BF16)
    ckvn = (c_kv * lax.rsqrt(jnp.mean(c_kv * c_kv, axis=-1, keepdims=True) + NORM_EPS)
            * gckv_ref[...]).astype(BF16)

    ang = pos_ref[...].astype(F32) * freq_ref[...]
    cosf = jnp.cos(ang)
    sinf = jnp.sin(ang)
    q_cos = gq_ref[...] * cosf * Q_SCALE
    q_sin = gqr_ref[...] * sinf * Q_SCALE
    k_cos = gk_ref[...] * cosf
    k_sin = kr_rot * (gkr_ref[...] * sinf)
    inv_qk = 1.0 / MLA_QK
    ones_rows = jnp.where(lax.broadcasted_iota(jnp.int32, (VT_ROWS - MLA_V, x.shape[0]), 0) == 0, 1.0, 0.0)

    for hh in range(MLA_HEADS):
        q_raw = jnp.dot(cqn, wq_ref[hh], preferred_element_type=F32)
        q_rot = jnp.dot(cqn, wqr_ref[hh], preferred_element_type=F32)
        sq = lax.rsqrt(jnp.sum(q_raw * q_raw, axis=-1, keepdims=True) * inv_qk + NORM_EPS)
        q_ref[0, hh] = (sq * (q_raw * q_cos + q_rot * q_sin)).astype(BF16)
        k_raw = jnp.dot(ckvn, wk_ref[hh], preferred_element_type=F32) + kr
        sk = lax.rsqrt(jnp.sum(k_raw * k_raw, axis=-1, keepdims=True) * inv_qk + NORM_EPS)
        k_ref[0, hh] = (sk * (k_raw * k_cos + k_sin)).astype(BF16)
        vt = lax.dot_general(wv_ref[hh], ckvn, (((1,), (1,)), ((), ())), preferred_element_type=F32)
        v_ref[0, hh, 0] = jnp.concatenate([vt, ones_rows], axis=0).astype(BF16)


def _rot_cols(w):
    half = MLA_ROPE // 2
    z = jnp.zeros_like(w[..., :MLA_NOPE])
    return jnp.concatenate([z, -w[..., MLA_NOPE + half:MLA_QK], w[..., MLA_NOPE:MLA_NOPE + half]], axis=-1)


def _pad_last(w, n):
    return jnp.pad(w, [(0, 0)] * (w.ndim - 1) + [(0, n - w.shape[-1])])


def _inproj(x2, pos2, g_mix, w_in, g_cq, w_uq, g_ckv, w_uk, w_uv, g_qn, g_kn, bsz, seq, tm, kv_blk):
    t = x2.shape[0]
    nt = seq // tm
    per_kv = kv_blk // tm
    half = MLA_ROPE // 2
    w_kr = w_in[:, MLA_Q_RANK + MLA_KV_RANK:MLA_IN]
    zeros64 = jnp.zeros((D_MODEL, MLA_NOPE), F32)
    zeros32 = jnp.zeros((D_MODEL, LANES - MLA_QK), F32)
    w_kr_rot = jnp.concatenate([-w_kr[:, half:], w_kr[:, :half]], axis=-1)
    w_mla = jnp.concatenate([w_in[:, :MLA_Q_RANK + MLA_KV_RANK], zeros64, w_kr, zeros32,
                             zeros64, w_kr_rot, zeros32], axis=-1).astype(BF16)
    w_rw = w_in[:, MLA_IN:].astype(BF16)
    wq = jnp.transpose(w_uq, (1, 0, 2))
    wq_rot = _pad_last(_rot_cols(wq), QK_PAD).astype(BF16)
    wq = _pad_last(wq, QK_PAD).astype(BF16)
    wk = _pad_last(jnp.transpose(w_uk, (1, 0, 2)), QK_PAD).astype(BF16)
    wv = jnp.transpose(w_uv, (1, 2, 0)).astype(BF16)
    inv_freq = 1.0 / (ROPE_THETA ** (jnp.arange(half, dtype=F32) / half))
    freq = jnp.concatenate([jnp.zeros((MLA_NOPE,), F32), inv_freq, inv_freq,
                            jnp.zeros((LANES - MLA_QK,), F32)])[None, :]
    swap = lambda g: jnp.concatenate([jnp.zeros((MLA_NOPE,), F32), g[MLA_NOPE + half:MLA_QK],
                                      g[MLA_NOPE:MLA_NOPE + half], jnp.zeros((LANES - MLA_QK,), F32)])[None, :]
    gq = _pad_last(g_qn[None, :], QK_PAD)
    gk = _pad_last(g_kn[None, :], QK_PAD)
    gq_rot = swap(g_qn)
    gk_rot = swap(g_kn)

    full = lambda a: pl.BlockSpec(a.shape, lambda i: (0,) * a.ndim)
    head_map = lambda i: (i // nt, 0, i % nt, 0)
    args = (x2, pos2, g_mix[None, :], w_mla, w_rw, freq, g_cq[None, :], g_ckv[None, :],
            wq, wq_rot, wk, wv, gq, gq_rot, gk, gk_rot)
    in_specs = [pl.BlockSpec((tm, D_MODEL), lambda i: (i, 0)),
                pl.BlockSpec((tm, 1), lambda i: (i, 0))] + [full(a) for a in args[2:]]
    return pl.pallas_call(
        _inproj_kernel,
        grid=(t // tm,),
        in_specs=in_specs,
        out_specs=[pl.BlockSpec((tm, RWKV_IN), lambda i: (i, 0)),
                   pl.BlockSpec((1, MLA_HEADS, tm, QK_PAD), head_map),
                   pl.BlockSpec((1, MLA_HEADS, tm, QK_PAD), head_map),
                   pl.BlockSpec((1, MLA_HEADS, 1, VT_ROWS, tm),
                                lambda i: (i // nt, 0, (i % nt) // per_kv, 0, (i % nt) % per_kv))],
        out_shape=[jax.ShapeDtypeStruct((t, RWKV_IN), F32),
                   jax.ShapeDtypeStruct((bsz, MLA_HEADS, seq, QK_PAD), BF16),
                   jax.ShapeDtypeStruct((bsz, MLA_HEADS, seq, QK_PAD), BF16),
                   jax.ShapeDtypeStruct((bsz, MLA_HEADS, seq // kv_blk, VT_ROWS, kv_blk), BF16)],
        compiler_params=_params(("parallel",)),
        name="inproj",
    )(*args)


def _attn_kernel(q_ref, k_ref, vt_ref, o_ref):
    qi = pl.program_id(2)
    nh = q_ref.shape[1]
    qb = q_ref.shape[2]
    kvb = vt_ref.shape[4]
    qs = [q_ref[0, h] for h in range(nh)]

    def scores(g):
        start = pl.multiple_of(g * kvb, kvb)
        return tuple(lax.dot_general(k_ref[0, h, pl.ds(start, kvb), :], qs[h], (((1,), (1,)), ((), ())),
                                     preferred_element_type=F32) for h in range(nh))

    def consume(g, ss, carry, masked):
        if masked:
            key = g * kvb + lax.broadcasted_iota(jnp.int32, (kvb, qb), 0)
            qry = qi * qb + lax.broadcasted_iota(jnp.int32, (kvb, qb), 1)
            ss = [jnp.where(key <= qry, s, -jnp.inf) for s in ss]
        ms = [jnp.maximum(carry[h][0], jnp.max(ss[h], axis=0, keepdims=True)) for h in range(nh)]
        ps = [jnp.exp2((ss[h] - ms[h]).astype(BF16)) for h in range(nh)]
        out = []
        for h in range(nh):
            m, acc = carry[h]
            alpha = jnp.exp2(m - ms[h])
            out.append((ms[h], alpha * acc + jnp.dot(vt_ref[0, h, g], ps[h], preferred_element_type=F32)))
        return tuple(out)

    init = tuple((jnp.full((1, qb), -1e30, F32), jnp.zeros((VT_ROWS, qb), F32)) for _ in range(nh))
    full_groups = (qi * qb) // kvb
    carry = lax.fori_loop(0, full_groups, lambda g, c: consume(g, scores(g), c, False), init)
    carry = consume(full_groups, scores(full_groups), carry, True)
    ot = jnp.concatenate([acc[:MLA_V] * (1.0 / acc[MLA_V:MLA_V + 1]) for _, acc in carry], axis=0)
    o_ref[0] = ot.T


def _attn(q, k, vt):
    bsz, nh, seq, _ = q.shape
    per_step = ATT_HEADS
    ngrp, kvb = vt.shape[2], vt.shape[4]
    qb = min(Q_BLK, kvb)
    return pl.pallas_call(
        _attn_kernel,
        grid=(bsz, nh // per_step, seq // qb),
        in_specs=[pl.BlockSpec((1, per_step, qb, QK_PAD), lambda b, h, i: (b, h, i, 0)),
                  pl.BlockSpec((1, per_step, seq, QK_PAD), lambda b, h, i: (b, h, 0, 0)),
                  pl.BlockSpec((1, per_step, ngrp, VT_ROWS, kvb), lambda b, h, i: (b, h, 0, 0, 0))],
        out_specs=pl.BlockSpec((1, qb, per_step * MLA_V), lambda b, h, i: (b, i, h)),
        out_shape=jax.ShapeDtypeStruct((bsz, seq, nh * MLA_V), F32),
        compiler_params=_params(("parallel", "parallel", "arbitrary")),
        name="attn",
    )(q, k, vt)


def _rwprep_kernel(feat_ref, prev_ref, mu_ref, w0_ref, w2_ref, a0_ref, a2_ref, g2_ref, kk_ref, ka_ref,
                   bd_ref, r_o, k_o, v_o, kk_o, lr_o, ld_o, g_o, *, nt):
    i = pl.program_id(0)
    feat = feat_ref[...]
    tm = feat.shape[0]
    first = jnp.where(i % nt == 0, 0.0, 1.0)
    prev_row = prev_ref[7:8, :] * first
    rolled = pltpu.roll(feat, 1, axis=0)
    rid = lax.broadcasted_iota(jnp.int32, (tm, 1), 0)
    prev = jnp.where(rid == 0, prev_row, rolled)
    feat = feat + (prev - feat) * mu_ref[...]

    r = feat[:, :RWKV_W]
    k = feat[:, RWKV_W:2 * RWKV_W]
    v = feat[:, 2 * RWKV_W:3 * RWKV_W]
    wa = feat[:, 3 * RWKV_W:3 * RWKV_W + LANES]
    gl = feat[:, 3 * RWKV_W + LANES:]
    lane = lax.broadcasted_iota(jnp.int32, (tm, LANES), 1)
    wl_in = jnp.where(lane < DECAY_LORA, jnp.tanh(wa), 0.0)
    al_in = jnp.where(lane < DECAY_LORA, 0.0, wa)
    wpre = w0_ref[...] + _dot(wl_in, w2_ref[...])
    w = -(jnp.maximum(-wpre, 0.0) + jnp.log(1.0 + jnp.exp(-jnp.abs(wpre)))) - 0.5
    logd = -jnp.exp(w)
    a = jax.nn.sigmoid(a0_ref[...] + _dot(al_in, a2_ref[...]))
    g = _dot(jax.nn.sigmoid(gl), g2_ref[...])
    kk = k * kk_ref[...]
    ss = _dot_split(kk * kk, bd_ref[...], split_lhs=True)
    r_o[...] = r
    k_o[...] = k * (1.0 + (a - 1.0) * ka_ref[...])
    v_o[...] = v
    kk_o[...] = kk * lax.rsqrt(ss + 1e-12)
    lr_o[...] = a
    ld_o[...] = logd
    g_o[...] = g


def _rwprep(feat, rw_mu, rw_w0, rw_w2, rw_a0, rw_a2, rw_g2, rw_k_k, rw_k_a, bsz, seq, tm):
    t = feat.shape[0]
    nt = seq // tm
    w2p = jnp.concatenate([rw_w2, jnp.zeros((LANES - DECAY_LORA, RWKV_W), F32)], axis=0).astype(BF16)
    a2p = jnp.concatenate([jnp.zeros((DECAY_LORA, RWKV_W), F32), rw_a2], axis=0).astype(BF16)
    args = (feat, feat, rw_mu[None, :], rw_w0[None, :], w2p, rw_a0[None, :], a2p, rw_g2.astype(BF16),
            rw_k_k[None, :], rw_k_a[None, :], _head_indicator(RWKV_W))
    full = lambda a: pl.BlockSpec(a.shape, lambda i: (0,) * a.ndim)
    rows8 = tm // 8
    in_specs = [pl.BlockSpec((tm, RWKV_IN), lambda i: (i, 0)),
                pl.BlockSpec((8, RWKV_IN), lambda i: (jnp.maximum(i * rows8 - 1, 0), 0))]
    in_specs += [full(a) for a in args[2:]]
    hshape = jax.ShapeDtypeStruct((t, RWKV_W), F32)
    hspec = pl.BlockSpec((tm, RWKV_W), lambda i: (i, 0))
    return pl.pallas_call(
        functools.partial(_rwprep_kernel, nt=nt),
        grid=(t // tm,),
        in_specs=in_specs,
        out_specs=[hspec] * 7,
        out_shape=[hshape] * 7,
        compiler_params=_params(("parallel",)),
        name="rwprep",
    )(*args)


def _rwscan_kernel(r_ref, k_ref, v_ref, kk_ref, lr_ref, ld_ref, g_ref, rk_ref, lng_ref, lnb_ref,
                   tri_ref, bd_ref, o_ref, state, *, chunk):
    c = pl.program_id(1)

    @pl.when(c == 0)
    def _():
        state[...] = jnp.zeros_like(state)

    C = chunk
    npairs = RWKV_W // LANES
    r = r_ref[0]
    k = k_ref[0]
    v = v_ref[0]
    kk = kk_ref[0]
    ld = ld_ref[0]
    bd = bd_ref[...]

    cum = _dot_split(tri_ref[...], ld, split_lhs=False)
    mid = C // 2 - 1
    cmid = cum[mid:mid + 1, :]
    cend = cum[C - 1:C, :]
    e_pos = jnp.exp(cum - cmid)
    e_prev = jnp.exp(cum - ld - cmid)
    e_neg = jnp.exp(cmid - cum)
    g_mid = jnp.exp(cmid)
    g_end_rel = jnp.exp(cend - cmid)
    g_end = jnp.exp(cend)
    rt = r * e_pos
    at = -kk * e_prev
    bt = kk * lr_ref[0] * e_neg
    kt = k * e_neg
    at_g = (at * g_mid).astype(BF16)
    rt_g = (rt * g_mid).astype(BF16)
    art = jnp.concatenate([at, rt], axis=0)
    bk = jnp.concatenate([bt, kt], axis=0).astype(BF16)
    bkg = jnp.concatenate([bt * g_end_rel, kt * g_end_rel], axis=0).astype(BF16)
    vb = v.astype(BF16)

    row = lax.broadcasted_iota(jnp.int32, (C, C), 0)
    col = lax.broadcasted_iota(jnp.int32, (C, C), 1)
    strict = row > col
    eye = (row == col).astype(F32)
    row2 = lax.broadcasted_iota(jnp.int32, (C, 2 * C), 0)
    col2 = lax.broadcasted_iota(jnp.int32, (C, 2 * C), 1)
    incl2 = row2 >= jnp.where(col2 >= C, col2 - C, col2)
    low2 = lax.broadcasted_iota(jnp.int32, (2 * C, LANES), 1) < RWKV_N
    low = lax.broadcasted_iota(jnp.int32, (C, LANES), 1) < RWKV_N
    srow = lax.broadcasted_iota(jnp.int32, (LANES, LANES), 0)
    scol = lax.broadcasted_iota(jnp.int32, (LANES, LANES), 1)
    same_head = (srow < RWKV_N) == (scol < RWKV_N)
    diag = srow == scol
    pair = lambda a, p: a[:, p * LANES:(p + 1) * LANES]

    a_ab, a_ak, a_rbk = [], [], []
    for p in range(npairs):
        art_p = pair(art, p)
        bk_p = pair(bk, p)
        for half in range(2):
            lhs = jnp.where(low2, art_p, 0.0) if half == 0 else jnp.where(low2, 0.0, art_p)
            big = _dot_nt(lhs, bk_p)
            a_ab.append(jnp.where(strict, big[:C, :C], 0.0))
            a_ak.append(jnp.where(strict, big[:C, C:], 0.0).astype(BF16))
            a_rbk.append(jnp.where(incl2, big[C:, :], 0.0).astype(BF16))

    ps = [a.astype(BF16) for a in a_ab]
    tinv = [eye + a for a in a_ab]
    n = 2
    while n < C:
        sq = [jnp.dot(pb, pb, preferred_element_type=F32) for pb in ps]
        ps = [s.astype(BF16) for s in sq]
        tinv = [t + jnp.dot(t.astype(BF16), pb, preferred_element_type=F32) for t, pb in zip(tinv, ps)]
        n *= 2
    tinv = [t.astype(BF16) for t in tinv]

    s0 = [state[p] for p in range(npairs)]
    s0b = [s.astype(BF16) for s in s0]
    sel = lambda h0, h1: jnp.where(low, h0, h1)
    mm = lambda a, b: jnp.dot(a, b, preferred_element_type=F32)
    xs = [mm(pair(at_g, p), s0b[p]) + sel(mm(a_ak[2 * p], pair(vb, p)), mm(a_ak[2 * p + 1], pair(vb, p)))
          for p in range(npairs)]
    xb = [x.astype(BF16) for x in xs]
    us = [sel(mm(tinv[2 * p], xb[p]), mm(tinv[2 * p + 1], xb[p])) for p in range(npairs)]
    uv = [jnp.concatenate([us[p].astype(BF16), pair(vb, p)], axis=0) for p in range(npairs)]
    ys = [mm(pair(rt_g, p), s0b[p]) + sel(mm(a_rbk[2 * p], uv[p]), mm(a_rbk[2 * p + 1], uv[p]))
          for p in range(npairs)]
    for p in range(npairs):
        upd = lax.dot_general(pair(bkg, p), uv[p], (((0,), (0,)), ((), ())), preferred_element_type=F32)
        g_col = jnp.sum(jnp.where(diag, pair(g_end, p), 0.0), axis=-1, keepdims=True)
        state[p] = s0[p] * g_col + jnp.where(same_head, upd, 0.0)

    inv_n = 1.0 / RWKV_N
    for p in range(npairs):
        sl = slice(p * LANES, (p + 1) * LANES)
        y = ys[p]
        mean = _dot_split(y, bd, split_lhs=True) * inv_n
        yc = y - mean
        var = _dot_split(yc * yc, bd, split_lhs=True) * inv_n
        yn = yc * lax.rsqrt(var + LNX_EPS) * lng_ref[:, sl] + lnb_ref[:, sl]
        bonus = _dot_split(pair(r, p) * pair(k, p) * rk_ref[:, sl], bd, split_lhs=True) * pair(v, p)
        o_ref[0, :, sl] = (yn + bonus) * g_ref[0, :, sl]


def _rwscan(r, k, v, kk, lr, ld, g, rw_r_k, rw_lnx_g, rw_lnx_b, bsz, seq, chunk):
    width = r.shape[-1]
    tri = jnp.asarray(np.tril(np.ones((chunk, chunk), np.float32))).astype(BF16)
    bd = _head_indicator(LANES)
    seq3 = lambda a: a.reshape(bsz, seq, width)
    cspec = pl.BlockSpec((1, chunk, width), lambda b, c: (b, c, 0))
    small = pl.BlockSpec((1, width), lambda b, c: (0, 0))
    return pl.pallas_call(
        functools.partial(_rwscan_kernel, chunk=chunk),
        grid=(bsz, seq // chunk),
        in_specs=[cspec] * 7 + [small] * 3 + [pl.BlockSpec((chunk, chunk), lambda b, c: (0, 0)),
                                              pl.BlockSpec((LANES, LANES), lambda b, c: (0, 0))],
        out_specs=cspec,
        out_shape=jax.ShapeDtypeStruct((bsz, seq, width), F32),
        scratch_shapes=[pltpu.VMEM((width // LANES, LANES, LANES), F32)],
        compiler_params=_params(("parallel", "arbitrary")),
        name="rwscan",
    )(seq3(r), seq3(k), seq3(v), seq3(kk), seq3(lr), seq3(ld), seq3(g), rw_r_k.reshape(1, width),
      rw_lnx_g[None, :], rw_lnx_b[None, :], tri, bd)


def _outproj_kernel(x_ref, mla_ref, rw_ref, gmo_ref, wo_ref, gffn_ref, wr_ref, br_ref,
                    x1_ref, comb_ref):
    tm = x_ref.shape[0]
    m = mla_ref[...]
    mn = m * lax.rsqrt(jnp.mean(m * m, axis=-1, keepdims=True) + NORM_EPS) * gmo_ref[...]
    mix = jnp.concatenate([mn.astype(BF16), rw_ref[...].astype(BF16)], axis=-1)
    acc = x_ref[...] + jnp.dot(mix, wo_ref[...], preferred_element_type=F32)
    x1_ref[...] = acc
    xn = acc * lax.rsqrt(jnp.mean(acc * acc, axis=-1, keepdims=True) + NORM_EPS) * gffn_ref[...]

    logits = _dot_bf16x3(xn, wr_ref[...]) + br_ref[...]
    lane = lax.broadcasted_iota(jnp.int32, (tm, LANES), 1)
    neg = -jnp.inf
    big = jnp.int32(1 << 20)
    is_g = (lane >= N_EXPERTS) & (lane < N_EXPERTS + N_GROUPS)
    glog = jnp.where(is_g, logits, neg)
    gmax = jnp.max(glog, axis=-1, keepdims=True)
    gsel = jnp.min(jnp.where(glog == gmax, lane, big), axis=-1, keepdims=True) - N_EXPERTS
    gprob = 1.0 / jnp.sum(jnp.exp(glog - gmax), axis=-1, keepdims=True)
    lo = gsel * EXPERTS_PER_GROUP
    in_group = (lane >= lo) & (lane < lo + EXPERTS_PER_GROUP)
    elog = jnp.where(in_group, logits, neg)
    v1 = jnp.max(elog, axis=-1, keepdims=True)
    i1 = jnp.min(jnp.where(elog == v1, lane, big), axis=-1, keepdims=True)
    elog2 = jnp.where(lane == i1, neg, elog)
    v2 = jnp.max(elog2, axis=-1, keepdims=True)
    i2 = jnp.min(jnp.where(elog2 == v2, lane, big), axis=-1, keepdims=True)
    e2 = jnp.exp(v2 - v1)
    w1 = gprob / (1.0 + e2)
    w2 = gprob * e2 / (1.0 + e2)
    comb_ref[...] = jnp.where(lane == 0, i1.astype(F32), jnp.where(lane == 1, i2.astype(F32),
                              jnp.where(lane == 2, w1, jnp.where(lane == 3, w2, 0.0))))


def _outproj(x2, mla, rw, g_mla_out, w_o, g_ffn, w_group, b_group, w_expert, b_expert, seq, tm):
    t = x2.shape[0]
    wo = w_o.astype(BF16)
    pad = LANES - N_EXPERTS - N_GROUPS
    wr = jnp.concatenate([w_expert, w_group, jnp.zeros((D_MODEL, pad), F32)], axis=-1)
    br = jnp.concatenate([b_expert, b_group, jnp.zeros((pad,), F32)])[None, :]
    gmo = g_mla_out[None, :]
    full = lambda a: pl.BlockSpec(a.shape, lambda i: (0,) * a.ndim)
    row = pl.BlockSpec((tm, D_MODEL), lambda i: (i, 0))
    return pl.pallas_call(
        _outproj_kernel,
        grid=(t // tm,),
        in_specs=[row,
                  pl.BlockSpec((tm, MLA_W), lambda i: (i, 0)),
                  pl.BlockSpec((tm, RWKV_W), lambda i: (i, 0)),
                  full(gmo), full(wo), pl.BlockSpec((1, D_MODEL), lambda i: (0, 0)), full(wr), full(br)],
        out_specs=[row, pl.BlockSpec((tm, LANES), lambda i: (i, 0))],
        out_shape=[jax.ShapeDtypeStruct((t, D_MODEL), F32),
                   jax.ShapeDtypeStruct((t, LANES), F32)],
        compiler_params=_params(("parallel",)),
        name="outproj",
    )(x2, mla, rw, gmo, wo, g_ffn[None, :], wr, br)


def _moeplan_kernel(route_ref, tri_ref, triu_ref, posw_ref, nb_ref, roff_ref):
    rt = route_ref[...]
    tb = rt.shape[0]
    lane = lax.broadcasted_iota(jnp.int32, (tb, LANES), 1)
    oh1 = lane == rt[:, 0:1].astype(jnp.int32)
    oh2 = lane == rt[:, 1:2].astype(jnp.int32)
    oh = jnp.where(oh1, 1.0, jnp.where(oh2, 1.0, 0.0))
    excl = jnp.dot(tri_ref[...], oh.astype(BF16), preferred_element_type=F32)
    cnt = jnp.sum(oh, axis=0, keepdims=True)
    nb = jnp.floor((cnt + (MOE_RB - 0.5)) * (1.0 / MOE_RB))
    boff = jnp.dot(jnp.broadcast_to(nb, (8, LANES)).astype(BF16), triu_ref[...],
                   preferred_element_type=F32)[0:1]
    roff = boff * MOE_RB
    base = roff + excl
    pos1 = jnp.sum(jnp.where(oh1, base, 0.0), axis=-1, keepdims=True)
    pos2 = jnp.sum(jnp.where(oh2, base, 0.0), axis=-1, keepdims=True)
    posw_ref[...] = jnp.where(lane == 0, pos1, jnp.where(lane == 1, pos2, rt))
    nb_ref[0] = nb.astype(jnp.int32)
    roff_ref[0] = roff.astype(jnp.int32)


def _moeplan(route, tb):
    t = route.shape[0]
    nt = t // tb
    tri = jnp.asarray(np.tril(np.ones((tb, tb), np.float32), -1)).astype(BF16)
    triu = jnp.asarray(np.triu(np.ones((LANES, LANES), np.float32), 1)).astype(BF16)
    small = pl.BlockSpec((1, 1, LANES), lambda i: (i, 0, 0))
    return pl.pallas_call(
        _moeplan_kernel,
        grid=(nt,),
        in_specs=[pl.BlockSpec((tb, LANES), lambda i: (i, 0)),
                  pl.BlockSpec((tb, tb), lambda i: (0, 0)),
                  pl.BlockSpec((LANES, LANES), lambda i: (0, 0))],
        out_specs=[pl.BlockSpec((tb, LANES), lambda i: (i, 0)), small, small],
        out_shape=[jax.ShapeDtypeStruct((t, LANES), F32),
                   jax.ShapeDtypeStruct((nt, 1, LANES), jnp.int32),
                   jax.ShapeDtypeStruct((nt, 1, LANES), jnp.int32)],
        compiler_params=_params(("parallel",)),
        name="moeplan",
    )(route, tri, triu)


def _moe_kernel(pos_ref, nb_ref, roff_ref, x1_ref, posw_ref, gffn_ref, wg_hbm, wu_hbm, wd_hbm, o_ref,
                xn, xs, g1, g2, wg_buf, wu_buf, wd_buf, sem):
    i = pl.program_id(0)
    nt = pl.num_programs(0)
    tb = x1_ref.shape[0]
    chunk = g1.shape[0]
    dummy_r0 = xs.shape[0] - MOE_RB
    group = MOE_EXPERTS_PER_STEP
    ngroups = N_EXPERTS // group

    def weight_copies(e, slot):
        return (pltpu.make_async_copy(wg_hbm.at[e], wg_buf.at[slot], sem.at[slot, 0]),
                pltpu.make_async_copy(wu_hbm.at[e], wu_buf.at[slot], sem.at[slot, 1]),
                pltpu.make_async_copy(wd_hbm.at[e], wd_buf.at[slot], sem.at[slot, 2]))

    @pl.when(i == 0)
    def _():
        for j in range(group):
            for c in weight_copies(j, j):
                c.start()
        xs[...] = jnp.zeros_like(xs)

    for c in range(tb // chunk):
        sl = slice(c * chunk, (c + 1) * chunk)
        x1 = x1_ref[sl, :]
        xn[sl, :] = x1 * lax.rsqrt(jnp.mean(x1 * x1, axis=-1, keepdims=True) + NORM_EPS) * gffn_ref[...]

    def put(tok, carry):
        row = xn[pl.ds(tok, 1), :]
        xs[pl.ds(pos_ref[2 * tok], 1), :] = row
        xs[pl.ds(pos_ref[2 * tok + 1], 1), :] = row
        return carry

    lax.fori_loop(0, tb, put, 0, unroll=8)

    def expert_group(p, carry):
        base = lax.rem(p, 2) * group
        nxt = p + 1

        @pl.when(jnp.logical_or(nxt < ngroups, i + 1 < nt))
        def _():
            first = jnp.where(nxt < ngroups, nxt, 0) * group
            for j in range(group):
                for c in weight_copies(first + j, group - base + j):
                    c.start()

        for j in range(group):
            for c in weight_copies(p * group + j, base + j):
                c.wait()
        nbs = [nb_ref[0, 0, p * group + j] for j in range(group)]
        roffs = [roff_ref[0, 0, p * group + j] for j in range(group)]
        trips = functools.reduce(jnp.maximum, nbs)

        def block(b, c2):
            r0s = [pl.multiple_of(jnp.where(b < nbs[j], roffs[j] + b * MOE_RB, dummy_r0), 8)
                   for j in range(group)]
            x = [xs[pl.ds(r0, MOE_RB), :].astype(BF16) for r0 in r0s]
            hg = [jnp.dot(x[j], wg_buf[base + j], preferred_element_type=F32) for j in range(group)]
            hu = [jnp.dot(x[j], wu_buf[base + j], preferred_element_type=F32) for j in range(group)]
            hid = [(hg[j] * jax.nn.sigmoid(hg[j]) * hu[j]).astype(BF16) for j in range(group)]
            y = [jnp.dot(hid[j], wd_buf[base + j], preferred_element_type=F32) for j in range(group)]
            for j in range(group):
                xs[pl.ds(r0s[j], MOE_RB), :] = y[j]
            return c2

        lax.fori_loop(0, trips, block, 0)
        return carry

    lax.fori_loop(0, ngroups, expert_group, 0)

    for c in range(tb // chunk):
        def take(j, carry, c=c):
            tok = c * chunk + j
            g1[pl.ds(j, 1), :] = xs[pl.ds(pos_ref[2 * tok], 1), :]
            g2[pl.ds(j, 1), :] = xs[pl.ds(pos_ref[2 * tok + 1], 1), :]
            return carry

        lax.fori_loop(0, chunk, take, 0, unroll=8)
        sl = slice(c * chunk, (c + 1) * chunk)
        o_ref[sl, :] = x1_ref[sl, :] + (posw_ref[sl, 2:3] * g1[...] + posw_ref[sl, 3:4] * g2[...])


def _moe(x1, route, g_ffn, w_gate, w_up, w_down, tb):
    t = x1.shape[0]
    nt = t // tb
    chunk = min(tb, 256)
    rows = 2 * tb + N_EXPERTS * MOE_RB + MOE_RB
    nslot = 2 * MOE_EXPERTS_PER_STEP
    posw, nb, roff = _moeplan(route, tb)
    pos = posw[:, 0:2].astype(jnp.int32).reshape(2 * t)
    smem3 = lambda: pl.BlockSpec((1, 1, LANES), lambda i: (i, 0, 0), memory_space=pltpu.SMEM)
    hbm = lambda: pl.BlockSpec(memory_space=pl.ANY)
    return pl.pallas_call(
        _moe_kernel,
        grid=(nt,),
        in_specs=[pl.BlockSpec((2 * tb,), lambda i: (i,), memory_space=pltpu.SMEM), smem3(), smem3(),
                  pl.BlockSpec((tb, D_MODEL), lambda i: (i, 0)),
                  pl.BlockSpec((tb, LANES), lambda i: (i, 0)),
                  pl.BlockSpec((1, D_MODEL), lambda i: (0, 0)),
                  hbm(), hbm(), hbm()],
        out_specs=pl.BlockSpec((tb, D_MODEL), lambda i: (i, 0)),
        out_shape=jax.ShapeDtypeStruct((t, D_MODEL), F32),
        scratch_shapes=[pltpu.VMEM((tb, D_MODEL), F32), pltpu.VMEM((rows, D_MODEL), F32),
                        pltpu.VMEM((chunk, D_MODEL), F32), pltpu.VMEM((chunk, D_MODEL), F32),
                        pltpu.VMEM((nslot, D_MODEL, D_EXPERT), BF16), pltpu.VMEM((nslot, D_MODEL, D_EXPERT), BF16),
                        pltpu.VMEM((nslot, D_EXPERT, D_MODEL), BF16), pltpu.SemaphoreType.DMA((nslot, 3))],
        compiler_params=_params(("arbitrary",)),
        name="moe",
    )(pos, nb, roff, x1, posw, g_ffn[None, :], w_gate.astype(BF16), w_up.astype(BF16), w_down.astype(BF16))


def _tile(seq, want):
    tm = min(seq, want)
    assert seq % tm == 0 and tm % 8 == 0
    return tm


def _layer(x, positions, g_mix, w_in, g_cq, w_uq, g_ckv, w_uk, w_uv, g_qn, g_kn, g_mla_out,
           rw_mu, rw_w0, rw_w2, rw_a0, rw_a2, rw_g2, rw_k_k, rw_k_a, rw_r_k, rw_lnx_g, rw_lnx_b,
           w_o, g_ffn, w_group, b_group, w_expert, b_expert, w_gate, w_up, w_down):
    bsz, seq, d = x.shape
    t = bsz * seq
    x2 = x.reshape(t, d)
    pos2 = positions.reshape(t, 1)
    kv_blk = _tile(seq, KV_BLK)
    feat, q, k, v = _inproj(x2, pos2, g_mix, w_in, g_cq, w_uq, g_ckv, w_uk, w_uv, g_qn, g_kn,
                            bsz, seq, _tile(kv_blk, 512), kv_blk)
    mla = _attn(q, k, v).reshape(t, MLA_W)
    r, k2, vv, kk, lr, ld, g = _rwprep(feat, rw_mu, rw_w0, rw_w2, rw_a0, rw_a2, rw_g2, rw_k_k, rw_k_a,
                                       bsz, seq, _tile(seq, 512))
    rw = _rwscan(r, k2, vv, kk, lr, ld, g, rw_r_k, rw_lnx_g, rw_lnx_b, bsz, seq, _tile(seq, 128))
    rw = rw.reshape(t, RWKV_W)
    x1, route = _outproj(x2, mla, rw, g_mla_out, w_o, g_ffn, w_group, b_group, w_expert, b_expert,
                         seq, _tile(seq, 512))
    out = _moe(x1, route, g_ffn, w_gate, w_up, w_down, _tile(t, MOE_TB))
    return out.reshape(bsz, seq, d)


def kernel(x, positions, g_mix, w_in, g_cq, w_uq, g_ckv, w_uk, w_uv, g_qn, g_kn, g_mla_out, rw_mu, rw_w0, rw_w2, rw_a0, rw_a2, rw_g2, rw_k_k, rw_k_a, rw_r_k, rw_lnx_g, rw_lnx_b, w_o, g_ffn, w_group, b_group, w_expert, b_expert, w_gate, w_up, w_down):
    for l in range(g_mix.shape[0]):
        x = _layer(x, positions, g_mix[l], w_in[l], g_cq[l], w_uq[l], g_ckv[l], w_uk[l], w_uv[l],
                   g_qn[l], g_kn[l], g_mla_out[l], rw_mu[l], rw_w0[l], rw_w2[l], rw_a0[l], rw_a2[l],
                   rw_g2[l], rw_k_k[l], rw_k_a[l], rw_r_k[l], rw_lnx_g[l], rw_lnx_b[l], w_o[l],
                   g_ffn[l], w_group[l], b_group[l], w_expert[l], b_expert[l], w_gate[l], w_up[l],
                   w_down[l])
    return x
```

```python
import functools

import numpy as np
import jax
import jax.numpy as jnp
from jax import lax
from jax.experimental import pallas as pl
from jax.experimental.pallas import tpu as pltpu

F32 = jnp.float32
BF16 = jnp.bfloat16

D_MODEL = 1024
MLA_HEADS = 8
MLA_NOPE = 64
MLA_ROPE = 32
MLA_QK = MLA_NOPE + MLA_ROPE
MLA_V = 64
MLA_W = MLA_HEADS * MLA_V
MLA_Q_RANK = 256
MLA_KV_RANK = 128
ROPE_THETA = 10000.0
RWKV_HEADS = 8
RWKV_N = 64
RWKV_W = RWKV_HEADS * RWKV_N
DECAY_LORA = 64
AAA_LORA = 64
GATE_LORA = 128
MLA_IN = MLA_Q_RANK + MLA_KV_RANK + MLA_ROPE
RWKV_IN = 3 * RWKV_W + DECAY_LORA + AAA_LORA + GATE_LORA
N_GROUPS = 4
EXPERTS_PER_GROUP = 8
N_EXPERTS = N_GROUPS * EXPERTS_PER_GROUP
D_EXPERT = 256
NORM_EPS = 1e-6
LNX_EPS = 64e-5

LANES = 128
QK_PAD = LANES
Q_BLK = 512
KV_BLK = 512
ATT_HEADS = 4
MOE_TB = 1024
MOE_RB = 96
MOE_EXPERTS_PER_STEP = 2
VT_ROWS = MLA_V + 16
Q_SCALE = MLA_QK ** -0.5 * float(np.log2(np.e))
PM_W = MLA_Q_RANK + MLA_KV_RANK + 2 * LANES
VMEM_LIMIT = 56 * 1024 * 1024


def _dot_bf16x3(a, b):
    a_hi = a.astype(BF16)
    a_lo = (a - a_hi.astype(F32)).astype(BF16)
    b_hi = b.astype(BF16)
    b_lo = (b - b_hi.astype(F32)).astype(BF16)
    d = lambda p, q: jnp.dot(p, q, preferred_element_type=F32)
    return d(a_hi, b_hi) + (d(a_hi, b_lo) + d(a_lo, b_hi))


def _dot(a, b):
    return jnp.dot(a.astype(BF16), b.astype(BF16), preferred_element_type=F32)


def _dot_nt(a, b):
    return lax.dot_general(a.astype(BF16), b.astype(BF16), (((1,), (1,)), ((), ())),
                           preferred_element_type=F32)


def _dot_tn(a, b):
    return lax.dot_general(a.astype(BF16), b.astype(BF16), (((0,), (0,)), ((), ())),
                           preferred_element_type=F32)


def _dot_split(a, b, split_lhs, passes=2):
    s = a if split_lhs else b
    out = None
    for _ in range(passes):
        piece = s.astype(BF16)
        term = (jnp.dot(piece, b, preferred_element_type=F32) if split_lhs
                else jnp.dot(a, piece, preferred_element_type=F32))
        out = term if out is None else out + term
        s = s - piece.astype(F32)
    return out


def _head_indicator(width):
    hid = np.arange(width) // RWKV_N
    return jnp.asarray((hid[:, None] == hid[None, :]).astype(np.float32)).astype(BF16)


def _params(sem):
    return pltpu.CompilerParams(dimension_semantics=sem, vmem_limit_bytes=VMEM_LIMIT)


def _inproj_kernel(x_ref, pos_ref, gmix_ref, wmla_ref, wrw_ref, freq_ref, gcq_ref, gckv_ref,
                   wq_ref, wqr_ref, wk_ref, wv_ref, gq_ref, gqr_ref, gk_ref, gkr_ref,
                   feat_ref, q_ref, k_ref, v_ref):
    x = x_ref[...]
    h = x * lax.rsqrt(jnp.mean(x * x, axis=-1, keepdims=True) + NORM_EPS) * gmix_ref[...]
    hb = h.astype(BF16)
    feat_ref[...] = jnp.dot(hb, wrw_ref[...], preferred_element_type=F32)
    pm = jnp.dot(hb, wmla_ref[...], preferred_element_type=F32)

    c_q = pm[:, :MLA_Q_RANK]
    c_kv = pm[:, MLA_Q_RANK:MLA_Q_RANK + MLA_KV_RANK]
    kr = pm[:, MLA_Q_RANK + MLA_KV_RANK:MLA_Q_RANK + MLA_KV_RANK + LANES]
    kr_rot = pm[:, MLA_Q_RANK + MLA_KV_RANK + LANES:PM_W]
    cqn = (c_q * lax.rsqrt(jnp.mean(c_q * c_q, axis=-1, keepdims=True) + NORM_EPS)
           * gcq_ref[...]).astype(BF16)
    ckvn = (c_kv * lax.rsqrt(jnp.mean(c_kv * c_kv, axis=-1, keepdims=True) + NORM_EPS)
            * gckv_ref[...]).astype(BF16)

    ang = pos_ref[...].astype(F32) * freq_ref[...]
    cosf = jnp.cos(ang)
    sinf = jnp.sin(ang)
    q_cos = gq_ref[...] * cosf * Q_SCALE
    q_sin = gqr_ref[...] * sinf * Q_SCALE
    k_cos = gk_ref[...] * cosf
    k_sin = kr_rot * (gkr_ref[...] * sinf)
    inv_qk = 1.0 / MLA_QK
    ones_rows = jnp.where(lax.broadcasted_iota(jnp.int32, (VT_ROWS - MLA_V, x.shape[0]), 0) == 0, 1.0, 0.0)

    for hh in range(MLA_HEADS):
        q_raw = jnp.dot(cqn, wq_ref[hh], preferred_element_type=F32)
        q_rot = jnp.dot(cqn, wqr_ref[hh], preferred_element_type=F32)
        sq = lax.rsqrt(jnp.sum(q_raw * q_raw, axis=-1, keepdims=True) * inv_qk + NORM_EPS)
        q_ref[0, hh] = (sq * (q_raw * q_cos + q_rot * q_sin)).astype(BF16)
        k_raw = jnp.dot(ckvn, wk_ref[hh], preferred_element_type=F32) + kr
        sk = lax.rsqrt(jnp.sum(k_raw * k_raw, axis=-1, keepdims=True) * inv_qk + NORM_EPS)
        k_ref[0, hh] = (sk * (k_raw * k_cos + k_sin)).astype(BF16)
        vt = lax.dot_general(wv_ref[hh], ckvn, (((1,), (1,)), ((), ())), preferred_element_type=F32)
        v_ref[0, hh, 0] = jnp.concatenate([vt, ones_rows], axis=0).astype(BF16)


def _rot_cols(w):
    half = MLA_ROPE // 2
    z = jnp.zeros_like(w[..., :MLA_NOPE])
    return jnp.concatenate([z, -w[..., MLA_NOPE + half:MLA_QK], w[..., MLA_NOPE:MLA_NOPE + half]], axis=-1)


def _pad_last(w, n):
    return jnp.pad(w, [(0, 0)] * (w.ndim - 1) + [(0, n - w.shape[-1])])


def _inproj(x2, pos2, g_mix, w_in, g_cq, w_uq, g_ckv, w_uk, w_uv, g_qn, g_kn, bsz, seq, tm, kv_blk):
    t = x2.shape[0]
    nt = seq // tm
    per_kv = kv_blk // tm
    half = MLA_ROPE // 2
    w_kr = w_in[:, MLA_Q_RANK + MLA_KV_RANK:MLA_IN]
    zeros64 = jnp.zeros((D_MODEL, MLA_NOPE), F32)
    zeros32 = jnp.zeros((D_MODEL, LANES - MLA_QK), F32)
    w_kr_rot = jnp.concatenate([-w_kr[:, half:], w_kr[:, :half]], axis=-1)
    w_mla = jnp.concatenate([w_in[:, :MLA_Q_RANK + MLA_KV_RANK], zeros64, w_kr, zeros32,
                             zeros64, w_kr_rot, zeros32], axis=-1).astype(BF16)
    w_rw = w_in[:, MLA_IN:].astype(BF16)
    wq = jnp.transpose(w_uq, (1, 0, 2))
    wq_rot = _pad_last(_rot_cols(wq), QK_PAD).astype(BF16)
    wq = _pad_last(wq, QK_PAD).astype(BF16)
    wk = _pad_last(jnp.transpose(w_uk, (1, 0, 2)), QK_PAD).astype(BF16)
    wv = jnp.transpose(w_uv, (1, 2, 0)).astype(BF16)
    inv_freq = 1.0 / (ROPE_THETA ** (jnp.arange(half, dtype=F32) / half))
    freq = jnp.concatenate([jnp.zeros((MLA_NOPE,), F32), inv_freq, inv_freq,
                            jnp.zeros((LANES - MLA_QK,), F32)])[None, :]
    swap = lambda g: jnp.concatenate([jnp.zeros((MLA_NOPE,), F32), g[MLA_NOPE + half:MLA_QK],
                                      g[MLA_NOPE:MLA_NOPE + half], jnp.zeros((LANES - MLA_QK,), F32)])[None, :]
    gq = _pad_last(g_qn[None, :], QK_PAD)
    gk = _pad_last(g_kn[None, :], QK_PAD)
    gq_rot = swap(g_qn)
    gk_rot = swap(g_kn)

    full = lambda a: pl.BlockSpec(a.shape, lambda i: (0,) * a.ndim)
    head_map = lambda i: (i // nt, 0, i % nt, 0)
    args = (x2, pos2, g_mix[None, :], w_mla, w_rw, freq, g_cq[None, :], g_ckv[None, :],
            wq, wq_rot, wk, wv, gq, gq_rot, gk, gk_rot)
    in_specs = [pl.BlockSpec((tm, D_MODEL), lambda i: (i, 0)),
                pl.BlockSpec((tm, 1), lambda i: (i, 0))] + [full(a) for a in args[2:]]
    return pl.pallas_call(
        _inproj_kernel,
        grid=(t // tm,),
        in_specs=in_specs,
        out_specs=[pl.BlockSpec((tm, RWKV_IN), lambda i: (i, 0)),
                   pl.BlockSpec((1, MLA_HEADS, tm, QK_PAD), head_map),
                   pl.BlockSpec((1, MLA_HEADS, tm, QK_PAD), head_map),
                   pl.BlockSpec((1, MLA_HEADS, 1, VT_ROWS, tm),
                                lambda i: (i // nt, 0, (i % nt) // per_kv, 0, (i % nt) % per_kv))],
        out_shape=[jax.ShapeDtypeStruct((t, RWKV_IN), F32),
                   jax.ShapeDtypeStruct((bsz, MLA_HEADS, seq, QK_PAD), BF16),
                   jax.ShapeDtypeStruct((bsz, MLA_HEADS, seq, QK_PAD), BF16),
                   jax.ShapeDtypeStruct((bsz, MLA_HEADS, seq // kv_blk, VT_ROWS, kv_blk), BF16)],
        compiler_params=_params(("parallel",)),
        name="inproj",
    )(*args)


def _attn_kernel(q_ref, k_ref, vt_ref, o_ref, s_scr):
    qi = pl.program_id(2)
    nh = q_ref.shape[1]
    qb = q_ref.shape[2]
    kvb = vt_ref.shape[4]
    qs = [q_ref[0, h] for h in range(nh)]

    def scores_to(g, slot):
        start = pl.multiple_of(g * kvb, kvb)
        for h in range(nh):
            s_scr[slot, h] = lax.dot_general(k_ref[0, h, pl.ds(start, kvb), :], qs[h], (((1,), (1,)), ((), ())),
                                             preferred_element_type=F32)

    def consume(g, slot, carry, masked):
        ss = [s_scr[slot, h] for h in range(nh)]
        if masked:
            key = g * kvb + lax.broadcasted_iota(jnp.int32, (kvb, qb), 0)
            qry = qi * qb + lax.broadcasted_iota(jnp.int32, (kvb, qb), 1)
            ss = [jnp.where(key <= qry, s, -jnp.inf) for s in ss]
        ms = [jnp.maximum(carry[h][0], jnp.max(ss[h], axis=0, keepdims=True)) for h in range(nh)]
        ps = [jnp.exp2((ss[h] - ms[h]).astype(BF16)) for h in range(nh)]
        out = []
        for h in range(nh):
            m, acc = carry[h]
            alpha = jnp.exp2(m - ms[h])
            out.append((ms[h], alpha * acc + jnp.dot(vt_ref[0, h, g], ps[h], preferred_element_type=F32)))
        return tuple(out)

    init = tuple((jnp.full((1, qb), -1e30, F32), jnp.zeros((VT_ROWS, qb), F32)) for _ in range(nh))
    full_groups = (qi * qb) // kvb

    scores_to(0, 0)

    def two_groups(j, carry):
        scores_to(2 * j + 1, 1)
        carry = consume(2 * j, 0, carry, False)
        scores_to(2 * j + 2, 0)
        return consume(2 * j + 1, 1, carry, False)

    carry = lax.fori_loop(0, full_groups // 2, two_groups, init)

    def odd_tail(carry):
        scores_to(full_groups, 1)
        carry = consume(full_groups - 1, 0, carry, False)
        return consume(full_groups, 1, carry, True)

    def even_tail(carry):
        return consume(full_groups, 0, carry, True)

    carry = lax.cond(full_groups % 2 == 1, odd_tail, even_tail, carry)
    ot = jnp.concatenate([acc[:MLA_V] * (1.0 / acc[MLA_V:MLA_V + 1]) for _, acc in carry], axis=0)
    o_ref[0] = ot.T


def _attn(q, k, vt):
    bsz, nh, seq, _ = q.shape
    per_step = ATT_HEADS
    ngrp, kvb = vt.shape[2], vt.shape[4]
    qb = min(Q_BLK, kvb)
    return pl.pallas_call(
        _attn_kernel,
        grid=(bsz, nh // per_step, seq // qb),
        in_specs=[pl.BlockSpec((1, per_step, qb, QK_PAD), lambda b, h, i: (b, h, i, 0)),
                  pl.BlockSpec((1, per_step, seq, QK_PAD), lambda b, h, i: (b, h, 0, 0)),
                  pl.BlockSpec((1, per_step, ngrp, VT_ROWS, kvb), lambda b, h, i: (b, h, 0, 0, 0))],
        out_specs=pl.BlockSpec((1, qb, per_step * MLA_V), lambda b, h, i: (b, i, h)),
        out_shape=jax.ShapeDtypeStruct((bsz, seq, nh * MLA_V), F32),
        scratch_shapes=[pltpu.VMEM((2, per_step, kvb, qb), F32)],
        compiler_params=_params(("parallel", "parallel", "arbitrary")),
        name="attn",
    )(q, k, vt)


def _rwprep_kernel(feat_ref, prev_ref, mu_ref, w0_ref, w2_ref, a0_ref, a2_ref, g2_ref, kk_ref, ka_ref,
                   bd_ref, r_o, k_o, v_o, kk_o, lr_o, ld_o, g_o, *, nt):
    i = pl.program_id(0)
    feat = feat_ref[...]
    tm = feat.shape[0]
    first = jnp.where(i % nt == 0, 0.0, 1.0)
    prev_row = prev_ref[7:8, :] * first
    rolled = pltpu.roll(feat, 1, axis=0)
    rid = lax.broadcasted_iota(jnp.int32, (tm, 1), 0)
    prev = jnp.where(rid == 0, prev_row, rolled)
    feat = feat + (prev - feat) * mu_ref[...]

    r = feat[:, :RWKV_W]
    k = feat[:, RWKV_W:2 * RWKV_W]
    v = feat[:, 2 * RWKV_W:3 * RWKV_W]
    wa = feat[:, 3 * RWKV_W:3 * RWKV_W + LANES]
    gl = feat[:, 3 * RWKV_W + LANES:]
    lane = lax.broadcasted_iota(jnp.int32, (tm, LANES), 1)
    wl_in = jnp.where(lane < DECAY_LORA, jnp.tanh(wa), 0.0)
    al_in = jnp.where(lane < DECAY_LORA, 0.0, wa)
    wpre = w0_ref[...] + _dot(wl_in, w2_ref[...])
    w = -(jnp.maximum(-wpre, 0.0) + jnp.log(1.0 + jnp.exp(-jnp.abs(wpre)))) - 0.5
    logd = -jnp.exp(w)
    a = jax.nn.sigmoid(a0_ref[...] + _dot(al_in, a2_ref[...]))
    g = _dot(jax.nn.sigmoid(gl), g2_ref[...])
    kk = k * kk_ref[...]
    ss = _dot_split(kk * kk, bd_ref[...], split_lhs=True)
    r_o[...] = r
    k_o[...] = k * (1.0 + (a - 1.0) * ka_ref[...])
    v_o[...] = v
    kk_o[...] = kk * lax.rsqrt(ss + 1e-12)
    lr_o[...] = a
    ld_o[...] = logd
    g_o[...] = g


def _rwprep(feat, rw_mu, rw_w0, rw_w2, rw_a0, rw_a2, rw_g2, rw_k_k, rw_k_a, bsz, seq, tm):
    t = feat.shape[0]
    nt = seq // tm
    w2p = jnp.concatenate([rw_w2, jnp.zeros((LANES - DECAY_LORA, RWKV_W), F32)], axis=0).astype(BF16)
    a2p = jnp.concatenate([jnp.zeros((DECAY_LORA, RWKV_W), F32), rw_a2], axis=0).astype(BF16)
    args = (feat, feat, rw_mu[None, :], rw_w0[None, :], w2p, rw_a0[None, :], a2p, rw_g2.astype(BF16),
            rw_k_k[None, :], rw_k_a[None, :], _head_indicator(RWKV_W))
    full = lambda a: pl.BlockSpec(a.shape, lambda i: (0,) * a.ndim)
    rows8 = tm // 8
    in_specs = [pl.BlockSpec((tm, RWKV_IN), lambda i: (i, 0)),
                pl.BlockSpec((8, RWKV_IN), lambda i: (jnp.maximum(i * rows8 - 1, 0), 0))]
    in_specs += [full(a) for a in args[2:]]
    hshape = jax.ShapeDtypeStruct((t, RWKV_W), F32)
    hspec = pl.BlockSpec((tm, RWKV_W), lambda i: (i, 0))
    return pl.pallas_call(
        functools.partial(_rwprep_kernel, nt=nt),
        grid=(t // tm,),
        in_specs=in_specs,
        out_specs=[hspec] * 7,
        out_shape=[hshape] * 7,
        compiler_params=_params(("parallel",)),
        name="rwprep",
    )(*args)


def _rwscan_kernel(r_ref, k_ref, v_ref, kk_ref, lr_ref, ld_ref, g_ref, rk_ref, lng_ref, lnb_ref,
                   tri_ref, bd_ref, o_ref, state, *, chunk):
    c = pl.program_id(1)

    @pl.when(c == 0)
    def _():
        state[...] = jnp.zeros_like(state)

    C = chunk
    npairs = RWKV_W // LANES
    r = r_ref[0]
    k = k_ref[0]
    v = v_ref[0]
    kk = kk_ref[0]
    ld = ld_ref[0]
    bd = bd_ref[...]

    cum = _dot_split(tri_ref[...], ld, split_lhs=False)
    mid = C // 2 - 1
    cmid = cum[mid:mid + 1, :]
    cend = cum[C - 1:C, :]
    e_pos = jnp.exp(cum - cmid)
    e_prev = jnp.exp(cum - ld - cmid)
    e_neg = jnp.exp(cmid - cum)
    g_mid = jnp.exp(cmid)
    g_end_rel = jnp.exp(cend - cmid)
    g_end = jnp.exp(cend)
    rt = r * e_pos
    at = -kk * e_prev
    bt = kk * lr_ref[0] * e_neg
    kt = k * e_neg
    at_g = (at * g_mid).astype(BF16)
    rt_g = (rt * g_mid).astype(BF16)
    art = jnp.concatenate([at, rt], axis=0)
    bk = jnp.concatenate([bt, kt], axis=0).astype(BF16)
    bkg = jnp.concatenate([bt * g_end_rel, kt * g_end_rel], axis=0).astype(BF16)
    vb = v.astype(BF16)

    row = lax.broadcasted_iota(jnp.int32, (C, C), 0)
    col = lax.broadcasted_iota(jnp.int32, (C, C), 1)
    strict = row > col
    eye = (row == col).astype(F32)
    row2 = lax.broadcasted_iota(jnp.int32, (C, 2 * C), 0)
    col2 = lax.broadcasted_iota(jnp.int32, (C, 2 * C), 1)
    incl2 = row2 >= jnp.where(col2 >= C, col2 - C, col2)
    low2 = lax.broadcasted_iota(jnp.int32, (2 * C, LANES), 1) < RWKV_N
    low = lax.broadcasted_iota(jnp.int32, (C, LANES), 1) < RWKV_N
    srow = lax.broadcasted_iota(jnp.int32, (LANES, LANES), 0)
    scol = lax.broadcasted_iota(jnp.int32, (LANES, LANES), 1)
    same_head = (srow < RWKV_N) == (scol < RWKV_N)
    diag = srow == scol
    pair = lambda a, p: a[:, p * LANES:(p + 1) * LANES]

    a_ab, a_ak, a_rbk = [], [], []
    for p in range(npairs):
        art_p = pair(art, p)
        bk_p = pair(bk, p)
        for half in range(2):
            lhs = jnp.where(low2, art_p, 0.0) if half == 0 else jnp.where(low2, 0.0, art_p)
            big = _dot_nt(lhs, bk_p)
            a_ab.append(jnp.where(strict, big[:C, :C], 0.0))
            a_ak.append(jnp.where(strict, big[:C, C:], 0.0).astype(BF16))
            a_rbk.append(jnp.where(incl2, big[C:, :], 0.0).astype(BF16))

    ps = [a.astype(BF16) for a in a_ab]
    tinv = [eye + a for a in a_ab]
    n = 2
    while n < C:
        sq = [jnp.dot(pb, pb, preferred_element_type=F32) for pb in ps]
        ps = [s.astype(BF16) for s in sq]
        tinv = [t + jnp.dot(t.astype(BF16), pb, preferred_element_type=F32) for t, pb in zip(tinv, ps)]
        n *= 2
    tinv = [t.astype(BF16) for t in tinv]

    s0 = [state[p] for p in range(npairs)]
    s0b = [s.astype(BF16) for s in s0]
    sel = lambda h0, h1: jnp.where(low, h0, h1)
    mm = lambda a, b: jnp.dot(a, b, preferred_element_type=F32)
    xs = [mm(pair(at_g, p), s0b[p]) + sel(mm(a_ak[2 * p], pair(vb, p)), mm(a_ak[2 * p + 1], pair(vb, p)))
          for p in range(npairs)]
    xb = [x.astype(BF16) for x in xs]
    us = [sel(mm(tinv[2 * p], xb[p]), mm(tinv[2 * p + 1], xb[p])) for p in range(npairs)]
    uv = [jnp.concatenate([us[p].astype(BF16), pair(vb, p)], axis=0) for p in range(npairs)]
    ys = [mm(pair(rt_g, p), s0b[p]) + sel(mm(a_rbk[2 * p], uv[p]), mm(a_rbk[2 * p + 1], uv[p]))
          for p in range(npairs)]
    for p in range(npairs):
        upd = lax.dot_general(pair(bkg, p), uv[p], (((0,), (0,)), ((), ())), preferred_element_type=F32)
        g_col = jnp.sum(jnp.where(diag, pair(g_end, p), 0.0), axis=-1, keepdims=True)
        state[p] = s0[p] * g_col + jnp.where(same_head, upd, 0.0)

    inv_n = 1.0 / RWKV_N
    for p in range(npairs):
        sl = slice(p * LANES, (p + 1) * LANES)
        y = ys[p]
        mean = _dot_split(y, bd, split_lhs=True) * inv_n
        yc = y - mean
        var = _dot_split(yc * yc, bd, split_lhs=True) * inv_n
        yn = yc * lax.rsqrt(var + LNX_EPS) * lng_ref[:, sl] + lnb_ref[:, sl]
        bonus = _dot_split(pair(r, p) * pair(k, p) * rk_ref[:, sl], bd, split_lhs=True) * pair(v, p)
        o_ref[0, :, sl] = (yn + bonus) * g_ref[0, :, sl]


def _rwscan(r, k, v, kk, lr, ld, g, rw_r_k, rw_lnx_g, rw_lnx_b, bsz, seq, chunk):
    width = r.shape[-1]
    tri = jnp.asarray(np.tril(np.ones((chunk, chunk), np.float32))).astype(BF16)
    bd = _head_indicator(LANES)
    seq3 = lambda a: a.reshape(bsz, seq, width)
    cspec = pl.BlockSpec((1, chunk, width), lambda b, c: (b, c, 0))
    small = pl.BlockSpec((1, width), lambda b, c: (0, 0))
    return pl.pallas_call(
        functools.partial(_rwscan_kernel, chunk=chunk),
        grid=(bsz, seq // chunk),
        in_specs=[cspec] * 7 + [small] * 3 + [pl.BlockSpec((chunk, chunk), lambda b, c: (0, 0)),
                                              pl.BlockSpec((LANES, LANES), lambda b, c: (0, 0))],
        out_specs=cspec,
        out_shape=jax.ShapeDtypeStruct((bsz, seq, width), F32),
        scratch_shapes=[pltpu.VMEM((width // LANES, LANES, LANES), F32)],
        compiler_params=_params(("parallel", "arbitrary")),
        name="rwscan",
    )(seq3(r), seq3(k), seq3(v), seq3(kk), seq3(lr), seq3(ld), seq3(g), rw_r_k.reshape(1, width),
      rw_lnx_g[None, :], rw_lnx_b[None, :], tri, bd)


def _outproj_kernel(x_ref, mla_ref, rw_ref, gmo_ref, wo_ref, gffn_ref, wr_ref, br_ref,
                    x1_ref, comb_ref):
    tm = x_ref.shape[0]
    m = mla_ref[...]
    mn = m * lax.rsqrt(jnp.mean(m * m, axis=-1, keepdims=True) + NORM_EPS) * gmo_ref[...]
    mix = jnp.concatenate([mn.astype(BF16), rw_ref[...].astype(BF16)], axis=-1)
    acc = x_ref[...] + jnp.dot(mix, wo_ref[...], preferred_element_type=F32)
    x1_ref[...] = acc
    xn = acc * lax.rsqrt(jnp.mean(acc * acc, axis=-1, keepdims=True) + NORM_EPS) * gffn_ref[...]

    logits = _dot_bf16x3(xn, wr_ref[...]) + br_ref[...]
    lane = lax.broadcasted_iota(jnp.int32, (tm, LANES), 1)
    neg = -jnp.inf
    big = jnp.int32(1 << 20)
    is_g = (lane >= N_EXPERTS) & (lane < N_EXPERTS + N_GROUPS)
    glog = jnp.where(is_g, logits, neg)
    gmax = jnp.max(glog, axis=-1, keepdims=True)
    gsel = jnp.min(jnp.where(glog == gmax, lane, big), axis=-1, keepdims=True) - N_EXPERTS
    gprob = 1.0 / jnp.sum(jnp.exp(glog - gmax), axis=-1, keepdims=True)
    lo = gsel * EXPERTS_PER_GROUP
    in_group = (lane >= lo) & (lane < lo + EXPERTS_PER_GROUP)
    elog = jnp.where(in_group, logits, neg)
    v1 = jnp.max(elog, axis=-1, keepdims=True)
    i1 = jnp.min(jnp.where(elog == v1, lane, big), axis=-1, keepdims=True)
    elog2 = jnp.where(lane == i1, neg, elog)
    v2 = jnp.max(elog2, axis=-1, keepdims=True)
    i2 = jnp.min(jnp.where(elog2 == v2, lane, big), axis=-1, keepdims=True)
    e2 = jnp.exp(v2 - v1)
    w1 = gprob / (1.0 + e2)
    w2 = gprob * e2 / (1.0 + e2)
    comb_ref[...] = jnp.where(lane == 0, i1.astype(F32), jnp.where(lane == 1, i2.astype(F32),
                              jnp.where(lane == 2, w1, jnp.where(lane == 3, w2, 0.0))))


def _outproj(x2, mla, rw, g_mla_out, w_o, g_ffn, w_group, b_group, w_expert, b_expert, seq, tm):
    t = x2.shape[0]
    wo = w_o.astype(BF16)
    pad = LANES - N_EXPERTS - N_GROUPS
    wr = jnp.concatenate([w_expert, w_group, jnp.zeros((D_MODEL, pad), F32)], axis=-1)
    br = jnp.concatenate([b_expert, b_group, jnp.zeros((pad,), F32)])[None, :]
    gmo = g_mla_out[None, :]
    full = lambda a: pl.BlockSpec(a.shape, lambda i: (0,) * a.ndim)
    row = pl.BlockSpec((tm, D_MODEL), lambda i: (i, 0))
    return pl.pallas_call(
        _outproj_kernel,
        grid=(t // tm,),
        in_specs=[row,
                  pl.BlockSpec((tm, MLA_W), lambda i: (i, 0)),
                  pl.BlockSpec((tm, RWKV_W), lambda i: (i, 0)),
                  full(gmo), full(wo), pl.BlockSpec((1, D_MODEL), lambda i: (0, 0)), full(wr), full(br)],
        out_specs=[row, pl.BlockSpec((tm, LANES), lambda i: (i, 0))],
        out_shape=[jax.ShapeDtypeStruct((t, D_MODEL), F32),
                   jax.ShapeDtypeStruct((t, LANES), F32)],
        compiler_params=_params(("parallel",)),
        name="outproj",
    )(x2, mla, rw, gmo, wo, g_ffn[None, :], wr, br)


def _moeplan_kernel(route_ref, tri_ref, triu_ref, posw_ref, nb_ref, roff_ref):
    rt = route_ref[...]
    tb = rt.shape[0]
    lane = lax.broadcasted_iota(jnp.int32, (tb, LANES), 1)
    oh1 = lane == rt[:, 0:1].astype(jnp.int32)
    oh2 = lane == rt[:, 1:2].astype(jnp.int32)
    oh = jnp.where(oh1, 1.0, jnp.where(oh2, 1.0, 0.0))
    excl = jnp.dot(tri_ref[...], oh.astype(BF16), preferred_element_type=F32)
    cnt = jnp.sum(oh, axis=0, keepdims=True)
    nb = jnp.floor((cnt + (MOE_RB - 0.5)) * (1.0 / MOE_RB))
    boff = jnp.dot(jnp.broadcast_to(nb, (8, LANES)).astype(BF16), triu_ref[...],
                   preferred_element_type=F32)[0:1]
    roff = boff * MOE_RB
    base = roff + excl
    pos1 = jnp.sum(jnp.where(oh1, base, 0.0), axis=-1, keepdims=True)
    pos2 = jnp.sum(jnp.where(oh2, base, 0.0), axis=-1, keepdims=True)
    posw_ref[...] = jnp.where(lane == 0, pos1, jnp.where(lane == 1, pos2, rt))
    nb_ref[0] = nb.astype(jnp.int32)
    roff_ref[0] = roff.astype(jnp.int32)


def _moeplan(route, tb):
    t = route.shape[0]
    nt = t // tb
    tri = jnp.asarray(np.tril(np.ones((tb, tb), np.float32), -1)).astype(BF16)
    triu = jnp.asarray(np.triu(np.ones((LANES, LANES), np.float32), 1)).astype(BF16)
    small = pl.BlockSpec((1, 1, LANES), lambda i: (i, 0, 0))
    return pl.pallas_call(
        _moeplan_kernel,
        grid=(nt,),
        in_specs=[pl.BlockSpec((tb, LANES), lambda i: (i, 0)),
                  pl.BlockSpec((tb, tb), lambda i: (0, 0)),
                  pl.BlockSpec((LANES, LANES), lambda i: (0, 0))],
        out_specs=[pl.BlockSpec((tb, LANES), lambda i: (i, 0)), small, small],
        out_shape=[jax.ShapeDtypeStruct((t, LANES), F32),
                   jax.ShapeDtypeStruct((nt, 1, LANES), jnp.int32),
                   jax.ShapeDtypeStruct((nt, 1, LANES), jnp.int32)],
        compiler_params=_params(("parallel",)),
        name="moeplan",
    )(route, tri, triu)


def _moe_kernel(pos_ref, nb_ref, roff_ref, x1_ref, posw_ref, gffn_ref, wg_hbm, wu_hbm, wd_hbm, o_ref,
                xn, xs, g1, g2, wg_buf, wu_buf, wd_buf, sem):
    i = pl.program_id(0)
    nt = pl.num_programs(0)
    tb = x1_ref.shape[0]
    chunk = g1.shape[0]
    dummy_r0 = xs.shape[0] - MOE_RB
    group = MOE_EXPERTS_PER_STEP
    ngroups = N_EXPERTS // group

    def weight_copies(e, slot):
        return (pltpu.make_async_copy(wg_hbm.at[e], wg_buf.at[slot], sem.at[slot, 0]),
                pltpu.make_async_copy(wu_hbm.at[e], wu_buf.at[slot], sem.at[slot, 1]),
                pltpu.make_async_copy(wd_hbm.at[e], wd_buf.at[slot], sem.at[slot, 2]))

    @pl.when(i == 0)
    def _():
        for j in range(group):
            for c in weight_copies(j, j):
                c.start()
        xs[...] = jnp.zeros_like(xs)

    for c in range(tb // chunk):
        sl = slice(c * chunk, (c + 1) * chunk)
        x1 = x1_ref[sl, :]
        xn[sl, :] = x1 * lax.rsqrt(jnp.mean(x1 * x1, axis=-1, keepdims=True) + NORM_EPS) * gffn_ref[...]

    def put(tok, carry):
        row = xn[pl.ds(tok, 1), :]
        xs[pl.ds(pos_ref[2 * tok], 1), :] = row
        xs[pl.ds(pos_ref[2 * tok + 1], 1), :] = row
        return carry

    lax.fori_loop(0, tb, put, 0, unroll=8)

    def expert_group(p, carry):
        base = lax.rem(p, 2) * group
        nxt = p + 1

        @pl.when(jnp.logical_or(nxt < ngroups, i + 1 < nt))
        def _():
            first = jnp.where(nxt < ngroups, nxt, 0) * group
            for j in range(group):
                for c in weight_copies(first + j, group - base + j):
                    c.start()

        for j in range(group):
            for c in weight_copies(p * group + j, base + j):
                c.wait()
        nbs = [nb_ref[0, 0, p * group + j] for j in range(group)]
        roffs = [roff_ref[0, 0, p * group + j] for j in range(group)]
        trips = functools.reduce(jnp.maximum, nbs)

        def block(b, c2):
            r0s = [pl.multiple_of(jnp.where(b < nbs[j], roffs[j] + b * MOE_RB, dummy_r0), 8)
                   for j in range(group)]
            x = [xs[pl.ds(r0, MOE_RB), :].astype(BF16) for r0 in r0s]
            hg = [jnp.dot(x[j], wg_buf[base + j], preferred_element_type=F32) for j in range(group)]
            hu = [jnp.dot(x[j], wu_buf[base + j], preferred_element_type=F32) for j in range(group)]
            hid = [(hg[j] * jax.nn.sigmoid(hg[j]) * hu[j]).astype(BF16) for j in range(group)]
            y = [jnp.dot(hid[j], wd_buf[base + j], preferred_element_type=F32) for j in range(group)]
            for j in range(group):
                xs[pl.ds(r0s[j], MOE_RB), :] = y[j]
            return c2

        lax.fori_loop(0, trips, block, 0)
        return carry

    lax.fori_loop(0, ngroups, expert_group, 0)

    for c in range(tb // chunk):
        def take(j, carry, c=c):
            tok = c * chunk + j
            g1[pl.ds(j, 1), :] = xs[pl.ds(pos_ref[2 * tok], 1), :]
            g2[pl.ds(j, 1), :] = xs[pl.ds(pos_ref[2 * tok + 1], 1), :]
            return carry

        lax.fori_loop(0, chunk, take, 0, unroll=8)
        sl = slice(c * chunk, (c + 1) * chunk)
        o_ref[sl, :] = x1_ref[sl, :] + (posw_ref[sl, 2:3] * g1[...] + posw_ref[sl, 3:4] * g2[...])


def _moe(x1, route, g_ffn, w_gate, w_up, w_down, tb):
    t = x1.shape[0]
    nt = t // tb
    chunk = min(tb, 256)
    rows = 2 * tb + N_EXPERTS * MOE_RB + MOE_RB
    nslot = 2 * MOE_EXPERTS_PER_STEP
    posw, nb, roff = _moeplan(route, tb)
    pos = posw[:, 0:2].astype(jnp.int32).reshape(2 * t)
    smem3 = lambda: pl.BlockSpec((1, 1, LANES), lambda i: (i, 0, 0), memory_space=pltpu.SMEM)
    hbm = lambda: pl.BlockSpec(memory_space=pl.ANY)
    return pl.pallas_call(
        _moe_kernel,
        grid=(nt,),
        in_specs=[pl.BlockSpec((2 * tb,), lambda i: (i,), memory_space=pltpu.SMEM), smem3(), smem3(),
                  pl.BlockSpec((tb, D_MODEL), lambda i: (i, 0)),
                  pl.BlockSpec((tb, LANES), lambda i: (i, 0)),
                  pl.BlockSpec((1, D_MODEL), lambda i: (0, 0)),
                  hbm(), hbm(), hbm()],
        out_specs=pl.BlockSpec((tb, D_MODEL), lambda i: (i, 0)),
        out_shape=jax.ShapeDtypeStruct((t, D_MODEL), F32),
        scratch_shapes=[pltpu.VMEM((tb, D_MODEL), F32), pltpu.VMEM((rows, D_MODEL), F32),
                        pltpu.VMEM((chunk, D_MODEL), F32), pltpu.VMEM((chunk, D_MODEL), F32),
                        pltpu.VMEM((nslot, D_MODEL, D_EXPERT), BF16), pltpu.VMEM((nslot, D_MODEL, D_EXPERT), BF16),
                        pltpu.VMEM((nslot, D_EXPERT, D_MODEL), BF16), pltpu.SemaphoreType.DMA((nslot, 3))],
        compiler_params=_params(("arbitrary",)),
        name="moe",
    )(pos, nb, roff, x1, posw, g_ffn[None, :], w_gate.astype(BF16), w_up.astype(BF16), w_down.astype(BF16))


def _tile(seq, want):
    tm = min(seq, want)
    assert seq % tm == 0 and tm % 8 == 0
    return tm


def _layer(x, positions, g_mix, w_in, g_cq, w_uq, g_ckv, w_uk, w_uv, g_qn, g_kn, g_mla_out,
           rw_mu, rw_w0, rw_w2, rw_a0, rw_a2, rw_g2, rw_k_k, rw_k_a, rw_r_k, rw_lnx_g, rw_lnx_b,
           w_o, g_ffn, w_group, b_group, w_expert, b_expert, w_gate, w_up, w_down):
    bsz, seq, d = x.shape
    t = bsz * seq
    x2 = x.reshape(t, d)
    pos2 = positions.reshape(t, 1)
    kv_blk = _tile(seq, KV_BLK)
    feat, q, k, v = _inproj(x2, pos2, g_mix, w_in, g_cq, w_uq, g_ckv, w_uk, w_uv, g_qn, g_kn,
                            bsz, seq, _tile(kv_blk, 512), kv_blk)
    mla = _attn(q, k, v).reshape(t, MLA_W)
    r, k2, vv, kk, lr, ld, g = _rwprep(feat, rw_mu, rw_w0, rw_w2, rw_a0, rw_a2, rw_g2, rw_k_k, rw_k_a,
                                       bsz, seq, _tile(seq, 512))
    rw = _rwscan(r, k2, vv, kk, lr, ld, g, rw_r_k, rw_lnx_g, rw_lnx_b, bsz, seq, _tile(seq, 128))
    rw = rw.reshape(t, RWKV_W)
    x1, route = _outproj(x2, mla, rw, g_mla_out, w_o, g_ffn, w_group, b_group, w_expert, b_expert,
                         seq, _tile(seq, 512))
    out = _moe(x1, route, g_ffn, w_gate, w_up, w_down, _tile(t, MOE_TB))
    return out.reshape(bsz, seq, d)


def kernel(x, positions, g_mix, w_in, g_cq, w_uq, g_ckv, w_uk, w_uv, g_qn, g_kn, g_mla_out, rw_mu, rw_w0, rw_w2, rw_a0, rw_a2, rw_g2, rw_k_k, rw_k_a, rw_r_k, rw_lnx_g, rw_lnx_b, w_o, g_ffn, w_group, b_group, w_expert, b_expert, w_gate, w_up, w_down):
    for l in range(g_mix.shape[0]):
        x = _layer(x, positions, g_mix[l], w_in[l], g_cq[l], w_uq[l], g_ckv[l], w_uk[l], w_uv[l],
                   g_qn[l], g_kn[l], g_mla_out[l], rw_mu[l], rw_w0[l], rw_w2[l], rw_a0[l], rw_a2[l],
                   rw_g2[l], rw_k_k[l], rw_k_a[l], rw_r_k[l], rw_lnx_g[l], rw_lnx_b[l], w_o[l],
                   g_ffn[l], w_group[l], b_group[l], w_expert[l], b_expert[l], w_gate[l], w_up[l],
                   w_down[l])
    return x
```

```python
import functools

import numpy as np
import jax
import jax.numpy as jnp
from jax import lax
from jax.experimental import pallas as pl
from jax.experimental.pallas import tpu as pltpu

F32 = jnp.float32
BF16 = jnp.bfloat16

D_MODEL = 1024
MLA_HEADS = 8
MLA_NOPE = 64
MLA_ROPE = 32
MLA_QK = MLA_NOPE + MLA_ROPE
MLA_V = 64
MLA_W = MLA_HEADS * MLA_V
MLA_Q_RANK = 256
MLA_KV_RANK = 128
ROPE_THETA = 10000.0
RWKV_HEADS = 8
RWKV_N = 64
RWKV_W = RWKV_HEADS * RWKV_N
DECAY_LORA = 64
AAA_LORA = 64
GATE_LORA = 128
MLA_IN = MLA_Q_RANK + MLA_KV_RANK + MLA_ROPE
RWKV_IN = 3 * RWKV_W + DECAY_LORA + AAA_LORA + GATE_LORA
N_GROUPS = 4
EXPERTS_PER_GROUP = 8
N_EXPERTS = N_GROUPS * EXPERTS_PER_GROUP
D_EXPERT = 256
NORM_EPS = 1e-6
LNX_EPS = 64e-5

LANES = 128
QK_PAD = LANES
Q_BLK = 512
KV_BLK = 512
ATT_HEADS = 4
MOE_TB = 1024
MOE_RB = 96
MOE_EXPERTS_PER_STEP = 2
MOE_WEIGHT_BUFS = 3
VT_ROWS = MLA_V + 16
Q_SCALE = MLA_QK ** -0.5 * float(np.log2(np.e))
PM_W = MLA_Q_RANK + MLA_KV_RANK + 2 * LANES
VMEM_LIMIT = 56 * 1024 * 1024


def _dot_bf16x3(a, b):
    a_hi = a.astype(BF16)
    a_lo = (a - a_hi.astype(F32)).astype(BF16)
    b_hi = b.astype(BF16)
    b_lo = (b - b_hi.astype(F32)).astype(BF16)
    d = lambda p, q: jnp.dot(p, q, preferred_element_type=F32)
    return d(a_hi, b_hi) + (d(a_hi, b_lo) + d(a_lo, b_hi))


def _dot(a, b):
    return jnp.dot(a.astype(BF16), b.astype(BF16), preferred_element_type=F32)


def _dot_nt(a, b):
    return lax.dot_general(a.astype(BF16), b.astype(BF16), (((1,), (1,)), ((), ())),
                           preferred_element_type=F32)


def _dot_tn(a, b):
    return lax.dot_general(a.astype(BF16), b.astype(BF16), (((0,), (0,)), ((), ())),
                           preferred_element_type=F32)


def _dot_split(a, b, split_lhs, passes=2):
    s = a if split_lhs else b
    out = None
    for _ in range(passes):
        piece = s.astype(BF16)
        term = (jnp.dot(piece, b, preferred_element_type=F32) if split_lhs
                else jnp.dot(a, piece, preferred_element_type=F32))
        out = term if out is None else out + term
        s = s - piece.astype(F32)
    return out


def _head_indicator(width):
    hid = np.arange(width) // RWKV_N
    return jnp.asarray((hid[:, None] == hid[None, :]).astype(np.float32)).astype(BF16)


def _params(sem):
    return pltpu.CompilerParams(dimension_semantics=sem, vmem_limit_bytes=VMEM_LIMIT)


def _inproj_kernel(x_ref, pos_ref, gmix_ref, wmla_ref, wrw_ref, freq_ref, gcq_ref, gckv_ref,
                   wq_ref, wqr_ref, wk_ref, wv_ref, gq_ref, gqr_ref, gk_ref, gkr_ref,
                   feat_ref, q_ref, k_ref, v_ref):
    x = x_ref[...]
    h = x * lax.rsqrt(jnp.mean(x * x, axis=-1, keepdims=True) + NORM_EPS) * gmix_ref[...]
    hb = h.astype(BF16)
    feat_ref[...] = jnp.dot(hb, wrw_ref[...], preferred_element_type=F32)
    pm = jnp.dot(hb, wmla_ref[...], preferred_element_type=F32)

    c_q = pm[:, :MLA_Q_RANK]
    c_kv = pm[:, MLA_Q_RANK:MLA_Q_RANK + MLA_KV_RANK]
    kr = pm[:, MLA_Q_RANK + MLA_KV_RANK:MLA_Q_RANK + MLA_KV_RANK + LANES]
    kr_rot = pm[:, MLA_Q_RANK + MLA_KV_RANK + LANES:PM_W]
    cqn = (c_q * lax.rsqrt(jnp.mean(c_q * c_q, axis=-1, keepdims=True) + NORM_EPS)
           * gcq_ref[...]).astype(BF16)
    ckvn = (c_kv * lax.rsqrt(jnp.mean(c_kv * c_kv, axis=-1, keepdims=True) + NORM_EPS)
            * gckv_ref[...]).astype(BF16)

    ang = pos_ref[...].astype(F32) * freq_ref[...]
    cosf = jnp.cos(ang)
    sinf = jnp.sin(ang)
    q_cos = gq_ref[...] * cosf * Q_SCALE
    q_sin = gqr_ref[...] * sinf * Q_SCALE
    k_cos = gk_ref[...] * cosf
    k_sin = kr_rot * (gkr_ref[...] * sinf)
    inv_qk = 1.0 / MLA_QK
    ones_rows = jnp.where(lax.broadcasted_iota(jnp.int32, (VT_ROWS - MLA_V, x.shape[0]), 0) == 0, 1.0, 0.0)

    for hh in range(MLA_HEADS):
        q_raw = jnp.dot(cqn, wq_ref[hh], preferred_element_type=F32)
        q_rot = jnp.dot(cqn, wqr_ref[hh], preferred_element_type=F32)
        sq = lax.rsqrt(jnp.sum(q_raw * q_raw, axis=-1, keepdims=True) * inv_qk + NORM_EPS)
        q_ref[0, hh] = (sq * (q_raw * q_cos + q_rot * q_sin)).astype(BF16)
        k_raw = jnp.dot(ckvn, wk_ref[hh], preferred_element_type=F32) + kr
        sk = lax.rsqrt(jnp.sum(k_raw * k_raw, axis=-1, keepdims=True) * inv_qk + NORM_EPS)
        k_ref[0, hh] = (sk * (k_raw * k_cos + k_sin)).astype(BF16)
        vt = lax.dot_general(wv_ref[hh], ckvn, (((1,), (1,)), ((), ())), preferred_element_type=F32)
        v_ref[0, hh, 0] = jnp.concatenate([vt, ones_rows], axis=0).astype(BF16)


def _rot_cols(w):
    half = MLA_ROPE // 2
    z = jnp.zeros_like(w[..., :MLA_NOPE])
    return jnp.concatenate([z, -w[..., MLA_NOPE + half:MLA_QK], w[..., MLA_NOPE:MLA_NOPE + half]], axis=-1)


def _pad_last(w, n):
    return jnp.pad(w, [(0, 0)] * (w.ndim - 1) + [(0, n - w.shape[-1])])


def _inproj(x2, pos2, g_mix, w_in, g_cq, w_uq, g_ckv, w_uk, w_uv, g_qn, g_kn, bsz, seq, tm, kv_blk):
    t = x2.shape[0]
    nt = seq // tm
    per_kv = kv_blk // tm
    half = MLA_ROPE // 2
    w_kr = w_in[:, MLA_Q_RANK + MLA_KV_RANK:MLA_IN]
    zeros64 = jnp.zeros((D_MODEL, MLA_NOPE), F32)
    zeros32 = jnp.zeros((D_MODEL, LANES - MLA_QK), F32)
    w_kr_rot = jnp.concatenate([-w_kr[:, half:], w_kr[:, :half]], axis=-1)
    w_mla = jnp.concatenate([w_in[:, :MLA_Q_RANK + MLA_KV_RANK], zeros64, w_kr, zeros32,
                             zeros64, w_kr_rot, zeros32], axis=-1).astype(BF16)
    w_rw = w_in[:, MLA_IN:].astype(BF16)
    wq = jnp.transpose(w_uq, (1, 0, 2))
    wq_rot = _pad_last(_rot_cols(wq), QK_PAD).astype(BF16)
    wq = _pad_last(wq, QK_PAD).astype(BF16)
    wk = _pad_last(jnp.transpose(w_uk, (1, 0, 2)), QK_PAD).astype(BF16)
    wv = jnp.transpose(w_uv, (1, 2, 0)).astype(BF16)
    inv_freq = 1.0 / (ROPE_THETA ** (jnp.arange(half, dtype=F32) / half))
    freq = jnp.concatenate([jnp.zeros((MLA_NOPE,), F32), inv_freq, inv_freq,
                            jnp.zeros((LANES - MLA_QK,), F32)])[None, :]
    swap = lambda g: jnp.concatenate([jnp.zeros((MLA_NOPE,), F32), g[MLA_NOPE + half:MLA_QK],
                                      g[MLA_NOPE:MLA_NOPE + half], jnp.zeros((LANES - MLA_QK,), F32)])[None, :]
    gq = _pad_last(g_qn[None, :], QK_PAD)
    gk = _pad_last(g_kn[None, :], QK_PAD)
    gq_rot = swap(g_qn)
    gk_rot = swap(g_kn)

    full = lambda a: pl.BlockSpec(a.shape, lambda i: (0,) * a.ndim)
    head_map = lambda i: (i // nt, 0, i % nt, 0)
    args = (x2, pos2, g_mix[None, :], w_mla, w_rw, freq, g_cq[None, :], g_ckv[None, :],
            wq, wq_rot, wk, wv, gq, gq_rot, gk, gk_rot)
    in_specs = [pl.BlockSpec((tm, D_MODEL), lambda i: (i, 0)),
                pl.BlockSpec((tm, 1), lambda i: (i, 0))] + [full(a) for a in args[2:]]
    return pl.pallas_call(
        _inproj_kernel,
        grid=(t // tm,),
        in_specs=in_specs,
        out_specs=[pl.BlockSpec((tm, RWKV_IN), lambda i: (i, 0)),
                   pl.BlockSpec((1, MLA_HEADS, tm, QK_PAD), head_map),
                   pl.BlockSpec((1, MLA_HEADS, tm, QK_PAD), head_map),
                   pl.BlockSpec((1, MLA_HEADS, 1, VT_ROWS, tm),
                                lambda i: (i // nt, 0, (i % nt) // per_kv, 0, (i % nt) % per_kv))],
        out_shape=[jax.ShapeDtypeStruct((t, RWKV_IN), F32),
                   jax.ShapeDtypeStruct((bsz, MLA_HEADS, seq, QK_PAD), BF16),
                   jax.ShapeDtypeStruct((bsz, MLA_HEADS, seq, QK_PAD), BF16),
                   jax.ShapeDtypeStruct((bsz, MLA_HEADS, seq // kv_blk, VT_ROWS, kv_blk), BF16)],
        compiler_params=_params(("parallel",)),
        name="inproj",
    )(*args)


def _attn_kernel(q_ref, k_ref, vt_ref, o_ref, s_scr):
    qi = pl.program_id(2)
    nh = q_ref.shape[1]
    qb = q_ref.shape[2]
    kvb = vt_ref.shape[4]
    qs = [q_ref[0, h] for h in range(nh)]

    def scores_to(g, slot):
        start = pl.multiple_of(g * kvb, kvb)
        for h in range(nh):
            s_scr[slot, h] = lax.dot_general(k_ref[0, h, pl.ds(start, kvb), :], qs[h], (((1,), (1,)), ((), ())),
                                             preferred_element_type=F32)

    def consume(g, slot, carry, masked):
        ss = [s_scr[slot, h] for h in range(nh)]
        if masked:
            key = g * kvb + lax.broadcasted_iota(jnp.int32, (kvb, qb), 0)
            qry = qi * qb + lax.broadcasted_iota(jnp.int32, (kvb, qb), 1)
            ss = [jnp.where(key <= qry, s, -jnp.inf) for s in ss]
        ms = [jnp.maximum(carry[h][0], jnp.max(ss[h], axis=0, keepdims=True)) for h in range(nh)]
        ps = [jnp.exp2((ss[h] - ms[h]).astype(BF16)) for h in range(nh)]
        out = []
        for h in range(nh):
            m, acc = carry[h]
            alpha = jnp.exp2(m - ms[h])
            out.append((ms[h], alpha * acc + jnp.dot(vt_ref[0, h, g], ps[h], preferred_element_type=F32)))
        return tuple(out)

    init = tuple((jnp.full((1, qb), -1e30, F32), jnp.zeros((VT_ROWS, qb), F32)) for _ in range(nh))
    full_groups = (qi * qb) // kvb

    scores_to(0, 0)

    def two_groups(j, carry):
        scores_to(2 * j + 1, 1)
        carry = consume(2 * j, 0, carry, False)
        scores_to(2 * j + 2, 0)
        return consume(2 * j + 1, 1, carry, False)

    carry = lax.fori_loop(0, full_groups // 2, two_groups, init)

    def odd_tail(carry):
        scores_to(full_groups, 1)
        carry = consume(full_groups - 1, 0, carry, False)
        return consume(full_groups, 1, carry, True)

    def even_tail(carry):
        return consume(full_groups, 0, carry, True)

    carry = lax.cond(full_groups % 2 == 1, odd_tail, even_tail, carry)
    ot = jnp.concatenate([acc[:MLA_V] * (1.0 / acc[MLA_V:MLA_V + 1]) for _, acc in carry], axis=0)
    o_ref[0] = ot.T


def _attn(q, k, vt):
    bsz, nh, seq, _ = q.shape
    per_step = ATT_HEADS
    ngrp, kvb = vt.shape[2], vt.shape[4]
    qb = min(Q_BLK, kvb)
    return pl.pallas_call(
        _attn_kernel,
        grid=(bsz, nh // per_step, seq // qb),
        in_specs=[pl.BlockSpec((1, per_step, qb, QK_PAD), lambda b, h, i: (b, h, i, 0)),
                  pl.BlockSpec((1, per_step, seq, QK_PAD), lambda b, h, i: (b, h, 0, 0)),
                  pl.BlockSpec((1, per_step, ngrp, VT_ROWS, kvb), lambda b, h, i: (b, h, 0, 0, 0))],
        out_specs=pl.BlockSpec((1, qb, per_step * MLA_V), lambda b, h, i: (b, i, h)),
        out_shape=jax.ShapeDtypeStruct((bsz, seq, nh * MLA_V), F32),
        scratch_shapes=[pltpu.VMEM((2, per_step, kvb, qb), F32)],
        compiler_params=_params(("parallel", "parallel", "arbitrary")),
        name="attn",
    )(q, k, vt)


def _rwprep_kernel(feat_ref, prev_ref, mu_ref, w0_ref, w2_ref, a0_ref, a2_ref, g2_ref, kk_ref, ka_ref,
                   bd_ref, r_o, k_o, v_o, kk_o, lr_o, ld_o, g_o, *, nt):
    i = pl.program_id(0)
    feat = feat_ref[...]
    tm = feat.shape[0]
    first = jnp.where(i % nt == 0, 0.0, 1.0)
    prev_row = prev_ref[7:8, :] * first
    rolled = pltpu.roll(feat, 1, axis=0)
    rid = lax.broadcasted_iota(jnp.int32, (tm, 1), 0)
    prev = jnp.where(rid == 0, prev_row, rolled)
    feat = feat + (prev - feat) * mu_ref[...]

    r = feat[:, :RWKV_W]
    k = feat[:, RWKV_W:2 * RWKV_W]
    v = feat[:, 2 * RWKV_W:3 * RWKV_W]
    wa = feat[:, 3 * RWKV_W:3 * RWKV_W + LANES]
    gl = feat[:, 3 * RWKV_W + LANES:]
    lane = lax.broadcasted_iota(jnp.int32, (tm, LANES), 1)
    wl_in = jnp.where(lane < DECAY_LORA, jnp.tanh(wa), 0.0)
    al_in = jnp.where(lane < DECAY_LORA, 0.0, wa)
    wpre = w0_ref[...] + _dot(wl_in, w2_ref[...])
    w = -(jnp.maximum(-wpre, 0.0) + jnp.log(1.0 + jnp.exp(-jnp.abs(wpre)))) - 0.5
    logd = -jnp.exp(w)
    a = jax.nn.sigmoid(a0_ref[...] + _dot(al_in, a2_ref[...]))
    g = _dot(jax.nn.sigmoid(gl), g2_ref[...])
    kk = k * kk_ref[...]
    ss = _dot_split(kk * kk, bd_ref[...], split_lhs=True)
    r_o[...] = r
    k_o[...] = k * (1.0 + (a - 1.0) * ka_ref[...])
    v_o[...] = v
    kk_o[...] = kk * lax.rsqrt(ss + 1e-12)
    lr_o[...] = a
    ld_o[...] = logd
    g_o[...] = g


def _rwprep(feat, rw_mu, rw_w0, rw_w2, rw_a0, rw_a2, rw_g2, rw_k_k, rw_k_a, bsz, seq, tm):
    t = feat.shape[0]
    nt = seq // tm
    w2p = jnp.concatenate([rw_w2, jnp.zeros((LANES - DECAY_LORA, RWKV_W), F32)], axis=0).astype(BF16)
    a2p = jnp.concatenate([jnp.zeros((DECAY_LORA, RWKV_W), F32), rw_a2], axis=0).astype(BF16)
    args = (feat, feat, rw_mu[None, :], rw_w0[None, :], w2p, rw_a0[None, :], a2p, rw_g2.astype(BF16),
            rw_k_k[None, :], rw_k_a[None, :], _head_indicator(RWKV_W))
    full = lambda a: pl.BlockSpec(a.shape, lambda i: (0,) * a.ndim)
    rows8 = tm // 8
    in_specs = [pl.BlockSpec((tm, RWKV_IN), lambda i: (i, 0)),
                pl.BlockSpec((8, RWKV_IN), lambda i: (jnp.maximum(i * rows8 - 1, 0), 0))]
    in_specs += [full(a) for a in args[2:]]
    hshape = jax.ShapeDtypeStruct((t, RWKV_W), F32)
    hspec = pl.BlockSpec((tm, RWKV_W), lambda i: (i, 0))
    return pl.pallas_call(
        functools.partial(_rwprep_kernel, nt=nt),
        grid=(t // tm,),
        in_specs=in_specs,
        out_specs=[hspec] * 7,
        out_shape=[hshape] * 7,
        compiler_params=_params(("parallel",)),
        name="rwprep",
    )(*args)


def _rwscan_kernel(r_ref, k_ref, v_ref, kk_ref, lr_ref, ld_ref, g_ref, rk_ref, lng_ref, lnb_ref,
                   tri_ref, bd_ref, o_ref, state, *, chunk):
    c = pl.program_id(1)

    @pl.when(c == 0)
    def _():
        state[...] = jnp.zeros_like(state)

    C = chunk
    npairs = RWKV_W // LANES
    r = r_ref[0]
    k = k_ref[0]
    v = v_ref[0]
    kk = kk_ref[0]
    ld = ld_ref[0]
    bd = bd_ref[...]

    cum = _dot_split(tri_ref[...], ld, split_lhs=False)
    mid = C // 2 - 1
    cmid = cum[mid:mid + 1, :]
    cend = cum[C - 1:C, :]
    e_pos = jnp.exp(cum - cmid)
    e_prev = jnp.exp(cum - ld - cmid)
    e_neg = jnp.exp(cmid - cum)
    g_mid = jnp.exp(cmid)
    g_end_rel = jnp.exp(cend - cmid)
    g_end = jnp.exp(cend)
    rt = r * e_pos
    at = -kk * e_prev
    bt = kk * lr_ref[0] * e_neg
    kt = k * e_neg
    at_g = (at * g_mid).astype(BF16)
    rt_g = (rt * g_mid).astype(BF16)
    art = jnp.concatenate([at, rt], axis=0)
    bk = jnp.concatenate([bt, kt], axis=0).astype(BF16)
    bkg = jnp.concatenate([bt * g_end_rel, kt * g_end_rel], axis=0).astype(BF16)
    vb = v.astype(BF16)

    row = lax.broadcasted_iota(jnp.int32, (C, C), 0)
    col = lax.broadcasted_iota(jnp.int32, (C, C), 1)
    strict = row > col
    eye = (row == col).astype(F32)
    row2 = lax.broadcasted_iota(jnp.int32, (C, 2 * C), 0)
    col2 = lax.broadcasted_iota(jnp.int32, (C, 2 * C), 1)
    incl2 = row2 >= jnp.where(col2 >= C, col2 - C, col2)
    low2 = lax.broadcasted_iota(jnp.int32, (2 * C, LANES), 1) < RWKV_N
    low = lax.broadcasted_iota(jnp.int32, (C, LANES), 1) < RWKV_N
    srow = lax.broadcasted_iota(jnp.int32, (LANES, LANES), 0)
    scol = lax.broadcasted_iota(jnp.int32, (LANES, LANES), 1)
    same_head = (srow < RWKV_N) == (scol < RWKV_N)
    diag = srow == scol
    pair = lambda a, p: a[:, p * LANES:(p + 1) * LANES]

    a_ab, a_ak, a_rbk = [], [], []
    for p in range(npairs):
        art_p = pair(art, p)
        bk_p = pair(bk, p)
        for half in range(2):
            lhs = jnp.where(low2, art_p, 0.0) if half == 0 else jnp.where(low2, 0.0, art_p)
            big = _dot_nt(lhs, bk_p)
            a_ab.append(jnp.where(strict, big[:C, :C], 0.0))
            a_ak.append(jnp.where(strict, big[:C, C:], 0.0).astype(BF16))
            a_rbk.append(jnp.where(incl2, big[C:, :], 0.0).astype(BF16))

    ps = [a.astype(BF16) for a in a_ab]
    tinv = [eye + a for a in a_ab]
    n = 2
    while n < C:
        sq = [jnp.dot(pb, pb, preferred_element_type=F32) for pb in ps]
        ps = [s.astype(BF16) for s in sq]
        tinv = [t + jnp.dot(t.astype(BF16), pb, preferred_element_type=F32) for t, pb in zip(tinv, ps)]
        n *= 2
    tinv = [t.astype(BF16) for t in tinv]

    s0 = [state[p] for p in range(npairs)]
    s0b = [s.astype(BF16) for s in s0]
    sel = lambda h0, h1: jnp.where(low, h0, h1)
    mm = lambda a, b: jnp.dot(a, b, preferred_element_type=F32)
    xs = [mm(pair(at_g, p), s0b[p]) + sel(mm(a_ak[2 * p], pair(vb, p)), mm(a_ak[2 * p + 1], pair(vb, p)))
          for p in range(npairs)]
    xb = [x.astype(BF16) for x in xs]
    us = [sel(mm(tinv[2 * p], xb[p]), mm(tinv[2 * p + 1], xb[p])) for p in range(npairs)]
    uv = [jnp.concatenate([us[p].astype(BF16), pair(vb, p)], axis=0) for p in range(npairs)]
    ys = [mm(pair(rt_g, p), s0b[p]) + sel(mm(a_rbk[2 * p], uv[p]), mm(a_rbk[2 * p + 1], uv[p]))
          for p in range(npairs)]
    for p in range(npairs):
        upd = lax.dot_general(pair(bkg, p), uv[p], (((0,), (0,)), ((), ())), preferred_element_type=F32)
        g_col = jnp.sum(jnp.where(diag, pair(g_end, p), 0.0), axis=-1, keepdims=True)
        state[p] = s0[p] * g_col + jnp.where(same_head, upd, 0.0)

    inv_n = 1.0 / RWKV_N
    for p in range(npairs):
        sl = slice(p * LANES, (p + 1) * LANES)
        y = ys[p]
        mean = _dot_split(y, bd, split_lhs=True) * inv_n
        yc = y - mean
        var = _dot_split(yc * yc, bd, split_lhs=True) * inv_n
        yn = yc * lax.rsqrt(var + LNX_EPS) * lng_ref[:, sl] + lnb_ref[:, sl]
        bonus = _dot_split(pair(r, p) * pair(k, p) * rk_ref[:, sl], bd, split_lhs=True) * pair(v, p)
        o_ref[0, :, sl] = (yn + bonus) * g_ref[0, :, sl]


def _rwscan(r, k, v, kk, lr, ld, g, rw_r_k, rw_lnx_g, rw_lnx_b, bsz, seq, chunk):
    width = r.shape[-1]
    tri = jnp.asarray(np.tril(np.ones((chunk, chunk), np.float32))).astype(BF16)
    bd = _head_indicator(LANES)
    seq3 = lambda a: a.reshape(bsz, seq, width)
    cspec = pl.BlockSpec((1, chunk, width), lambda b, c: (b, c, 0))
    small = pl.BlockSpec((1, width), lambda b, c: (0, 0))
    return pl.pallas_call(
        functools.partial(_rwscan_kernel, chunk=chunk),
        grid=(bsz, seq // chunk),
        in_specs=[cspec] * 7 + [small] * 3 + [pl.BlockSpec((chunk, chunk), lambda b, c: (0, 0)),
                                              pl.BlockSpec((LANES, LANES), lambda b, c: (0, 0))],
        out_specs=cspec,
        out_shape=jax.ShapeDtypeStruct((bsz, seq, width), F32),
        scratch_shapes=[pltpu.VMEM((width // LANES, LANES, LANES), F32)],
        compiler_params=_params(("parallel", "arbitrary")),
        name="rwscan",
    )(seq3(r), seq3(k), seq3(v), seq3(kk), seq3(lr), seq3(ld), seq3(g), rw_r_k.reshape(1, width),
      rw_lnx_g[None, :], rw_lnx_b[None, :], tri, bd)


def _outproj_kernel(x_ref, mla_ref, rw_ref, gmo_ref, wo_ref, gffn_ref, wr_ref, br_ref,
                    x1_ref, comb_ref):
    tm = x_ref.shape[0]
    m = mla_ref[...]
    mn = m * lax.rsqrt(jnp.mean(m * m, axis=-1, keepdims=True) + NORM_EPS) * gmo_ref[...]
    mix = jnp.concatenate([mn.astype(BF16), rw_ref[...].astype(BF16)], axis=-1)
    acc = x_ref[...] + jnp.dot(mix, wo_ref[...], preferred_element_type=F32)
    x1_ref[...] = acc
    xn = acc * lax.rsqrt(jnp.mean(acc * acc, axis=-1, keepdims=True) + NORM_EPS) * gffn_ref[...]

    logits = _dot_bf16x3(xn, wr_ref[...]) + br_ref[...]
    lane = lax.broadcasted_iota(jnp.int32, (tm, LANES), 1)
    neg = -jnp.inf
    big = jnp.int32(1 << 20)
    is_g = (lane >= N_EXPERTS) & (lane < N_EXPERTS + N_GROUPS)
    glog = jnp.where(is_g, logits, neg)
    gmax = jnp.max(glog, axis=-1, keepdims=True)
    gsel = jnp.min(jnp.where(glog == gmax, lane, big), axis=-1, keepdims=True) - N_EXPERTS
    gprob = 1.0 / jnp.sum(jnp.exp(glog - gmax), axis=-1, keepdims=True)
    lo = gsel * EXPERTS_PER_GROUP
    in_group = (lane >= lo) & (lane < lo + EXPERTS_PER_GROUP)
    elog = jnp.where(in_group, logits, neg)
    v1 = jnp.max(elog, axis=-1, keepdims=True)
    i1 = jnp.min(jnp.where(elog == v1, lane, big), axis=-1, keepdims=True)
    elog2 = jnp.where(lane == i1, neg, elog)
    v2 = jnp.max(elog2, axis=-1, keepdims=True)
    i2 = jnp.min(jnp.where(elog2 == v2, lane, big), axis=-1, keepdims=True)
    e2 = jnp.exp(v2 - v1)
    w1 = gprob / (1.0 + e2)
    w2 = gprob * e2 / (1.0 + e2)
    comb_ref[...] = jnp.where(lane == 0, i1.astype(F32), jnp.where(lane == 1, i2.astype(F32),
                              jnp.where(lane == 2, w1, jnp.where(lane == 3, w2, 0.0))))


def _outproj(x2, mla, rw, g_mla_out, w_o, g_ffn, w_group, b_group, w_expert, b_expert, seq, tm):
    t = x2.shape[0]
    wo = w_o.astype(BF16)
    pad = LANES - N_EXPERTS - N_GROUPS
    wr = jnp.concatenate([w_expert, w_group, jnp.zeros((D_MODEL, pad), F32)], axis=-1)
    br = jnp.concatenate([b_expert, b_group, jnp.zeros((pad,), F32)])[None, :]
    gmo = g_mla_out[None, :]
    full = lambda a: pl.BlockSpec(a.shape, lambda i: (0,) * a.ndim)
    row = pl.BlockSpec((tm, D_MODEL), lambda i: (i, 0))
    return pl.pallas_call(
        _outproj_kernel,
        grid=(t // tm,),
        in_specs=[row,
                  pl.BlockSpec((tm, MLA_W), lambda i: (i, 0)),
                  pl.BlockSpec((tm, RWKV_W), lambda i: (i, 0)),
                  full(gmo), full(wo), pl.BlockSpec((1, D_MODEL), lambda i: (0, 0)), full(wr), full(br)],
        out_specs=[row, pl.BlockSpec((tm, LANES), lambda i: (i, 0))],
        out_shape=[jax.ShapeDtypeStruct((t, D_MODEL), F32),
                   jax.ShapeDtypeStruct((t, LANES), F32)],
        compiler_params=_params(("parallel",)),
        name="outproj",
    )(x2, mla, rw, gmo, wo, g_ffn[None, :], wr, br)


def _moeplan_kernel(route_ref, tri_ref, triu_ref, posw_ref, nb_ref, roff_ref):
    rt = route_ref[...]
    tb = rt.shape[0]
    lane = lax.broadcasted_iota(jnp.int32, (tb, LANES), 1)
    oh1 = lane == rt[:, 0:1].astype(jnp.int32)
    oh2 = lane == rt[:, 1:2].astype(jnp.int32)
    oh = jnp.where(oh1, 1.0, jnp.where(oh2, 1.0, 0.0))
    excl = jnp.dot(tri_ref[...], oh.astype(BF16), preferred_element_type=F32)
    cnt = jnp.sum(oh, axis=0, keepdims=True)
    nb = jnp.floor((cnt + (MOE_RB - 0.5)) * (1.0 / MOE_RB))
    boff = jnp.dot(jnp.broadcast_to(nb, (8, LANES)).astype(BF16), triu_ref[...],
                   preferred_element_type=F32)[0:1]
    roff = boff * MOE_RB
    base = roff + excl
    pos1 = jnp.sum(jnp.where(oh1, base, 0.0), axis=-1, keepdims=True)
    pos2 = jnp.sum(jnp.where(oh2, base, 0.0), axis=-1, keepdims=True)
    posw_ref[...] = jnp.where(lane == 0, pos1, jnp.where(lane == 1, pos2, rt))
    nb_ref[0] = nb.astype(jnp.int32)
    roff_ref[0] = roff.astype(jnp.int32)


def _moeplan(route, tb):
    t = route.shape[0]
    nt = t // tb
    tri = jnp.asarray(np.tril(np.ones((tb, tb), np.float32), -1)).astype(BF16)
    triu = jnp.asarray(np.triu(np.ones((LANES, LANES), np.float32), 1)).astype(BF16)
    small = pl.BlockSpec((1, 1, LANES), lambda i: (i, 0, 0))
    return pl.pallas_call(
        _moeplan_kernel,
        grid=(nt,),
        in_specs=[pl.BlockSpec((tb, LANES), lambda i: (i, 0)),
                  pl.BlockSpec((tb, tb), lambda i: (0, 0)),
                  pl.BlockSpec((LANES, LANES), lambda i: (0, 0))],
        out_specs=[pl.BlockSpec((tb, LANES), lambda i: (i, 0)), small, small],
        out_shape=[jax.ShapeDtypeStruct((t, LANES), F32),
                   jax.ShapeDtypeStruct((nt, 1, LANES), jnp.int32),
                   jax.ShapeDtypeStruct((nt, 1, LANES), jnp.int32)],
        compiler_params=_params(("parallel",)),
        name="moeplan",
    )(route, tri, triu)


def _moe_kernel(pos_ref, nb_ref, roff_ref, x1_ref, posw_ref, gffn_ref, wg_hbm, wu_hbm, wd_hbm, o_ref,
                xn, xs, g1, g2, wg_buf, wu_buf, wd_buf, sem):
    i = pl.program_id(0)
    nt = pl.num_programs(0)
    tb = x1_ref.shape[0]
    chunk = g1.shape[0]
    dummy_r0 = xs.shape[0] - MOE_RB
    group = MOE_EXPERTS_PER_STEP
    ngroups = N_EXPERTS // group

    def weight_copies(e, slot):
        return (pltpu.make_async_copy(wg_hbm.at[e], wg_buf.at[slot], sem.at[slot, 0]),
                pltpu.make_async_copy(wu_hbm.at[e], wu_buf.at[slot], sem.at[slot, 1]),
                pltpu.make_async_copy(wd_hbm.at[e], wd_buf.at[slot], sem.at[slot, 2]))

    ahead = MOE_WEIGHT_BUFS - 1

    def start_group(n):
        first = lax.rem(n, ngroups) * group
        ring = lax.rem(n, MOE_WEIGHT_BUFS) * group
        for j in range(group):
            for c in weight_copies(first + j, ring + j):
                c.start()

    @pl.when(i == 0)
    def _():
        for n in range(ahead):
            start_group(n)
        xs[...] = jnp.zeros_like(xs)

    for c in range(tb // chunk):
        sl = slice(c * chunk, (c + 1) * chunk)
        x1 = x1_ref[sl, :]
        xn[sl, :] = x1 * lax.rsqrt(jnp.mean(x1 * x1, axis=-1, keepdims=True) + NORM_EPS) * gffn_ref[...]

    def put(tok, carry):
        row = xn[pl.ds(tok, 1), :]
        xs[pl.ds(pos_ref[2 * tok], 1), :] = row
        xs[pl.ds(pos_ref[2 * tok + 1], 1), :] = row
        return carry

    lax.fori_loop(0, tb, put, 0, unroll=8)

    def expert_group(p, carry):
        n = i * ngroups + p
        base = lax.rem(n, MOE_WEIGHT_BUFS) * group

        @pl.when(n + ahead < nt * ngroups)
        def _():
            start_group(n + ahead)

        for j in range(group):
            for c in weight_copies(p * group + j, base + j):
                c.wait()
        nbs = [nb_ref[0, 0, p * group + j] for j in range(group)]
        roffs = [roff_ref[0, 0, p * group + j] for j in range(group)]
        trips = functools.reduce(jnp.maximum, nbs)

        def block(b, c2):
            r0s = [pl.multiple_of(jnp.where(b < nbs[j], roffs[j] + b * MOE_RB, dummy_r0), 8)
                   for j in range(group)]
            x = [xs[pl.ds(r0, MOE_RB), :].astype(BF16) for r0 in r0s]
            hg = [jnp.dot(x[j], wg_buf[base + j], preferred_element_type=F32) for j in range(group)]
            hu = [jnp.dot(x[j], wu_buf[base + j], preferred_element_type=F32) for j in range(group)]
            hid = [(hg[j] * jax.nn.sigmoid(hg[j]) * hu[j]).astype(BF16) for j in range(group)]
            y = [jnp.dot(hid[j], wd_buf[base + j], preferred_element_type=F32) for j in range(group)]
            for j in range(group):
                xs[pl.ds(r0s[j], MOE_RB), :] = y[j]
            return c2

        lax.fori_loop(0, trips, block, 0)
        return carry

    lax.fori_loop(0, ngroups, expert_group, 0)

    for c in range(tb // chunk):
        def take(j, carry, c=c):
            tok = c * chunk + j
            g1[pl.ds(j, 1), :] = xs[pl.ds(pos_ref[2 * tok], 1), :]
            g2[pl.ds(j, 1), :] = xs[pl.ds(pos_ref[2 * tok + 1], 1), :]
            return carry

        lax.fori_loop(0, chunk, take, 0, unroll=8)
        sl = slice(c * chunk, (c + 1) * chunk)
        o_ref[sl, :] = x1_ref[sl, :] + (posw_ref[sl, 2:3] * g1[...] + posw_ref[sl, 3:4] * g2[...])


def _moe(x1, route, g_ffn, w_gate, w_up, w_down, tb):
    t = x1.shape[0]
    nt = t // tb
    chunk = min(tb, 256)
    rows = 2 * tb + N_EXPERTS * MOE_RB + MOE_RB
    nslot = MOE_WEIGHT_BUFS * MOE_EXPERTS_PER_STEP
    posw, nb, roff = _moeplan(route, tb)
    pos = posw[:, 0:2].astype(jnp.int32).reshape(2 * t)
    smem3 = lambda: pl.BlockSpec((1, 1, LANES), lambda i: (i, 0, 0), memory_space=pltpu.SMEM)
    hbm = lambda: pl.BlockSpec(memory_space=pl.ANY)
    return pl.pallas_call(
        _moe_kernel,
        grid=(nt,),
        in_specs=[pl.BlockSpec((2 * tb,), lambda i: (i,), memory_space=pltpu.SMEM), smem3(), smem3(),
                  pl.BlockSpec((tb, D_MODEL), lambda i: (i, 0)),
                  pl.BlockSpec((tb, LANES), lambda i: (i, 0)),
                  pl.BlockSpec((1, D_MODEL), lambda i: (0, 0)),
                  hbm(), hbm(), hbm()],
        out_specs=pl.BlockSpec((tb, D_MODEL), lambda i: (i, 0)),
        out_shape=jax.ShapeDtypeStruct((t, D_MODEL), F32),
        scratch_shapes=[pltpu.VMEM((tb, D_MODEL), F32), pltpu.VMEM((rows, D_MODEL), F32),
                        pltpu.VMEM((chunk, D_MODEL), F32), pltpu.VMEM((chunk, D_MODEL), F32),
                        pltpu.VMEM((nslot, D_MODEL, D_EXPERT), BF16), pltpu.VMEM((nslot, D_MODEL, D_EXPERT), BF16),
                        pltpu.VMEM((nslot, D_EXPERT, D_MODEL), BF16), pltpu.SemaphoreType.DMA((nslot, 3))],
        compiler_params=_params(("arbitrary",)),
        name="moe",
    )(pos, nb, roff, x1, posw, g_ffn[None, :], w_gate.astype(BF16), w_up.astype(BF16), w_down.astype(BF16))


def _tile(seq, want):
    tm = min(seq, want)
    assert seq % tm == 0 and tm % 8 == 0
    return tm


def _layer(x, positions, g_mix, w_in, g_cq, w_uq, g_ckv, w_uk, w_uv, g_qn, g_kn, g_mla_out,
           rw_mu, rw_w0, rw_w2, rw_a0, rw_a2, rw_g2, rw_k_k, rw_k_a, rw_r_k, rw_lnx_g, rw_lnx_b,
           w_o, g_ffn, w_group, b_group, w_expert, b_expert, w_gate, w_up, w_down):
    bsz, seq, d = x.shape
    t = bsz * seq
    x2 = x.reshape(t, d)
    pos2 = positions.reshape(t, 1)
    kv_blk = _tile(seq, KV_BLK)
    feat, q, k, v = _inproj(x2, pos2, g_mix, w_in, g_cq, w_uq, g_ckv, w_uk, w_uv, g_qn, g_kn,
                            bsz, seq, _tile(kv_blk, 512), kv_blk)
    mla = _attn(q, k, v).reshape(t, MLA_W)
    r, k2, vv, kk, lr, ld, g = _rwprep(feat, rw_mu, rw_w0, rw_w2, rw_a0, rw_a2, rw_g2, rw_k_k, rw_k_a,
                                       bsz, seq, _tile(seq, 512))
    rw = _rwscan(r, k2, vv, kk, lr, ld, g, rw_r_k, rw_lnx_g, rw_lnx_b, bsz, seq, _tile(seq, 128))
    rw = rw.reshape(t, RWKV_W)
    x1, route = _outproj(x2, mla, rw, g_mla_out, w_o, g_ffn, w_group, b_group, w_expert, b_expert,
                         seq, _tile(seq, 512))
    out = _moe(x1, route, g_ffn, w_gate, w_up, w_down, _tile(t, MOE_TB))
    return out.reshape(bsz, seq, d)


def kernel(x, positions, g_mix, w_in, g_cq, w_uq, g_ckv, w_uk, w_uv, g_qn, g_kn, g_mla_out, rw_mu, rw_w0, rw_w2, rw_a0, rw_a2, rw_g2, rw_k_k, rw_k_a, rw_r_k, rw_lnx_g, rw_lnx_b, w_o, g_ffn, w_group, b_group, w_expert, b_expert, w_gate, w_up, w_down):
    for l in range(g_mix.shape[0]):
        x = _layer(x, positions, g_mix[l], w_in[l], g_cq[l], w_uq[l], g_ckv[l], w_uk[l], w_uv[l],
                   g_qn[l], g_kn[l], g_mla_out[l], rw_mu[l], rw_w0[l], rw_w2[l], rw_a0[l], rw_a2[l],
                   rw_g2[l], rw_k_k[l], rw_k_a[l], rw_r_k[l], rw_lnx_g[l], rw_lnx_b[l], w_o[l],
                   g_ffn[l], w_group[l], b_group[l], w_expert[l], b_expert[l], w_gate[l], w_up[l],
                   w_down[l])
    return x
```

```python
import functools

import numpy as np
import jax
import jax.numpy as jnp
from jax import lax
from jax.experimental import pallas as pl
from jax.experimental.pallas import tpu as pltpu

F32 = jnp.float32
BF16 = jnp.bfloat16

D_MODEL = 1024
MLA_HEADS = 8
MLA_NOPE = 64
MLA_ROPE = 32
MLA_QK = MLA_NOPE + MLA_ROPE
MLA_V = 64
MLA_W = MLA_HEADS * MLA_V
MLA_Q_RANK = 256
MLA_KV_RANK = 128
ROPE_THETA = 10000.0
RWKV_HEADS = 8
RWKV_N = 64
RWKV_W = RWKV_HEADS * RWKV_N
DECAY_LORA = 64
AAA_LORA = 64
GATE_LORA = 128
MLA_IN = MLA_Q_RANK + MLA_KV_RANK + MLA_ROPE
RWKV_IN = 3 * RWKV_W + DECAY_LORA + AAA_LORA + GATE_LORA
N_GROUPS = 4
EXPERTS_PER_GROUP = 8
N_EXPERTS = N_GROUPS * EXPERTS_PER_GROUP
D_EXPERT = 256
NORM_EPS = 1e-6
LNX_EPS = 64e-5

LANES = 128
QK_PAD = LANES
KV_BLK = 512
ATT_HEADS = 4
RWSCAN_ROWS = 4
MOE_TB = 1024
MOE_RB = 96
MOE_EXPERTS_PER_STEP = 2
MOE_WEIGHT_BUFS = 3
VT_ROWS = MLA_V + 16
Q_SCALE = MLA_QK ** -0.5 * float(np.log2(np.e))
PM_W = MLA_Q_RANK + MLA_KV_RANK + 2 * LANES
VMEM_LIMIT = 56 * 1024 * 1024


def _dot_bf16x3(a, b):
    a_hi = a.astype(BF16)
    a_lo = (a - a_hi.astype(F32)).astype(BF16)
    b_hi = b.astype(BF16)
    b_lo = (b - b_hi.astype(F32)).astype(BF16)
    d = lambda p, q: jnp.dot(p, q, preferred_element_type=F32)
    return d(a_hi, b_hi) + (d(a_hi, b_lo) + d(a_lo, b_hi))


def _dot(a, b):
    return jnp.dot(a.astype(BF16), b.astype(BF16), preferred_element_type=F32)


def _dot_nt(a, b):
    return lax.dot_general(a.astype(BF16), b.astype(BF16), (((1,), (1,)), ((), ())),
                           preferred_element_type=F32)


def _dot_tn(a, b):
    return lax.dot_general(a.astype(BF16), b.astype(BF16), (((0,), (0,)), ((), ())),
                           preferred_element_type=F32)


def _dot_split(a, b, split_lhs, passes=2):
    s = a if split_lhs else b
    out = None
    for _ in range(passes):
        piece = s.astype(BF16)
        term = (jnp.dot(piece, b, preferred_element_type=F32) if split_lhs
                else jnp.dot(a, piece, preferred_element_type=F32))
        out = term if out is None else out + term
        s = s - piece.astype(F32)
    return out


def _head_indicator(width):
    hid = np.arange(width) // RWKV_N
    return jnp.asarray((hid[:, None] == hid[None, :]).astype(np.float32)).astype(BF16)


def _params(sem):
    return pltpu.CompilerParams(dimension_semantics=sem, vmem_limit_bytes=VMEM_LIMIT)


def _inproj_kernel(x_ref, pos_ref, gmix_ref, wmla_ref, wrw_ref, freq_ref, gcq_ref, gckv_ref,
                   wq_ref, wqr_ref, wk_ref, wv_ref, gq_ref, gqr_ref, gk_ref, gkr_ref,
                   feat_ref, q_ref, k_ref, v_ref):
    x = x_ref[...]
    h = x * lax.rsqrt(jnp.mean(x * x, axis=-1, keepdims=True) + NORM_EPS) * gmix_ref[...]
    hb = h.astype(BF16)
    feat_ref[...] = jnp.dot(hb, wrw_ref[...], preferred_element_type=F32)
    pm = jnp.dot(hb, wmla_ref[...], preferred_element_type=F32)

    c_q = pm[:, :MLA_Q_RANK]
    c_kv = pm[:, MLA_Q_RANK:MLA_Q_RANK + MLA_KV_RANK]
    kr = pm[:, MLA_Q_RANK + MLA_KV_RANK:MLA_Q_RANK + MLA_KV_RANK + LANES]
    kr_rot = pm[:, MLA_Q_RANK + MLA_KV_RANK + LANES:PM_W]
    cqn = (c_q * lax.rsqrt(jnp.mean(c_q * c_q, axis=-1, keepdims=True) + NORM_EPS)
           * gcq_ref[...]).astype(BF16)
    ckvn = (c_kv * lax.rsqrt(jnp.mean(c_kv * c_kv, axis=-1, keepdims=True) + NORM_EPS)
            * gckv_ref[...]).astype(BF16)

    ang = pos_ref[...].astype(F32) * freq_ref[...]
    cosf = jnp.cos(ang)
    sinf = jnp.sin(ang)
    q_cos = gq_ref[...] * cosf * Q_SCALE
    q_sin = gqr_ref[...] * sinf * Q_SCALE
    k_cos = gk_ref[...] * cosf
    k_sin = kr_rot * (gkr_ref[...] * sinf)
    inv_qk = 1.0 / MLA_QK
    ones_rows = jnp.where(lax.broadcasted_iota(jnp.int32, (VT_ROWS - MLA_V, x.shape[0]), 0) == 0, 1.0, 0.0)

    for hh in range(MLA_HEADS):
        q_raw = jnp.dot(cqn, wq_ref[hh], preferred_element_type=F32)
        q_rot = jnp.dot(cqn, wqr_ref[hh], preferred_element_type=F32)
        sq = lax.rsqrt(jnp.sum(q_raw * q_raw, axis=-1, keepdims=True) * inv_qk + NORM_EPS)
        q_ref[0, hh] = (sq * (q_raw * q_cos + q_rot * q_sin)).astype(BF16)
        k_raw = jnp.dot(ckvn, wk_ref[hh], preferred_element_type=F32) + kr
        sk = lax.rsqrt(jnp.sum(k_raw * k_raw, axis=-1, keepdims=True) * inv_qk + NORM_EPS)
        k_ref[0, hh] = (sk * (k_raw * k_cos + k_sin)).astype(BF16)
        vt = lax.dot_general(wv_ref[hh], ckvn, (((1,), (1,)), ((), ())), preferred_element_type=F32)
        v_ref[0, hh, 0] = jnp.concatenate([vt, ones_rows], axis=0).astype(BF16)


def _rot_cols(w):
    half = MLA_ROPE // 2
    z = jnp.zeros_like(w[..., :MLA_NOPE])
    return jnp.concatenate([z, -w[..., MLA_NOPE + half:MLA_QK], w[..., MLA_NOPE:MLA_NOPE + half]], axis=-1)


def _pad_last(w, n):
    return jnp.pad(w, [(0, 0)] * (w.ndim - 1) + [(0, n - w.shape[-1])])


def _inproj(x2, pos2, g_mix, w_in, g_cq, w_uq, g_ckv, w_uk, w_uv, g_qn, g_kn, bsz, seq, tm, kv_blk):
    t = x2.shape[0]
    nt = seq // tm
    per_kv = kv_blk // tm
    half = MLA_ROPE // 2
    w_kr = w_in[:, MLA_Q_RANK + MLA_KV_RANK:MLA_IN]
    zeros64 = jnp.zeros((D_MODEL, MLA_NOPE), F32)
    zeros32 = jnp.zeros((D_MODEL, LANES - MLA_QK), F32)
    w_kr_rot = jnp.concatenate([-w_kr[:, half:], w_kr[:, :half]], axis=-1)
    w_mla = jnp.concatenate([w_in[:, :MLA_Q_RANK + MLA_KV_RANK], zeros64, w_kr, zeros32,
                             zeros64, w_kr_rot, zeros32], axis=-1).astype(BF16)
    w_rw = w_in[:, MLA_IN:].astype(BF16)
    wq = jnp.transpose(w_uq, (1, 0, 2))
    wq_rot = _pad_last(_rot_cols(wq), QK_PAD).astype(BF16)
    wq = _pad_last(wq, QK_PAD).astype(BF16)
    wk = _pad_last(jnp.transpose(w_uk, (1, 0, 2)), QK_PAD).astype(BF16)
    wv = jnp.transpose(w_uv, (1, 2, 0)).astype(BF16)
    inv_freq = 1.0 / (ROPE_THETA ** (jnp.arange(half, dtype=F32) / half))
    freq = jnp.concatenate([jnp.zeros((MLA_NOPE,), F32), inv_freq, inv_freq,
                            jnp.zeros((LANES - MLA_QK,), F32)])[None, :]
    swap = lambda g: jnp.concatenate([jnp.zeros((MLA_NOPE,), F32), g[MLA_NOPE + half:MLA_QK],
                                      g[MLA_NOPE:MLA_NOPE + half], jnp.zeros((LANES - MLA_QK,), F32)])[None, :]
    gq = _pad_last(g_qn[None, :], QK_PAD)
    gk = _pad_last(g_kn[None, :], QK_PAD)
    gq_rot = swap(g_qn)
    gk_rot = swap(g_kn)

    full = lambda a: pl.BlockSpec(a.shape, lambda i: (0,) * a.ndim)
    head_map = lambda i: (i // nt, 0, i % nt, 0)
    args = (x2, pos2, g_mix[None, :], w_mla, w_rw, freq, g_cq[None, :], g_ckv[None, :],
            wq, wq_rot, wk, wv, gq, gq_rot, gk, gk_rot)
    in_specs = [pl.BlockSpec((tm, D_MODEL), lambda i: (i, 0)),
                pl.BlockSpec((tm, 1), lambda i: (i, 0))] + [full(a) for a in args[2:]]
    return pl.pallas_call(
        _inproj_kernel,
        grid=(t // tm,),
        in_specs=in_specs,
        out_specs=[pl.BlockSpec((tm, RWKV_IN), lambda i: (i, 0)),
                   pl.BlockSpec((1, MLA_HEADS, tm, QK_PAD), head_map),
                   pl.BlockSpec((1, MLA_HEADS, tm, QK_PAD), head_map),
                   pl.BlockSpec((1, MLA_HEADS, 1, VT_ROWS, tm),
                                lambda i: (i // nt, 0, (i % nt) // per_kv, 0, (i % nt) % per_kv))],
        out_shape=[jax.ShapeDtypeStruct((t, RWKV_IN), F32),
                   jax.ShapeDtypeStruct((bsz, MLA_HEADS, seq, QK_PAD), BF16),
                   jax.ShapeDtypeStruct((bsz, MLA_HEADS, seq, QK_PAD), BF16),
                   jax.ShapeDtypeStruct((bsz, MLA_HEADS, seq // kv_blk, VT_ROWS, kv_blk), BF16)],
        compiler_params=_params(("parallel",)),
        name="inproj",
    )(*args)


def _attn_kernel(q_ref, k_ref, vt_ref, o_ref, s_scr):
    t = pl.program_id(2)
    nh = q_ref.shape[1]
    kvb = vt_ref.shape[4]
    qb = kvb

    def q_tile(w):
        return [q_ref[0, h, w * qb:(w + 1) * qb, :] for h in range(nh)]

    def scores_to(qs, g, slot):
        start = pl.multiple_of(g * kvb, kvb)
        for h in range(nh):
            s_scr[slot, h] = lax.dot_general(k_ref[0, h, pl.ds(start, kvb), :], qs[h], (((1,), (1,)), ((), ())),
                                             preferred_element_type=F32)

    def consume(qi, g, slot, carry, masked):
        ss = [s_scr[slot, h] for h in range(nh)]
        if masked:
            key = g * kvb + lax.broadcasted_iota(jnp.int32, (kvb, qb), 0)
            qry = qi * qb + lax.broadcasted_iota(jnp.int32, (kvb, qb), 1)
            ss = [jnp.where(key <= qry, s, -jnp.inf) for s in ss]
        ms = [jnp.maximum(carry[h][0], jnp.max(ss[h], axis=0, keepdims=True)) for h in range(nh)]
        ps = [jnp.exp2((ss[h] - ms[h]).astype(BF16)) for h in range(nh)]
        out = []
        for h in range(nh):
            m, acc = carry[h]
            alpha = jnp.exp2(m - ms[h])
            out.append((ms[h], alpha * acc + jnp.dot(vt_ref[0, h, g], ps[h], preferred_element_type=F32)))
        return tuple(out)

    def trips(qs, qi, cur, nxt, carry):
        def two_groups(j, c):
            scores_to(qs, 2 * j + 1, nxt)
            c = consume(qi, 2 * j, cur, c, False)
            scores_to(qs, 2 * j + 2, cur)
            return consume(qi, 2 * j + 1, nxt, c, False)

        return lax.fori_loop(0, t, two_groups, carry)

    def finish(carry, w):
        ot = jnp.concatenate([acc[:MLA_V] * (1.0 / acc[MLA_V:MLA_V + 1]) for _, acc in carry], axis=0)
        o_ref[0, w * qb:(w + 1) * qb, :] = ot.T

    init = tuple((jnp.full((1, qb), -1e30, F32), jnp.zeros((VT_ROWS, qb), F32)) for _ in range(nh))
    q_a, q_b = q_tile(0), q_tile(1)
    tile_a, tile_b = 2 * t, 2 * t + 1

    scores_to(q_a, 0, 0)
    carry = trips(q_a, tile_a, 0, 1, init)
    scores_to(q_b, 0, 1)
    finish(consume(tile_a, 2 * t, 0, carry, True), 0)
    carry = trips(q_b, tile_b, 1, 0, init)
    scores_to(q_b, 2 * t + 1, 0)
    carry = consume(tile_b, 2 * t, 1, carry, False)
    finish(consume(tile_b, 2 * t + 1, 0, carry, True), 1)


def _attn(q, k, vt):
    bsz, nh, seq, _ = q.shape
    per_step = ATT_HEADS
    ngrp, kvb = vt.shape[2], vt.shape[4]
    assert seq % (2 * kvb) == 0
    return pl.pallas_call(
        _attn_kernel,
        grid=(bsz, nh // per_step, seq // (2 * kvb)),
        in_specs=[pl.BlockSpec((1, per_step, 2 * kvb, QK_PAD), lambda b, h, i: (b, h, i, 0)),
                  pl.BlockSpec((1, per_step, seq, QK_PAD), lambda b, h, i: (b, h, 0, 0)),
                  pl.BlockSpec((1, per_step, ngrp, VT_ROWS, kvb), lambda b, h, i: (b, h, 0, 0, 0))],
        out_specs=pl.BlockSpec((1, 2 * kvb, per_step * MLA_V), lambda b, h, i: (b, i, h)),
        out_shape=jax.ShapeDtypeStruct((bsz, seq, nh * MLA_V), F32),
        scratch_shapes=[pltpu.VMEM((2, per_step, kvb, kvb), F32)],
        compiler_params=_params(("parallel", "parallel", "arbitrary")),
        name="attn",
    )(q, k, vt)


def _rwprep_kernel(feat_ref, prev_ref, mu_ref, w0_ref, w2_ref, a0_ref, a2_ref, g2_ref, kk_ref, ka_ref,
                   bd_ref, r_o, k_o, v_o, kk_o, lr_o, ld_o, g_o, *, nt):
    i = pl.program_id(0)
    feat = feat_ref[...]
    tm = feat.shape[0]
    first = jnp.where(i % nt == 0, 0.0, 1.0)
    prev_row = prev_ref[7:8, :] * first
    rolled = pltpu.roll(feat, 1, axis=0)
    rid = lax.broadcasted_iota(jnp.int32, (tm, 1), 0)
    prev = jnp.where(rid == 0, prev_row, rolled)
    feat = feat + (prev - feat) * mu_ref[...]

    r = feat[:, :RWKV_W]
    k = feat[:, RWKV_W:2 * RWKV_W]
    v = feat[:, 2 * RWKV_W:3 * RWKV_W]
    wa = feat[:, 3 * RWKV_W:3 * RWKV_W + LANES]
    gl = feat[:, 3 * RWKV_W + LANES:]
    lane = lax.broadcasted_iota(jnp.int32, (tm, LANES), 1)
    wl_in = jnp.where(lane < DECAY_LORA, jnp.tanh(wa), 0.0)
    al_in = jnp.where(lane < DECAY_LORA, 0.0, wa)
    wpre = w0_ref[...] + _dot(wl_in, w2_ref[...])
    w = -(jnp.maximum(-wpre, 0.0) + jnp.log(1.0 + jnp.exp(-jnp.abs(wpre)))) - 0.5
    logd = -jnp.exp(w)
    a = jax.nn.sigmoid(a0_ref[...] + _dot(al_in, a2_ref[...]))
    g = _dot(jax.nn.sigmoid(gl), g2_ref[...])
    kk = k * kk_ref[...]
    ss = _dot_split(kk * kk, bd_ref[...], split_lhs=True)
    r_o[...] = r
    k_o[...] = k * (1.0 + (a - 1.0) * ka_ref[...])
    v_o[...] = v
    kk_o[...] = kk * lax.rsqrt(ss + 1e-12)
    lr_o[...] = a
    ld_o[...] = logd
    g_o[...] = g


def _rwprep(feat, rw_mu, rw_w0, rw_w2, rw_a0, rw_a2, rw_g2, rw_k_k, rw_k_a, bsz, seq, tm):
    t = feat.shape[0]
    nt = seq // tm
    w2p = jnp.concatenate([rw_w2, jnp.zeros((LANES - DECAY_LORA, RWKV_W), F32)], axis=0).astype(BF16)
    a2p = jnp.concatenate([jnp.zeros((DECAY_LORA, RWKV_W), F32), rw_a2], axis=0).astype(BF16)
    args = (feat, feat, rw_mu[None, :], rw_w0[None, :], w2p, rw_a0[None, :], a2p, rw_g2.astype(BF16),
            rw_k_k[None, :], rw_k_a[None, :], _head_indicator(RWKV_W))
    full = lambda a: pl.BlockSpec(a.shape, lambda i: (0,) * a.ndim)
    rows8 = tm // 8
    in_specs = [pl.BlockSpec((tm, RWKV_IN), lambda i: (i, 0)),
                pl.BlockSpec((8, RWKV_IN), lambda i: (jnp.maximum(i * rows8 - 1, 0), 0))]
    in_specs += [full(a) for a in args[2:]]
    hshape = jax.ShapeDtypeStruct((t, RWKV_W), F32)
    hspec = pl.BlockSpec((tm, RWKV_W), lambda i: (i, 0))
    return pl.pallas_call(
        functools.partial(_rwprep_kernel, nt=nt),
        grid=(t // tm,),
        in_specs=in_specs,
        out_specs=[hspec] * 7,
        out_shape=[hshape] * 7,
        compiler_params=_params(("parallel",)),
        name="rwprep",
    )(*args)


def _rwscan_kernel(r_ref, k_ref, v_ref, kk_ref, lr_ref, ld_ref, g_ref, rk_ref, lng_ref, lnb_ref,
                   tri_ref, bd_ref, o_ref, state, *, chunk):
    c = pl.program_id(1)

    @pl.when(c == 0)
    def _():
        state[...] = jnp.zeros_like(state)

    C = chunk
    nrows = r_ref.shape[0]
    pairs_per_row = RWKV_W // LANES
    npairs = nrows * pairs_per_row
    wide = lambda ref: jnp.concatenate([ref[b] for b in range(nrows)], axis=-1)
    tiled = lambda ref: jnp.concatenate([ref[...]] * nrows, axis=-1)
    r = wide(r_ref)
    k = wide(k_ref)
    v = wide(v_ref)
    kk = wide(kk_ref)
    ld = wide(ld_ref)
    lr = wide(lr_ref)
    gate = wide(g_ref)
    rk = tiled(rk_ref)
    lng = tiled(lng_ref)
    lnb = tiled(lnb_ref)
    bd = bd_ref[...]

    cum = _dot_split(tri_ref[...], ld, split_lhs=False)
    mid = C // 2 - 1
    cmid = cum[mid:mid + 1, :]
    cend = cum[C - 1:C, :]
    e_pos = jnp.exp(cum - cmid)
    e_prev = jnp.exp(cum - ld - cmid)
    e_neg = jnp.exp(cmid - cum)
    g_mid = jnp.exp(cmid)
    g_end_rel = jnp.exp(cend - cmid)
    g_end = jnp.exp(cend)
    rt = r * e_pos
    at = -kk * e_prev
    bt = kk * lr * e_neg
    kt = k * e_neg
    at_g = (at * g_mid).astype(BF16)
    rt_g = (rt * g_mid).astype(BF16)
    art = jnp.concatenate([at, rt], axis=0)
    bk = jnp.concatenate([bt, kt], axis=0).astype(BF16)
    bkg = jnp.concatenate([bt * g_end_rel, kt * g_end_rel], axis=0).astype(BF16)
    vb = v.astype(BF16)

    row = lax.broadcasted_iota(jnp.int32, (C, C), 0)
    col = lax.broadcasted_iota(jnp.int32, (C, C), 1)
    strict = row > col
    eye = (row == col).astype(F32)
    row2 = lax.broadcasted_iota(jnp.int32, (C, 2 * C), 0)
    col2 = lax.broadcasted_iota(jnp.int32, (C, 2 * C), 1)
    incl2 = row2 >= jnp.where(col2 >= C, col2 - C, col2)
    low2 = lax.broadcasted_iota(jnp.int32, (2 * C, LANES), 1) < RWKV_N
    low = lax.broadcasted_iota(jnp.int32, (C, LANES), 1) < RWKV_N
    srow = lax.broadcasted_iota(jnp.int32, (LANES, LANES), 0)
    scol = lax.broadcasted_iota(jnp.int32, (LANES, LANES), 1)
    same_head = (srow < RWKV_N) == (scol < RWKV_N)
    diag = srow == scol
    pair = lambda a, p: a[:, p * LANES:(p + 1) * LANES]

    a_ab, a_ak, a_rbk = [], [], []
    for p in range(npairs):
        art_p = pair(art, p)
        bk_p = pair(bk, p)
        for half in range(2):
            lhs = jnp.where(low2, art_p, 0.0) if half == 0 else jnp.where(low2, 0.0, art_p)
            big = _dot_nt(lhs, bk_p)
            a_ab.append(jnp.where(strict, big[:C, :C], 0.0))
            a_ak.append(jnp.where(strict, big[:C, C:], 0.0).astype(BF16))
            a_rbk.append(jnp.where(incl2, big[C:, :], 0.0).astype(BF16))

    ps = [a.astype(BF16) for a in a_ab]
    tinv = [eye + a for a in a_ab]
    n = 2
    while n < C:
        sq = [jnp.dot(pb, pb, preferred_element_type=F32) for pb in ps]
        ps = [s.astype(BF16) for s in sq]
        tinv = [t + jnp.dot(t.astype(BF16), pb, preferred_element_type=F32) for t, pb in zip(tinv, ps)]
        n *= 2
    tinv = [t.astype(BF16) for t in tinv]

    s0 = [state[p] for p in range(npairs)]
    s0b = [s.astype(BF16) for s in s0]
    sel = lambda h0, h1: jnp.where(low, h0, h1)
    mm = lambda a, b: jnp.dot(a, b, preferred_element_type=F32)
    xs = [mm(pair(at_g, p), s0b[p]) + sel(mm(a_ak[2 * p], pair(vb, p)), mm(a_ak[2 * p + 1], pair(vb, p)))
          for p in range(npairs)]
    xb = [x.astype(BF16) for x in xs]
    us = [sel(mm(tinv[2 * p], xb[p]), mm(tinv[2 * p + 1], xb[p])) for p in range(npairs)]
    uv = [jnp.concatenate([us[p].astype(BF16), pair(vb, p)], axis=0) for p in range(npairs)]
    ys = [mm(pair(rt_g, p), s0b[p]) + sel(mm(a_rbk[2 * p], uv[p]), mm(a_rbk[2 * p + 1], uv[p]))
          for p in range(npairs)]
    for p in range(npairs):
        upd = lax.dot_general(pair(bkg, p), uv[p], (((0,), (0,)), ((), ())), preferred_element_type=F32)
        g_col = jnp.sum(jnp.where(diag, pair(g_end, p), 0.0), axis=-1, keepdims=True)
        state[p] = s0[p] * g_col + jnp.where(same_head, upd, 0.0)

    inv_n = 1.0 / RWKV_N
    for p in range(npairs):
        sl = slice(p * LANES, (p + 1) * LANES)
        y = ys[p]
        mean = _dot_split(y, bd, split_lhs=True) * inv_n
        yc = y - mean
        var = _dot_split(yc * yc, bd, split_lhs=True) * inv_n
        yn = yc * lax.rsqrt(var + LNX_EPS) * lng[:, sl] + lnb[:, sl]
        bonus = _dot_split(pair(r, p) * pair(k, p) * rk[:, sl], bd, split_lhs=True) * pair(v, p)
        bb, pp = divmod(p, pairs_per_row)
        o_ref[bb, :, pp * LANES:(pp + 1) * LANES] = (yn + bonus) * gate[:, sl]


def _rwscan(r, k, v, kk, lr, ld, g, rw_r_k, rw_lnx_g, rw_lnx_b, bsz, seq, chunk):
    width = r.shape[-1]
    tri = jnp.asarray(np.tril(np.ones((chunk, chunk), np.float32))).astype(BF16)
    bd = _head_indicator(LANES)
    seq3 = lambda a: a.reshape(bsz, seq, width)
    nrows = RWSCAN_ROWS if bsz % RWSCAN_ROWS == 0 else 1
    cspec = pl.BlockSpec((nrows, chunk, width), lambda b, c: (b, c, 0))
    small = pl.BlockSpec((1, width), lambda b, c: (0, 0))
    return pl.pallas_call(
        functools.partial(_rwscan_kernel, chunk=chunk),
        grid=(bsz // nrows, seq // chunk),
        in_specs=[cspec] * 7 + [small] * 3 + [pl.BlockSpec((chunk, chunk), lambda b, c: (0, 0)),
                                              pl.BlockSpec((LANES, LANES), lambda b, c: (0, 0))],
        out_specs=cspec,
        out_shape=jax.ShapeDtypeStruct((bsz, seq, width), F32),
        scratch_shapes=[pltpu.VMEM((nrows * width // LANES, LANES, LANES), F32)],
        compiler_params=_params(("parallel", "arbitrary")),
        name="rwscan",
    )(seq3(r), seq3(k), seq3(v), seq3(kk), seq3(lr), seq3(ld), seq3(g), rw_r_k.reshape(1, width),
      rw_lnx_g[None, :], rw_lnx_b[None, :], tri, bd)


def _outproj_kernel(x_ref, mla_ref, rw_ref, gmo_ref, wo_ref, gffn_ref, wr_ref, br_ref,
                    x1_ref, comb_ref):
    tm = x_ref.shape[0]
    m = mla_ref[...]
    mn = m * lax.rsqrt(jnp.mean(m * m, axis=-1, keepdims=True) + NORM_EPS) * gmo_ref[...]
    mix = jnp.concatenate([mn.astype(BF16), rw_ref[...].astype(BF16)], axis=-1)
    acc = x_ref[...] + jnp.dot(mix, wo_ref[...], preferred_element_type=F32)
    x1_ref[...] = acc
    xn = acc * lax.rsqrt(jnp.mean(acc * acc, axis=-1, keepdims=True) + NORM_EPS) * gffn_ref[...]

    logits = _dot_bf16x3(xn, wr_ref[...]) + br_ref[...]
    lane = lax.broadcasted_iota(jnp.int32, (tm, LANES), 1)
    neg = -jnp.inf
    big = jnp.int32(1 << 20)
    is_g = (lane >= N_EXPERTS) & (lane < N_EXPERTS + N_GROUPS)
    glog = jnp.where(is_g, logits, neg)
    gmax = jnp.max(glog, axis=-1, keepdims=True)
    gsel = jnp.min(jnp.where(glog == gmax, lane, big), axis=-1, keepdims=True) - N_EXPERTS
    gprob = 1.0 / jnp.sum(jnp.exp(glog - gmax), axis=-1, keepdims=True)
    lo = gsel * EXPERTS_PER_GROUP
    in_group = (lane >= lo) & (lane < lo + EXPERTS_PER_GROUP)
    elog = jnp.where(in_group, logits, neg)
    v1 = jnp.max(elog, axis=-1, keepdims=True)
    i1 = jnp.min(jnp.where(elog == v1, lane, big), axis=-1, keepdims=True)
    elog2 = jnp.where(lane == i1, neg, elog)
    v2 = jnp.max(elog2, axis=-1, keepdims=True)
    i2 = jnp.min(jnp.where(elog2 == v2, lane, big), axis=-1, keepdims=True)
    e2 = jnp.exp(v2 - v1)
    w1 = gprob / (1.0 + e2)
    w2 = gprob * e2 / (1.0 + e2)
    comb_ref[...] = jnp.where(lane == 0, i1.astype(F32), jnp.where(lane == 1, i2.astype(F32),
                              jnp.where(lane == 2, w1, jnp.where(lane == 3, w2, 0.0))))


def _outproj(x2, mla, rw, g_mla_out, w_o, g_ffn, w_group, b_group, w_expert, b_expert, seq, tm):
    t = x2.shape[0]
    wo = w_o.astype(BF16)
    pad = LANES - N_EXPERTS - N_GROUPS
    wr = jnp.concatenate([w_expert, w_group, jnp.zeros((D_MODEL, pad), F32)], axis=-1)
    br = jnp.concatenate([b_expert, b_group, jnp.zeros((pad,), F32)])[None, :]
    gmo = g_mla_out[None, :]
    full = lambda a: pl.BlockSpec(a.shape, lambda i: (0,) * a.ndim)
    row = pl.BlockSpec((tm, D_MODEL), lambda i: (i, 0))
    return pl.pallas_call(
        _outproj_kernel,
        grid=(t // tm,),
        in_specs=[row,
                  pl.BlockSpec((tm, MLA_W), lambda i: (i, 0)),
                  pl.BlockSpec((tm, RWKV_W), lambda i: (i, 0)),
                  full(gmo), full(wo), pl.BlockSpec((1, D_MODEL), lambda i: (0, 0)), full(wr), full(br)],
        out_specs=[row, pl.BlockSpec((tm, LANES), lambda i: (i, 0))],
        out_shape=[jax.ShapeDtypeStruct((t, D_MODEL), F32),
                   jax.ShapeDtypeStruct((t, LANES), F32)],
        compiler_params=_params(("parallel",)),
        name="outproj",
    )(x2, mla, rw, gmo, wo, g_ffn[None, :], wr, br)


def _moeplan_kernel(route_ref, tri_ref, triu_ref, posw_ref, nb_ref, roff_ref):
    rt = route_ref[...]
    tb = rt.shape[0]
    lane = lax.broadcasted_iota(jnp.int32, (tb, LANES), 1)
    oh1 = lane == rt[:, 0:1].astype(jnp.int32)
    oh2 = lane == rt[:, 1:2].astype(jnp.int32)
    oh = jnp.where(oh1, 1.0, jnp.where(oh2, 1.0, 0.0))
    excl = jnp.dot(tri_ref[...], oh.astype(BF16), preferred_element_type=F32)
    cnt = jnp.sum(oh, axis=0, keepdims=True)
    nb = jnp.floor((cnt + (MOE_RB - 0.5)) * (1.0 / MOE_RB))
    boff = jnp.dot(jnp.broadcast_to(nb, (8, LANES)).astype(BF16), triu_ref[...],
                   preferred_element_type=F32)[0:1]
    roff = boff * MOE_RB
    base = roff + excl
    pos1 = jnp.sum(jnp.where(oh1, base, 0.0), axis=-1, keepdims=True)
    pos2 = jnp.sum(jnp.where(oh2, base, 0.0), axis=-1, keepdims=True)
    posw_ref[...] = jnp.where(lane == 0, pos1, jnp.where(lane == 1, pos2, rt))
    nb_ref[0] = nb.astype(jnp.int32)
    roff_ref[0] = roff.astype(jnp.int32)


def _moeplan(route, tb):
    t = route.shape[0]
    nt = t // tb
    tri = jnp.asarray(np.tril(np.ones((tb, tb), np.float32), -1)).astype(BF16)
    triu = jnp.asarray(np.triu(np.ones((LANES, LANES), np.float32), 1)).astype(BF16)
    small = pl.BlockSpec((1, 1, LANES), lambda i: (i, 0, 0))
    return pl.pallas_call(
        _moeplan_kernel,
        grid=(nt,),
        in_specs=[pl.BlockSpec((tb, LANES), lambda i: (i, 0)),
                  pl.BlockSpec((tb, tb), lambda i: (0, 0)),
                  pl.BlockSpec((LANES, LANES), lambda i: (0, 0))],
        out_specs=[pl.BlockSpec((tb, LANES), lambda i: (i, 0)), small, small],
        out_shape=[jax.ShapeDtypeStruct((t, LANES), F32),
                   jax.ShapeDtypeStruct((nt, 1, LANES), jnp.int32),
                   jax.ShapeDtypeStruct((nt, 1, LANES), jnp.int32)],
        compiler_params=_params(("parallel",)),
        name="moeplan",
    )(route, tri, triu)


def _moe_kernel(pos_ref, nb_ref, roff_ref, x1_ref, posw_ref, gffn_ref, wg_hbm, wu_hbm, wd_hbm, o_ref,
                xn, xs, g1, g2, wg_buf, wu_buf, wd_buf, sem):
    i = pl.program_id(0)
    nt = pl.num_programs(0)
    tb = x1_ref.shape[0]
    chunk = g1.shape[0]
    dummy_r0 = xs.shape[0] - MOE_RB
    group = MOE_EXPERTS_PER_STEP
    ngroups = N_EXPERTS // group

    def weight_copies(e, slot):
        return (pltpu.make_async_copy(wg_hbm.at[e], wg_buf.at[slot], sem.at[slot, 0]),
                pltpu.make_async_copy(wu_hbm.at[e], wu_buf.at[slot], sem.at[slot, 1]),
                pltpu.make_async_copy(wd_hbm.at[e], wd_buf.at[slot], sem.at[slot, 2]))

    ahead = MOE_WEIGHT_BUFS - 1

    def start_group(n):
        first = lax.rem(n, ngroups) * group
        ring = lax.rem(n, MOE_WEIGHT_BUFS) * group
        for j in range(group):
            for c in weight_copies(first + j, ring + j):
                c.start()

    @pl.when(i == 0)
    def _():
        for n in range(ahead):
            start_group(n)
        xs[...] = jnp.zeros_like(xs)

    for c in range(tb // chunk):
        sl = slice(c * chunk, (c + 1) * chunk)
        x1 = x1_ref[sl, :]
        xn[sl, :] = x1 * lax.rsqrt(jnp.mean(x1 * x1, axis=-1, keepdims=True) + NORM_EPS) * gffn_ref[...]

    def put(tok, carry):
        row = xn[pl.ds(tok, 1), :]
        xs[pl.ds(pos_ref[2 * tok], 1), :] = row
        xs[pl.ds(pos_ref[2 * tok + 1], 1), :] = row
        return carry

    lax.fori_loop(0, tb, put, 0, unroll=8)

    def expert_group(p, carry):
        n = i * ngroups + p
        base = lax.rem(n, MOE_WEIGHT_BUFS) * group

        @pl.when(n + ahead < nt * ngroups)
        def _():
            start_group(n + ahead)

        for j in range(group):
            for c in weight_copies(p * group + j, base + j):
                c.wait()
        nbs = [nb_ref[0, 0, p * group + j] for j in range(group)]
        roffs = [roff_ref[0, 0, p * group + j] for j in range(group)]
        trips = functools.reduce(jnp.maximum, nbs)

        def block(b, c2):
            r0s = [pl.multiple_of(jnp.where(b < nbs[j], roffs[j] + b * MOE_RB, dummy_r0), 8)
                   for j in range(group)]
            x = [xs[pl.ds(r0, MOE_RB), :].astype(BF16) for r0 in r0s]
            hg = [jnp.dot(x[j], wg_buf[base + j], preferred_element_type=F32) for j in range(group)]
            hu = [jnp.dot(x[j], wu_buf[base + j], preferred_element_type=F32) for j in range(group)]
            hid = [(hg[j] * jax.nn.sigmoid(hg[j]) * hu[j]).astype(BF16) for j in range(group)]
            y = [jnp.dot(hid[j], wd_buf[base + j], preferred_element_type=F32) for j in range(group)]
            for j in range(group):
                xs[pl.ds(r0s[j], MOE_RB), :] = y[j]
            return c2

        lax.fori_loop(0, trips, block, 0)
        return carry

    lax.fori_loop(0, ngroups, expert_group, 0)

    for c in range(tb // chunk):
        def take(j, carry, c=c):
            tok = c * chunk + j
            g1[pl.ds(j, 1), :] = xs[pl.ds(pos_ref[2 * tok], 1), :]
            g2[pl.ds(j, 1), :] = xs[pl.ds(pos_ref[2 * tok + 1], 1), :]
            return carry

        lax.fori_loop(0, chunk, take, 0, unroll=8)
        sl = slice(c * chunk, (c + 1) * chunk)
        o_ref[sl, :] = x1_ref[sl, :] + (posw_ref[sl, 2:3] * g1[...] + posw_ref[sl, 3:4] * g2[...])


def _moe(x1, route, g_ffn, w_gate, w_up, w_down, tb):
    t = x1.shape[0]
    nt = t // tb
    chunk = min(tb, 256)
    rows = 2 * tb + N_EXPERTS * MOE_RB + MOE_RB
    nslot = MOE_WEIGHT_BUFS * MOE_EXPERTS_PER_STEP
    posw, nb, roff = _moeplan(route, tb)
    pos = posw[:, 0:2].astype(jnp.int32).reshape(2 * t)
    smem3 = lambda: pl.BlockSpec((1, 1, LANES), lambda i: (i, 0, 0), memory_space=pltpu.SMEM)
    hbm = lambda: pl.BlockSpec(memory_space=pl.ANY)
    return pl.pallas_call(
        _moe_kernel,
        grid=(nt,),
        in_specs=[pl.BlockSpec((2 * tb,), lambda i: (i,), memory_space=pltpu.SMEM), smem3(), smem3(),
                  pl.BlockSpec((tb, D_MODEL), lambda i: (i, 0)),
                  pl.BlockSpec((tb, LANES), lambda i: (i, 0)),
                  pl.BlockSpec((1, D_MODEL), lambda i: (0, 0)),
                  hbm(), hbm(), hbm()],
        out_specs=pl.BlockSpec((tb, D_MODEL), lambda i: (i, 0)),
        out_shape=jax.ShapeDtypeStruct((t, D_MODEL), F32),
        scratch_shapes=[pltpu.VMEM((tb, D_MODEL), F32), pltpu.VMEM((rows, D_MODEL), F32),
                        pltpu.VMEM((chunk, D_MODEL), F32), pltpu.VMEM((chunk, D_MODEL), F32),
                        pltpu.VMEM((nslot, D_MODEL, D_EXPERT), BF16), pltpu.VMEM((nslot, D_MODEL, D_EXPERT), BF16),
                        pltpu.VMEM((nslot, D_EXPERT, D_MODEL), BF16), pltpu.SemaphoreType.DMA((nslot, 3))],
        compiler_params=_params(("arbitrary",)),
        name="moe",
    )(pos, nb, roff, x1, posw, g_ffn[None, :], w_gate.astype(BF16), w_up.astype(BF16), w_down.astype(BF16))


def _tile(seq, want):
    tm = min(seq, want)
    assert seq % tm == 0 and tm % 8 == 0
    return tm


def _layer(x, positions, g_mix, w_in, g_cq, w_uq, g_ckv, w_uk, w_uv, g_qn, g_kn, g_mla_out,
           rw_mu, rw_w0, rw_w2, rw_a0, rw_a2, rw_g2, rw_k_k, rw_k_a, rw_r_k, rw_lnx_g, rw_lnx_b,
           w_o, g_ffn, w_group, b_group, w_expert, b_expert, w_gate, w_up, w_down):
    bsz, seq, d = x.shape
    t = bsz * seq
    x2 = x.reshape(t, d)
    pos2 = positions.reshape(t, 1)
    kv_blk = _tile(seq, KV_BLK)
    feat, q, k, v = _inproj(x2, pos2, g_mix, w_in, g_cq, w_uq, g_ckv, w_uk, w_uv, g_qn, g_kn,
                            bsz, seq, _tile(kv_blk, 512), kv_blk)
    mla = _attn(q, k, v).reshape(t, MLA_W)
    r, k2, vv, kk, lr, ld, g = _rwprep(feat, rw_mu, rw_w0, rw_w2, rw_a0, rw_a2, rw_g2, rw_k_k, rw_k_a,
                                       bsz, seq, _tile(seq, 512))
    rw = _rwscan(r, k2, vv, kk, lr, ld, g, rw_r_k, rw_lnx_g, rw_lnx_b, bsz, seq, _tile(seq, 128))
    rw = rw.reshape(t, RWKV_W)
    x1, route = _outproj(x2, mla, rw, g_mla_out, w_o, g_ffn, w_group, b_group, w_expert, b_expert,
                         seq, _tile(seq, 512))
    out = _moe(x1, route, g_ffn, w_gate, w_up, w_down, _tile(t, MOE_TB))
    return out.reshape(bsz, seq, d)


def kernel(x, positions, g_mix, w_in, g_cq, w_uq, g_ckv, w_uk, w_uv, g_qn, g_kn, g_mla_out, rw_mu, rw_w0, rw_w2, rw_a0, rw_a2, rw_g2, rw_k_k, rw_k_a, rw_r_k, rw_lnx_g, rw_lnx_b, w_o, g_ffn, w_group, b_group, w_expert, b_expert, w_gate, w_up, w_down):
    for l in range(g_mix.shape[0]):
        x = _layer(x, positions, g_mix[l], w_in[l], g_cq[l], w_uq[l], g_ckv[l], w_uk[l], w_uv[l],
                   g_qn[l], g_kn[l], g_mla_out[l], rw_mu[l], rw_w0[l], rw_w2[l], rw_a0[l], rw_a2[l],
                   rw_g2[l], rw_k_k[l], rw_k_a[l], rw_r_k[l], rw_lnx_g[l], rw_lnx_b[l], w_o[l],
                   g_ffn[l], w_group[l], b_group[l], w_expert[l], b_expert[l], w_gate[l], w_up[l],
                   w_down[l])
    return x
```

```python
import functools

import numpy as np
import jax
import jax.numpy as jnp
from jax import lax
from jax.experimental import pallas as pl
from jax.experimental.pallas import tpu as pltpu

F32 = jnp.float32
BF16 = jnp.bfloat16

D_MODEL = 1024
MLA_HEADS = 8
MLA_NOPE = 64
MLA_ROPE = 32
MLA_QK = MLA_NOPE + MLA_ROPE
MLA_V = 64
MLA_W = MLA_HEADS * MLA_V
MLA_Q_RANK = 256
MLA_KV_RANK = 128
ROPE_THETA = 10000.0
RWKV_HEADS = 8
RWKV_N = 64
RWKV_W = RWKV_HEADS * RWKV_N
DECAY_LORA = 64
AAA_LORA = 64
GATE_LORA = 128
MLA_IN = MLA_Q_RANK + MLA_KV_RANK + MLA_ROPE
RWKV_IN = 3 * RWKV_W + DECAY_LORA + AAA_LORA + GATE_LORA
N_GROUPS = 4
EXPERTS_PER_GROUP = 8
N_EXPERTS = N_GROUPS * EXPERTS_PER_GROUP
D_EXPERT = 256
NORM_EPS = 1e-6
LNX_EPS = 64e-5

LANES = 128
QK_PAD = LANES
KV_BLK = 512
ATT_HEADS = 4
RWSCAN_ROWS = 4
MOE_TB = 1024
MOE_RB = 80
MOE_EXPERTS_PER_STEP = 2
MOE_WEIGHT_BUFS = 4
VT_ROWS = MLA_V + 16
Q_SCALE = MLA_QK ** -0.5 * float(np.log2(np.e))
PM_W = MLA_Q_RANK + MLA_KV_RANK + 2 * LANES
VMEM_LIMIT = 56 * 1024 * 1024


def _dot_bf16x3(a, b):
    a_hi = a.astype(BF16)
    a_lo = (a - a_hi.astype(F32)).astype(BF16)
    b_hi = b.astype(BF16)
    b_lo = (b - b_hi.astype(F32)).astype(BF16)
    d = lambda p, q: jnp.dot(p, q, preferred_element_type=F32)
    return d(a_hi, b_hi) + (d(a_hi, b_lo) + d(a_lo, b_hi))


def _dot(a, b):
    return jnp.dot(a.astype(BF16), b.astype(BF16), preferred_element_type=F32)


def _dot_nt(a, b):
    return lax.dot_general(a.astype(BF16), b.astype(BF16), (((1,), (1,)), ((), ())),
                           preferred_element_type=F32)


def _dot_tn(a, b):
    return lax.dot_general(a.astype(BF16), b.astype(BF16), (((0,), (0,)), ((), ())),
                           preferred_element_type=F32)


def _dot_split(a, b, split_lhs, passes=2):
    s = a if split_lhs else b
    out = None
    for _ in range(passes):
        piece = s.astype(BF16)
        term = (jnp.dot(piece, b, preferred_element_type=F32) if split_lhs
                else jnp.dot(a, piece, preferred_element_type=F32))
        out = term if out is None else out + term
        s = s - piece.astype(F32)
    return out


def _head_indicator(width):
    hid = np.arange(width) // RWKV_N
    return jnp.asarray((hid[:, None] == hid[None, :]).astype(np.float32)).astype(BF16)


def _params(sem):
    return pltpu.CompilerParams(dimension_semantics=sem, vmem_limit_bytes=VMEM_LIMIT)


def _inproj_kernel(x_ref, pos_ref, gmix_ref, wmla_ref, wrw_ref, freq_ref, gcq_ref, gckv_ref,
                   wq_ref, wqr_ref, wk_ref, wv_ref, gq_ref, gqr_ref, gk_ref, gkr_ref,
                   feat_ref, q_ref, k_ref, v_ref):
    x = x_ref[...]
    h = x * lax.rsqrt(jnp.mean(x * x, axis=-1, keepdims=True) + NORM_EPS) * gmix_ref[...]
    hb = h.astype(BF16)
    feat_ref[...] = jnp.dot(hb, wrw_ref[...], preferred_element_type=F32)
    pm = jnp.dot(hb, wmla_ref[...], preferred_element_type=F32)

    c_q = pm[:, :MLA_Q_RANK]
    c_kv = pm[:, MLA_Q_RANK:MLA_Q_RANK + MLA_KV_RANK]
    kr = pm[:, MLA_Q_RANK + MLA_KV_RANK:MLA_Q_RANK + MLA_KV_RANK + LANES]
    kr_rot = pm[:, MLA_Q_RANK + MLA_KV_RANK + LANES:PM_W]
    cqn = (c_q * lax.rsqrt(jnp.mean(c_q * c_q, axis=-1, keepdims=True) + NORM_EPS)
           * gcq_ref[...]).astype(BF16)
    ckvn = (c_kv * lax.rsqrt(jnp.mean(c_kv * c_kv, axis=-1, keepdims=True) + NORM_EPS)
            * gckv_ref[...]).astype(BF16)

    ang = pos_ref[...].astype(F32) * freq_ref[...]
    cosf = jnp.cos(ang)
    sinf = jnp.sin(ang)
    q_cos = gq_ref[...] * cosf * Q_SCALE
    q_sin = gqr_ref[...] * sinf * Q_SCALE
    k_cos = gk_ref[...] * cosf
    k_sin = kr_rot * (gkr_ref[...] * sinf)
    inv_qk = 1.0 / MLA_QK
    ones_rows = jnp.where(lax.broadcasted_iota(jnp.int32, (VT_ROWS - MLA_V, x.shape[0]), 0) == 0, 1.0, 0.0)

    for hh in range(MLA_HEADS):
        q_raw = jnp.dot(cqn, wq_ref[hh], preferred_element_type=F32)
        q_rot = jnp.dot(cqn, wqr_ref[hh], preferred_element_type=F32)
        sq = lax.rsqrt(jnp.sum(q_raw * q_raw, axis=-1, keepdims=True) * inv_qk + NORM_EPS)
        q_ref[0, hh] = (sq * (q_raw * q_cos + q_rot * q_sin)).astype(BF16)
        k_raw = jnp.dot(ckvn, wk_ref[hh], preferred_element_type=F32) + kr
        sk = lax.rsqrt(jnp.sum(k_raw * k_raw, axis=-1, keepdims=True) * inv_qk + NORM_EPS)
        k_ref[0, hh] = (sk * (k_raw * k_cos + k_sin)).astype(BF16)
        vt = lax.dot_general(wv_ref[hh], ckvn, (((1,), (1,)), ((), ())), preferred_element_type=F32)
        v_ref[0, hh, 0] = jnp.concatenate([vt, ones_rows], axis=0).astype(BF16)


def _rot_cols(w):
    half = MLA_ROPE // 2
    z = jnp.zeros_like(w[..., :MLA_NOPE])
    return jnp.concatenate([z, -w[..., MLA_NOPE + half:MLA_QK], w[..., MLA_NOPE:MLA_NOPE + half]], axis=-1)


def _pad_last(w, n):
    return jnp.pad(w, [(0, 0)] * (w.ndim - 1) + [(0, n - w.shape[-1])])


def _inproj(x2, pos2, g_mix, w_in, g_cq, w_uq, g_ckv, w_uk, w_uv, g_qn, g_kn, bsz, seq, tm, kv_blk):
    t = x2.shape[0]
    nt = seq // tm
    per_kv = kv_blk // tm
    half = MLA_ROPE // 2
    w_kr = w_in[:, MLA_Q_RANK + MLA_KV_RANK:MLA_IN]
    zeros64 = jnp.zeros((D_MODEL, MLA_NOPE), F32)
    zeros32 = jnp.zeros((D_MODEL, LANES - MLA_QK), F32)
    w_kr_rot = jnp.concatenate([-w_kr[:, half:], w_kr[:, :half]], axis=-1)
    w_mla = jnp.concatenate([w_in[:, :MLA_Q_RANK + MLA_KV_RANK], zeros64, w_kr, zeros32,
                             zeros64, w_kr_rot, zeros32], axis=-1).astype(BF16)
    w_rw = w_in[:, MLA_IN:].astype(BF16)
    wq = jnp.transpose(w_uq, (1, 0, 2))
    wq_rot = _pad_last(_rot_cols(wq), QK_PAD).astype(BF16)
    wq = _pad_last(wq, QK_PAD).astype(BF16)
    wk = _pad_last(jnp.transpose(w_uk, (1, 0, 2)), QK_PAD).astype(BF16)
    wv = jnp.transpose(w_uv, (1, 2, 0)).astype(BF16)
    inv_freq = 1.0 / (ROPE_THETA ** (jnp.arange(half, dtype=F32) / half))
    freq = jnp.concatenate([jnp.zeros((MLA_NOPE,), F32), inv_freq, inv_freq,
                            jnp.zeros((LANES - MLA_QK,), F32)])[None, :]
    swap = lambda g: jnp.concatenate([jnp.zeros((MLA_NOPE,), F32), g[MLA_NOPE + half:MLA_QK],
                                      g[MLA_NOPE:MLA_NOPE + half], jnp.zeros((LANES - MLA_QK,), F32)])[None, :]
    gq = _pad_last(g_qn[None, :], QK_PAD)
    gk = _pad_last(g_kn[None, :], QK_PAD)
    gq_rot = swap(g_qn)
    gk_rot = swap(g_kn)

    full = lambda a: pl.BlockSpec(a.shape, lambda i: (0,) * a.ndim)
    head_map = lambda i: (i // nt, 0, i % nt, 0)
    args = (x2, pos2, g_mix[None, :], w_mla, w_rw, freq, g_cq[None, :], g_ckv[None, :],
            wq, wq_rot, wk, wv, gq, gq_rot, gk, gk_rot)
    in_specs = [pl.BlockSpec((tm, D_MODEL), lambda i: (i, 0)),
                pl.BlockSpec((tm, 1), lambda i: (i, 0))] + [full(a) for a in args[2:]]
    return pl.pallas_call(
        _inproj_kernel,
        grid=(t // tm,),
        in_specs=in_specs,
        out_specs=[pl.BlockSpec((tm, RWKV_IN), lambda i: (i, 0)),
                   pl.BlockSpec((1, MLA_HEADS, tm, QK_PAD), head_map),
                   pl.BlockSpec((1, MLA_HEADS, tm, QK_PAD), head_map),
                   pl.BlockSpec((1, MLA_HEADS, 1, VT_ROWS, tm),
                                lambda i: (i // nt, 0, (i % nt) // per_kv, 0, (i % nt) % per_kv))],
        out_shape=[jax.ShapeDtypeStruct((t, RWKV_IN), F32),
                   jax.ShapeDtypeStruct((bsz, MLA_HEADS, seq, QK_PAD), BF16),
                   jax.ShapeDtypeStruct((bsz, MLA_HEADS, seq, QK_PAD), BF16),
                   jax.ShapeDtypeStruct((bsz, MLA_HEADS, seq // kv_blk, VT_ROWS, kv_blk), BF16)],
        compiler_params=_params(("parallel",)),
        name="inproj",
    )(*args)


def _attn_kernel(q_ref, k_ref, vt_ref, o_ref, s_scr):
    t = pl.program_id(2)
    nh = q_ref.shape[1]
    kvb = vt_ref.shape[4]
    qb = kvb

    def q_tile(w):
        return [q_ref[0, h, w * qb:(w + 1) * qb, :] for h in range(nh)]

    def scores_to(qs, g, slot):
        start = pl.multiple_of(g * kvb, kvb)
        for h in range(nh):
            s_scr[slot, h] = lax.dot_general(k_ref[0, h, pl.ds(start, kvb), :], qs[h], (((1,), (1,)), ((), ())),
                                             preferred_element_type=F32)

    def consume(qi, g, slot, carry, masked):
        ss = [s_scr[slot, h] for h in range(nh)]
        if masked:
            key = g * kvb + lax.broadcasted_iota(jnp.int32, (kvb, qb), 0)
            qry = qi * qb + lax.broadcasted_iota(jnp.int32, (kvb, qb), 1)
            ss = [jnp.where(key <= qry, s, -jnp.inf) for s in ss]
        ms = [jnp.maximum(carry[h][0], jnp.max(ss[h], axis=0, keepdims=True)) for h in range(nh)]
        ps = [jnp.exp2((ss[h] - ms[h]).astype(BF16)) for h in range(nh)]
        out = []
        for h in range(nh):
            m, acc = carry[h]
            alpha = jnp.exp2(m - ms[h])
            out.append((ms[h], alpha * acc + jnp.dot(vt_ref[0, h, g], ps[h], preferred_element_type=F32)))
        return tuple(out)

    def trips(qs, qi, cur, nxt, carry):
        def two_groups(j, c):
            scores_to(qs, 2 * j + 1, nxt)
            c = consume(qi, 2 * j, cur, c, False)
            scores_to(qs, 2 * j + 2, cur)
            return consume(qi, 2 * j + 1, nxt, c, False)

        return lax.fori_loop(0, t, two_groups, carry)

    def finish(carry, w):
        ot = jnp.concatenate([acc[:MLA_V] * (1.0 / acc[MLA_V:MLA_V + 1]) for _, acc in carry], axis=0)
        o_ref[0, w * qb:(w + 1) * qb, :] = ot.T

    init = tuple((jnp.full((1, qb), -1e30, F32), jnp.zeros((VT_ROWS, qb), F32)) for _ in range(nh))
    q_a, q_b = q_tile(0), q_tile(1)
    tile_a, tile_b = 2 * t, 2 * t + 1

    scores_to(q_a, 0, 0)
    carry = trips(q_a, tile_a, 0, 1, init)
    scores_to(q_b, 0, 1)
    finish(consume(tile_a, 2 * t, 0, carry, True), 0)
    carry = trips(q_b, tile_b, 1, 0, init)
    scores_to(q_b, 2 * t + 1, 0)
    carry = consume(tile_b, 2 * t, 1, carry, False)
    finish(consume(tile_b, 2 * t + 1, 0, carry, True), 1)


def _attn(q, k, vt):
    bsz, nh, seq, _ = q.shape
    per_step = ATT_HEADS
    ngrp, kvb = vt.shape[2], vt.shape[4]
    assert seq % (2 * kvb) == 0
    return pl.pallas_call(
        _attn_kernel,
        grid=(bsz, nh // per_step, seq // (2 * kvb)),
        in_specs=[pl.BlockSpec((1, per_step, 2 * kvb, QK_PAD), lambda b, h, i: (b, h, i, 0)),
                  pl.BlockSpec((1, per_step, seq, QK_PAD), lambda b, h, i: (b, h, 0, 0)),
                  pl.BlockSpec((1, per_step, ngrp, VT_ROWS, kvb), lambda b, h, i: (b, h, 0, 0, 0))],
        out_specs=pl.BlockSpec((1, 2 * kvb, per_step * MLA_V), lambda b, h, i: (b, i, h)),
        out_shape=jax.ShapeDtypeStruct((bsz, seq, nh * MLA_V), F32),
        scratch_shapes=[pltpu.VMEM((2, per_step, kvb, kvb), F32)],
        compiler_params=_params(("parallel", "parallel", "arbitrary")),
        name="attn",
    )(q, k, vt)


def _rwprep_kernel(feat_ref, prev_ref, mu_ref, w0_ref, w2_ref, a0_ref, a2_ref, g2_ref, kk_ref, ka_ref,
                   bd_ref, r_o, k_o, v_o, kk_o, lr_o, ld_o, g_o, *, nt):
    i = pl.program_id(0)
    feat = feat_ref[...]
    tm = feat.shape[0]
    first = jnp.where(i % nt == 0, 0.0, 1.0)
    prev_row = prev_ref[7:8, :] * first
    rolled = pltpu.roll(feat, 1, axis=0)
    rid = lax.broadcasted_iota(jnp.int32, (tm, 1), 0)
    prev = jnp.where(rid == 0, prev_row, rolled)
    feat = feat + (prev - feat) * mu_ref[...]

    r = feat[:, :RWKV_W]
    k = feat[:, RWKV_W:2 * RWKV_W]
    v = feat[:, 2 * RWKV_W:3 * RWKV_W]
    wa = feat[:, 3 * RWKV_W:3 * RWKV_W + LANES]
    gl = feat[:, 3 * RWKV_W + LANES:]
    lane = lax.broadcasted_iota(jnp.int32, (tm, LANES), 1)
    wl_in = jnp.where(lane < DECAY_LORA, jnp.tanh(wa), 0.0)
    al_in = jnp.where(lane < DECAY_LORA, 0.0, wa)
    wpre = w0_ref[...] + _dot(wl_in, w2_ref[...])
    w = -(jnp.maximum(-wpre, 0.0) + jnp.log(1.0 + jnp.exp(-jnp.abs(wpre)))) - 0.5
    logd = -jnp.exp(w)
    a = jax.nn.sigmoid(a0_ref[...] + _dot(al_in, a2_ref[...]))
    g = _dot(jax.nn.sigmoid(gl), g2_ref[...])
    kk = k * kk_ref[...]
    ss = _dot_split(kk * kk, bd_ref[...], split_lhs=True)
    r_o[...] = r
    k_o[...] = k * (1.0 + (a - 1.0) * ka_ref[...])
    v_o[...] = v
    kk_o[...] = kk * lax.rsqrt(ss + 1e-12)
    lr_o[...] = a
    ld_o[...] = logd
    g_o[...] = g


def _rwprep(feat, rw_mu, rw_w0, rw_w2, rw_a0, rw_a2, rw_g2, rw_k_k, rw_k_a, bsz, seq, tm):
    t = feat.shape[0]
    nt = seq // tm
    w2p = jnp.concatenate([rw_w2, jnp.zeros((LANES - DECAY_LORA, RWKV_W), F32)], axis=0).astype(BF16)
    a2p = jnp.concatenate([jnp.zeros((DECAY_LORA, RWKV_W), F32), rw_a2], axis=0).astype(BF16)
    args = (feat, feat, rw_mu[None, :], rw_w0[None, :], w2p, rw_a0[None, :], a2p, rw_g2.astype(BF16),
            rw_k_k[None, :], rw_k_a[None, :], _head_indicator(RWKV_W))
    full = lambda a: pl.BlockSpec(a.shape, lambda i: (0,) * a.ndim)
    rows8 = tm // 8
    in_specs = [pl.BlockSpec((tm, RWKV_IN), lambda i: (i, 0)),
                pl.BlockSpec((8, RWKV_IN), lambda i: (jnp.maximum(i * rows8 - 1, 0), 0))]
    in_specs += [full(a) for a in args[2:]]
    hshape = jax.ShapeDtypeStruct((t, RWKV_W), F32)
    hspec = pl.BlockSpec((tm, RWKV_W), lambda i: (i, 0))
    return pl.pallas_call(
        functools.partial(_rwprep_kernel, nt=nt),
        grid=(t // tm,),
        in_specs=in_specs,
        out_specs=[hspec] * 7,
        out_shape=[hshape] * 7,
        compiler_params=_params(("parallel",)),
        name="rwprep",
    )(*args)


def _rwscan_kernel(r_ref, k_ref, v_ref, kk_ref, lr_ref, ld_ref, g_ref, rk_ref, lng_ref, lnb_ref,
                   tri_ref, bd_ref, o_ref, state, *, chunk):
    c = pl.program_id(1)

    @pl.when(c == 0)
    def _():
        state[...] = jnp.zeros_like(state)

    C = chunk
    nrows = r_ref.shape[0]
    pairs_per_row = RWKV_W // LANES
    npairs = nrows * pairs_per_row
    wide = lambda ref: jnp.concatenate([ref[b] for b in range(nrows)], axis=-1)
    tiled = lambda ref: jnp.concatenate([ref[...]] * nrows, axis=-1)
    r = wide(r_ref)
    k = wide(k_ref)
    v = wide(v_ref)
    kk = wide(kk_ref)
    ld = wide(ld_ref)
    lr = wide(lr_ref)
    gate = wide(g_ref)
    rk = tiled(rk_ref)
    lng = tiled(lng_ref)
    lnb = tiled(lnb_ref)
    bd = bd_ref[...]

    cum = _dot_split(tri_ref[...], ld, split_lhs=False)
    mid = C // 2 - 1
    cmid = cum[mid:mid + 1, :]
    cend = cum[C - 1:C, :]
    e_pos = jnp.exp(cum - cmid)
    e_prev = jnp.exp(cum - ld - cmid)
    e_neg = jnp.exp(cmid - cum)
    g_mid = jnp.exp(cmid)
    g_end_rel = jnp.exp(cend - cmid)
    g_end = jnp.exp(cend)
    rt = r * e_pos
    at = -kk * e_prev
    bt = kk * lr * e_neg
    kt = k * e_neg
    at_g = (at * g_mid).astype(BF16)
    rt_g = (rt * g_mid).astype(BF16)
    art = jnp.concatenate([at, rt], axis=0)
    bk = jnp.concatenate([bt, kt], axis=0).astype(BF16)
    bkg = jnp.concatenate([bt * g_end_rel, kt * g_end_rel], axis=0).astype(BF16)
    vb = v.astype(BF16)

    row = lax.broadcasted_iota(jnp.int32, (C, C), 0)
    col = lax.broadcasted_iota(jnp.int32, (C, C), 1)
    strict = row > col
    eye = (row == col).astype(F32)
    row2 = lax.broadcasted_iota(jnp.int32, (C, 2 * C), 0)
    col2 = lax.broadcasted_iota(jnp.int32, (C, 2 * C), 1)
    incl2 = row2 >= jnp.where(col2 >= C, col2 - C, col2)
    low2 = lax.broadcasted_iota(jnp.int32, (2 * C, LANES), 1) < RWKV_N
    low = lax.broadcasted_iota(jnp.int32, (C, LANES), 1) < RWKV_N
    srow = lax.broadcasted_iota(jnp.int32, (LANES, LANES), 0)
    scol = lax.broadcasted_iota(jnp.int32, (LANES, LANES), 1)
    same_head = (srow < RWKV_N) == (scol < RWKV_N)
    diag = srow == scol
    pair = lambda a, p: a[:, p * LANES:(p + 1) * LANES]

    a_ab, a_ak, a_rbk = [], [], []
    for p in range(npairs):
        art_p = pair(art, p)
        bk_p = pair(bk, p)
        for half in range(2):
            lhs = jnp.where(low2, art_p, 0.0) if half == 0 else jnp.where(low2, 0.0, art_p)
            big = _dot_nt(lhs, bk_p)
            a_ab.append(jnp.where(strict, big[:C, :C], 0.0))
            a_ak.append(jnp.where(strict, big[:C, C:], 0.0).astype(BF16))
            a_rbk.append(jnp.where(incl2, big[C:, :], 0.0).astype(BF16))

    ps = [a.astype(BF16) for a in a_ab]
    tinv = [eye + a for a in a_ab]
    n = 2
    while n < C:
        sq = [jnp.dot(pb, pb, preferred_element_type=F32) for pb in ps]
        ps = [s.astype(BF16) for s in sq]
        tinv = [t + jnp.dot(t.astype(BF16), pb, preferred_element_type=F32) for t, pb in zip(tinv, ps)]
        n *= 2
    tinv = [t.astype(BF16) for t in tinv]

    s0 = [state[p] for p in range(npairs)]
    s0b = [s.astype(BF16) for s in s0]
    sel = lambda h0, h1: jnp.where(low, h0, h1)
    mm = lambda a, b: jnp.dot(a, b, preferred_element_type=F32)
    xs = [mm(pair(at_g, p), s0b[p]) + sel(mm(a_ak[2 * p], pair(vb, p)), mm(a_ak[2 * p + 1], pair(vb, p)))
          for p in range(npairs)]
    xb = [x.astype(BF16) for x in xs]
    us = [sel(mm(tinv[2 * p], xb[p]), mm(tinv[2 * p + 1], xb[p])) for p in range(npairs)]
    uv = [jnp.concatenate([us[p].astype(BF16), pair(vb, p)], axis=0) for p in range(npairs)]
    ys = [mm(pair(rt_g, p), s0b[p]) + sel(mm(a_rbk[2 * p], uv[p]), mm(a_rbk[2 * p + 1], uv[p]))
          for p in range(npairs)]
    for p in range(npairs):
        upd = lax.dot_general(pair(bkg, p), uv[p], (((0,), (0,)), ((), ())), preferred_element_type=F32)
        g_col = jnp.sum(jnp.where(diag, pair(g_end, p), 0.0), axis=-1, keepdims=True)
        state[p] = s0[p] * g_col + jnp.where(same_head, upd, 0.0)

    inv_n = 1.0 / RWKV_N
    for p in range(npairs):
        sl = slice(p * LANES, (p + 1) * LANES)
        y = ys[p]
        mean = _dot_split(y, bd, split_lhs=True) * inv_n
        yc = y - mean
        var = _dot_split(yc * yc, bd, split_lhs=True) * inv_n
        yn = yc * lax.rsqrt(var + LNX_EPS) * lng[:, sl] + lnb[:, sl]
        bonus = _dot_split(pair(r, p) * pair(k, p) * rk[:, sl], bd, split_lhs=True) * pair(v, p)
        bb, pp = divmod(p, pairs_per_row)
        o_ref[bb, :, pp * LANES:(pp + 1) * LANES] = (yn + bonus) * gate[:, sl]


def _rwscan(r, k, v, kk, lr, ld, g, rw_r_k, rw_lnx_g, rw_lnx_b, bsz, seq, chunk):
    width = r.shape[-1]
    tri = jnp.asarray(np.tril(np.ones((chunk, chunk), np.float32))).astype(BF16)
    bd = _head_indicator(LANES)
    seq3 = lambda a: a.reshape(bsz, seq, width)
    nrows = RWSCAN_ROWS if bsz % RWSCAN_ROWS == 0 else 1
    cspec = pl.BlockSpec((nrows, chunk, width), lambda b, c: (b, c, 0))
    small = pl.BlockSpec((1, width), lambda b, c: (0, 0))
    return pl.pallas_call(
        functools.partial(_rwscan_kernel, chunk=chunk),
        grid=(bsz // nrows, seq // chunk),
        in_specs=[cspec] * 7 + [small] * 3 + [pl.BlockSpec((chunk, chunk), lambda b, c: (0, 0)),
                                              pl.BlockSpec((LANES, LANES), lambda b, c: (0, 0))],
        out_specs=cspec,
        out_shape=jax.ShapeDtypeStruct((bsz, seq, width), F32),
        scratch_shapes=[pltpu.VMEM((nrows * width // LANES, LANES, LANES), F32)],
        compiler_params=_params(("parallel", "arbitrary")),
        name="rwscan",
    )(seq3(r), seq3(k), seq3(v), seq3(kk), seq3(lr), seq3(ld), seq3(g), rw_r_k.reshape(1, width),
      rw_lnx_g[None, :], rw_lnx_b[None, :], tri, bd)


def _outproj_kernel(x_ref, mla_ref, rw_ref, gmo_ref, wo_ref, gffn_ref, wr_ref, br_ref,
                    x1_ref, comb_ref):
    tm = x_ref.shape[0]
    m = mla_ref[...]
    mn = m * lax.rsqrt(jnp.mean(m * m, axis=-1, keepdims=True) + NORM_EPS) * gmo_ref[...]
    mix = jnp.concatenate([mn.astype(BF16), rw_ref[...].astype(BF16)], axis=-1)
    acc = x_ref[...] + jnp.dot(mix, wo_ref[...], preferred_element_type=F32)
    x1_ref[...] = acc
    xn = acc * lax.rsqrt(jnp.mean(acc * acc, axis=-1, keepdims=True) + NORM_EPS) * gffn_ref[...]

    logits = _dot_bf16x3(xn, wr_ref[...]) + br_ref[...]
    lane = lax.broadcasted_iota(jnp.int32, (tm, LANES), 1)
    neg = -jnp.inf
    big = jnp.int32(1 << 20)
    is_g = (lane >= N_EXPERTS) & (lane < N_EXPERTS + N_GROUPS)
    glog = jnp.where(is_g, logits, neg)
    gmax = jnp.max(glog, axis=-1, keepdims=True)
    gsel = jnp.min(jnp.where(glog == gmax, lane, big), axis=-1, keepdims=True) - N_EXPERTS
    gprob = 1.0 / jnp.sum(jnp.exp(glog - gmax), axis=-1, keepdims=True)
    lo = gsel * EXPERTS_PER_GROUP
    in_group = (lane >= lo) & (lane < lo + EXPERTS_PER_GROUP)
    elog = jnp.where(in_group, logits, neg)
    v1 = jnp.max(elog, axis=-1, keepdims=True)
    i1 = jnp.min(jnp.where(elog == v1, lane, big), axis=-1, keepdims=True)
    elog2 = jnp.where(lane == i1, neg, elog)
    v2 = jnp.max(elog2, axis=-1, keepdims=True)
    i2 = jnp.min(jnp.where(elog2 == v2, lane, big), axis=-1, keepdims=True)
    e2 = jnp.exp(v2 - v1)
    w1 = gprob / (1.0 + e2)
    w2 = gprob * e2 / (1.0 + e2)
    comb_ref[...] = jnp.where(lane == 0, i1.astype(F32), jnp.where(lane == 1, i2.astype(F32),
                              jnp.where(lane == 2, w1, jnp.where(lane == 3, w2, 0.0))))


def _outproj(x2, mla, rw, g_mla_out, w_o, g_ffn, w_group, b_group, w_expert, b_expert, seq, tm):
    t = x2.shape[0]
    wo = w_o.astype(BF16)
    pad = LANES - N_EXPERTS - N_GROUPS
    wr = jnp.concatenate([w_expert, w_group, jnp.zeros((D_MODEL, pad), F32)], axis=-1)
    br = jnp.concatenate([b_expert, b_group, jnp.zeros((pad,), F32)])[None, :]
    gmo = g_mla_out[None, :]
    full = lambda a: pl.BlockSpec(a.shape, lambda i: (0,) * a.ndim)
    row = pl.BlockSpec((tm, D_MODEL), lambda i: (i, 0))
    return pl.pallas_call(
        _outproj_kernel,
        grid=(t // tm,),
        in_specs=[row,
                  pl.BlockSpec((tm, MLA_W), lambda i: (i, 0)),
                  pl.BlockSpec((tm, RWKV_W), lambda i: (i, 0)),
                  full(gmo), full(wo), pl.BlockSpec((1, D_MODEL), lambda i: (0, 0)), full(wr), full(br)],
        out_specs=[row, pl.BlockSpec((tm, LANES), lambda i: (i, 0))],
        out_shape=[jax.ShapeDtypeStruct((t, D_MODEL), F32),
                   jax.ShapeDtypeStruct((t, LANES), F32)],
        compiler_params=_params(("parallel",)),
        name="outproj",
    )(x2, mla, rw, gmo, wo, g_ffn[None, :], wr, br)


def _moeplan_kernel(route_ref, tri_ref, triu_ref, posw_ref, nb_ref, roff_ref):
    rt = route_ref[...]
    tb = rt.shape[0]
    lane = lax.broadcasted_iota(jnp.int32, (tb, LANES), 1)
    oh1 = lane == rt[:, 0:1].astype(jnp.int32)
    oh2 = lane == rt[:, 1:2].astype(jnp.int32)
    oh = jnp.where(oh1, 1.0, jnp.where(oh2, 1.0, 0.0))
    excl = jnp.dot(tri_ref[...], oh.astype(BF16), preferred_element_type=F32)
    cnt = jnp.sum(oh, axis=0, keepdims=True)
    nb = jnp.floor((cnt + (MOE_RB - 0.5)) * (1.0 / MOE_RB))
    boff = jnp.dot(jnp.broadcast_to(nb, (8, LANES)).astype(BF16), triu_ref[...],
                   preferred_element_type=F32)[0:1]
    roff = boff * MOE_RB
    base = roff + excl
    pos1 = jnp.sum(jnp.where(oh1, base, 0.0), axis=-1, keepdims=True)
    pos2 = jnp.sum(jnp.where(oh2, base, 0.0), axis=-1, keepdims=True)
    posw_ref[...] = jnp.where(lane == 0, pos1, jnp.where(lane == 1, pos2, rt))
    nb_ref[0] = nb.astype(jnp.int32)
    roff_ref[0] = roff.astype(jnp.int32)


def _moeplan(route, tb):
    t = route.shape[0]
    nt = t // tb
    tri = jnp.asarray(np.tril(np.ones((tb, tb), np.float32), -1)).astype(BF16)
    triu = jnp.asarray(np.triu(np.ones((LANES, LANES), np.float32), 1)).astype(BF16)
    small = pl.BlockSpec((1, 1, LANES), lambda i: (i, 0, 0))
    return pl.pallas_call(
        _moeplan_kernel,
        grid=(nt,),
        in_specs=[pl.BlockSpec((tb, LANES), lambda i: (i, 0)),
                  pl.BlockSpec((tb, tb), lambda i: (0, 0)),
                  pl.BlockSpec((LANES, LANES), lambda i: (0, 0))],
        out_specs=[pl.BlockSpec((tb, LANES), lambda i: (i, 0)), small, small],
        out_shape=[jax.ShapeDtypeStruct((t, LANES), F32),
                   jax.ShapeDtypeStruct((nt, 1, LANES), jnp.int32),
                   jax.ShapeDtypeStruct((nt, 1, LANES), jnp.int32)],
        compiler_params=_params(("parallel",)),
        name="moeplan",
    )(route, tri, triu)


def _moe_kernel(pos_ref, nb_ref, roff_ref, x1_ref, posw_ref, gffn_ref, wg_hbm, wu_hbm, wd_hbm, o_ref,
                xn, xs, g1, g2, wg_buf, wu_buf, wd_buf, sem):
    i = pl.program_id(0)
    nt = pl.num_programs(0)
    tb = x1_ref.shape[0]
    chunk = g1.shape[0]
    dummy_r0 = xs.shape[0] - MOE_RB
    group = MOE_EXPERTS_PER_STEP
    ngroups = N_EXPERTS // group

    def weight_copies(e, slot):
        return (pltpu.make_async_copy(wg_hbm.at[e], wg_buf.at[slot], sem.at[slot, 0]),
                pltpu.make_async_copy(wu_hbm.at[e], wu_buf.at[slot], sem.at[slot, 1]),
                pltpu.make_async_copy(wd_hbm.at[e], wd_buf.at[slot], sem.at[slot, 2]))

    ahead = MOE_WEIGHT_BUFS - 1

    def start_group(n):
        first = lax.rem(n, ngroups) * group
        ring = lax.rem(n, MOE_WEIGHT_BUFS) * group
        for j in range(group):
            for c in weight_copies(first + j, ring + j):
                c.start()

    @pl.when(i == 0)
    def _():
        for n in range(ahead):
            start_group(n)
        xs[...] = jnp.zeros_like(xs)

    for c in range(tb // chunk):
        sl = slice(c * chunk, (c + 1) * chunk)
        x1 = x1_ref[sl, :]
        xn[sl, :] = x1 * lax.rsqrt(jnp.mean(x1 * x1, axis=-1, keepdims=True) + NORM_EPS) * gffn_ref[...]

    def put(tok, carry):
        row = xn[pl.ds(tok, 1), :]
        xs[pl.ds(pos_ref[2 * tok], 1), :] = row
        xs[pl.ds(pos_ref[2 * tok + 1], 1), :] = row
        return carry

    lax.fori_loop(0, tb, put, 0, unroll=8)

    def expert_group(p, carry):
        n = i * ngroups + p
        base = lax.rem(n, MOE_WEIGHT_BUFS) * group

        @pl.when(n + ahead < nt * ngroups)
        def _():
            start_group(n + ahead)

        for j in range(group):
            for c in weight_copies(p * group + j, base + j):
                c.wait()
        nbs = [nb_ref[0, 0, p * group + j] for j in range(group)]
        roffs = [roff_ref[0, 0, p * group + j] for j in range(group)]
        trips = functools.reduce(jnp.maximum, nbs)

        def block(b, c2):
            r0s = [pl.multiple_of(jnp.where(b < nbs[j], roffs[j] + b * MOE_RB, dummy_r0), 8)
                   for j in range(group)]
            x = [xs[pl.ds(r0, MOE_RB), :].astype(BF16) for r0 in r0s]
            hg = [jnp.dot(x[j], wg_buf[base + j], preferred_element_type=F32) for j in range(group)]
            hu = [jnp.dot(x[j], wu_buf[base + j], preferred_element_type=F32) for j in range(group)]
            hid = [(hg[j] * jax.nn.sigmoid(hg[j]) * hu[j]).astype(BF16) for j in range(group)]
            y = [jnp.dot(hid[j], wd_buf[base + j], preferred_element_type=F32) for j in range(group)]
            for j in range(group):
                xs[pl.ds(r0s[j], MOE_RB), :] = y[j]
            return c2

        lax.fori_loop(0, trips, block, 0)
        return carry

    lax.fori_loop(0, ngroups, expert_group, 0)

    for c in range(tb // chunk):
        def take(j, carry, c=c):
            tok = c * chunk + j
            g1[pl.ds(j, 1), :] = xs[pl.ds(pos_ref[2 * tok], 1), :]
            g2[pl.ds(j, 1), :] = xs[pl.ds(pos_ref[2 * tok + 1], 1), :]
            return carry

        lax.fori_loop(0, chunk, take, 0, unroll=8)
        sl = slice(c * chunk, (c + 1) * chunk)
        o_ref[sl, :] = x1_ref[sl, :] + (posw_ref[sl, 2:3] * g1[...] + posw_ref[sl, 3:4] * g2[...])


def _moe(x1, route, g_ffn, w_gate, w_up, w_down, tb):
    t = x1.shape[0]
    nt = t // tb
    chunk = min(tb, 256)
    rows = 2 * tb + N_EXPERTS * MOE_RB + MOE_RB
    nslot = MOE_WEIGHT_BUFS * MOE_EXPERTS_PER_STEP
    posw, nb, roff = _moeplan(route, tb)
    pos = posw[:, 0:2].astype(jnp.int32).reshape(2 * t)
    smem3 = lambda: pl.BlockSpec((1, 1, LANES), lambda i: (i, 0, 0), memory_space=pltpu.SMEM)
    hbm = lambda: pl.BlockSpec(memory_space=pl.ANY)
    return pl.pallas_call(
        _moe_kernel,
        grid=(nt,),
        in_specs=[pl.BlockSpec((2 * tb,), lambda i: (i,), memory_space=pltpu.SMEM), smem3(), smem3(),
                  pl.BlockSpec((tb, D_MODEL), lambda i: (i, 0)),
                  pl.BlockSpec((tb, LANES), lambda i: (i, 0)),
                  pl.BlockSpec((1, D_MODEL), lambda i: (0, 0)),
                  hbm(), hbm(), hbm()],
        out_specs=pl.BlockSpec((tb, D_MODEL), lambda i: (i, 0)),
        out_shape=jax.ShapeDtypeStruct((t, D_MODEL), F32),
        scratch_shapes=[pltpu.VMEM((tb, D_MODEL), F32), pltpu.VMEM((rows, D_MODEL), F32),
                        pltpu.VMEM((chunk, D_MODEL), F32), pltpu.VMEM((chunk, D_MODEL), F32),
                        pltpu.VMEM((nslot, D_MODEL, D_EXPERT), BF16), pltpu.VMEM((nslot, D_MODEL, D_EXPERT), BF16),
                        pltpu.VMEM((nslot, D_EXPERT, D_MODEL), BF16), pltpu.SemaphoreType.DMA((nslot, 3))],
        compiler_params=_params(("arbitrary",)),
        name="moe",
    )(pos, nb, roff, x1, posw, g_ffn[None, :], w_gate.astype(BF16), w_up.astype(BF16), w_down.astype(BF16))


def _tile(seq, want):
    tm = min(seq, want)
    assert seq % tm == 0 and tm % 8 == 0
    return tm


def _layer(x, positions, g_mix, w_in, g_cq, w_uq, g_ckv, w_uk, w_uv, g_qn, g_kn, g_mla_out,
           rw_mu, rw_w0, rw_w2, rw_a0, rw_a2, rw_g2, rw_k_k, rw_k_a, rw_r_k, rw_lnx_g, rw_lnx_b,
           w_o, g_ffn, w_group, b_group, w_expert, b_expert, w_gate, w_up, w_down):
    bsz, seq, d = x.shape
    t = bsz * seq
    x2 = x.reshape(t, d)
    pos2 = positions.reshape(t, 1)
    kv_blk = _tile(seq, KV_BLK)
    feat, q, k, v = _inproj(x2, pos2, g_mix, w_in, g_cq, w_uq, g_ckv, w_uk, w_uv, g_qn, g_kn,
                            bsz, seq, _tile(kv_blk, 512), kv_blk)
    mla = _attn(q, k, v).reshape(t, MLA_W)
    r, k2, vv, kk, lr, ld, g = _rwprep(feat, rw_mu, rw_w0, rw_w2, rw_a0, rw_a2, rw_g2, rw_k_k, rw_k_a,
                                       bsz, seq, _tile(seq, 512))
    rw = _rwscan(r, k2, vv, kk, lr, ld, g, rw_r_k, rw_lnx_g, rw_lnx_b, bsz, seq, _tile(seq, 128))
    rw = rw.reshape(t, RWKV_W)
    x1, route = _outproj(x2, mla, rw, g_mla_out, w_o, g_ffn, w_group, b_group, w_expert, b_expert,
                         seq, _tile(seq, 512))
    out = _moe(x1, route, g_ffn, w_gate, w_up, w_down, _tile(t, MOE_TB))
    return out.reshape(bsz, seq, d)


def kernel(x, positions, g_mix, w_in, g_cq, w_uq, g_ckv, w_uk, w_uv, g_qn, g_kn, g_mla_out, rw_mu, rw_w0, rw_w2, rw_a0, rw_a2, rw_g2, rw_k_k, rw_k_a, rw_r_k, rw_lnx_g, rw_lnx_b, w_o, g_ffn, w_group, b_group, w_expert, b_expert, w_gate, w_up, w_down):
    for l in range(g_mix.shape[0]):
        x = _layer(x, positions, g_mix[l], w_in[l], g_cq[l], w_uq[l], g_ckv[l], w_uk[l], w_uv[l],
                   g_qn[l], g_kn[l], g_mla_out[l], rw_mu[l], rw_w0[l], rw_w2[l], rw_a0[l], rw_a2[l],
                   rw_g2[l], rw_k_k[l], rw_k_a[l], rw_r_k[l], rw_lnx_g[l], rw_lnx_b[l], w_o[l],
                   g_ffn[l], w_group[l], b_group[l], w_expert[l], b_expert[l], w_gate[l], w_up[l],
                   w_down[l])
    return x
```

```python
import functools

import numpy as np
import jax
import jax.numpy as jnp
from jax import lax
from jax.experimental import pallas as pl
from jax.experimental.pallas import tpu as pltpu

F32 = jnp.float32
BF16 = jnp.bfloat16

D_MODEL = 1024
MLA_HEADS = 8
MLA_NOPE = 64
MLA_ROPE = 32
MLA_QK = MLA_NOPE + MLA_ROPE
MLA_V = 64
MLA_W = MLA_HEADS * MLA_V
MLA_Q_RANK = 256
MLA_KV_RANK = 128
ROPE_THETA = 10000.0
RWKV_HEADS = 8
RWKV_N = 64
RWKV_W = RWKV_HEADS * RWKV_N
DECAY_LORA = 64
AAA_LORA = 64
GATE_LORA = 128
MLA_IN = MLA_Q_RANK + MLA_KV_RANK + MLA_ROPE
RWKV_IN = 3 * RWKV_W + DECAY_LORA + AAA_LORA + GATE_LORA
N_GROUPS = 4
EXPERTS_PER_GROUP = 8
N_EXPERTS = N_GROUPS * EXPERTS_PER_GROUP
D_EXPERT = 256
NORM_EPS = 1e-6
LNX_EPS = 64e-5

LANES = 128
QK_PAD = LANES
KV_BLK = 512
ATT_HEADS = 4
RWSCAN_ROWS = 4
MOE_TB = 1024
MOE_RB = 96
MOE_EXPERTS_PER_STEP = 2
MOE_WEIGHT_BUFS = 3
VT_ROWS = MLA_V + 16
Q_SCALE = MLA_QK ** -0.5 * float(np.log2(np.e))
PM_W = MLA_Q_RANK + MLA_KV_RANK + 2 * LANES
VMEM_LIMIT = 56 * 1024 * 1024


def _dot_bf16x3(a, b):
    a_hi = a.astype(BF16)
    a_lo = (a - a_hi.astype(F32)).astype(BF16)
    b_hi = b.astype(BF16)
    b_lo = (b - b_hi.astype(F32)).astype(BF16)
    d = lambda p, q: jnp.dot(p, q, preferred_element_type=F32)
    return d(a_hi, b_hi) + (d(a_hi, b_lo) + d(a_lo, b_hi))


def _dot(a, b):
    return jnp.dot(a.astype(BF16), b.astype(BF16), preferred_element_type=F32)


def _dot_nt(a, b):
    return lax.dot_general(a.astype(BF16), b.astype(BF16), (((1,), (1,)), ((), ())),
                           preferred_element_type=F32)


def _dot_tn(a, b):
    return lax.dot_general(a.astype(BF16), b.astype(BF16), (((0,), (0,)), ((), ())),
                           preferred_element_type=F32)


def _dot_split(a, b, split_lhs, passes=2):
    s = a if split_lhs else b
    out = None
    for _ in range(passes):
        piece = s.astype(BF16)
        term = (jnp.dot(piece, b, preferred_element_type=F32) if split_lhs
                else jnp.dot(a, piece, preferred_element_type=F32))
        out = term if out is None else out + term
        s = s - piece.astype(F32)
    return out


def _head_indicator(width):
    hid = np.arange(width) // RWKV_N
    return jnp.asarray((hid[:, None] == hid[None, :]).astype(np.float32)).astype(BF16)


def _params(sem):
    return pltpu.CompilerParams(dimension_semantics=sem, vmem_limit_bytes=VMEM_LIMIT)


def _inproj_kernel(x_ref, pos_ref, gmix_ref, wmla_ref, wrw_ref, freq_ref, gcq_ref, gckv_ref,
                   wq_ref, wqr_ref, wk_ref, wv_ref, gq_ref, gqr_ref, gk_ref, gkr_ref,
                   feat_ref, q_ref, k_ref, v_ref):
    x = x_ref[...]
    h = x * lax.rsqrt(jnp.mean(x * x, axis=-1, keepdims=True) + NORM_EPS) * gmix_ref[...]
    hb = h.astype(BF16)
    feat_ref[...] = jnp.dot(hb, wrw_ref[...], preferred_element_type=F32)
    pm = jnp.dot(hb, wmla_ref[...], preferred_element_type=F32)

    c_q = pm[:, :MLA_Q_RANK]
    c_kv = pm[:, MLA_Q_RANK:MLA_Q_RANK + MLA_KV_RANK]
    kr = pm[:, MLA_Q_RANK + MLA_KV_RANK:MLA_Q_RANK + MLA_KV_RANK + LANES]
    kr_rot = pm[:, MLA_Q_RANK + MLA_KV_RANK + LANES:PM_W]
    cqn = (c_q * lax.rsqrt(jnp.mean(c_q * c_q, axis=-1, keepdims=True) + NORM_EPS)
           * gcq_ref[...]).astype(BF16)
    ckvn = (c_kv * lax.rsqrt(jnp.mean(c_kv * c_kv, axis=-1, keepdims=True) + NORM_EPS)
            * gckv_ref[...]).astype(BF16)

    ang = pos_ref[...].astype(F32) * freq_ref[...]
    cosf = jnp.cos(ang)
    sinf = jnp.sin(ang)
    q_cos = gq_ref[...] * cosf * Q_SCALE
    q_sin = gqr_ref[...] * sinf * Q_SCALE
    k_cos = gk_ref[...] * cosf
    k_sin = kr_rot * (gkr_ref[...] * sinf)
    inv_qk = 1.0 / MLA_QK
    ones_rows = jnp.where(lax.broadcasted_iota(jnp.int32, (VT_ROWS - MLA_V, x.shape[0]), 0) == 0, 1.0, 0.0)

    for hh in range(MLA_HEADS):
        q_raw = jnp.dot(cqn, wq_ref[hh], preferred_element_type=F32)
        q_rot = jnp.dot(cqn, wqr_ref[hh], preferred_element_type=F32)
        sq = lax.rsqrt(jnp.sum(q_raw * q_raw, axis=-1, keepdims=True) * inv_qk + NORM_EPS)
        q_ref[0, hh] = (sq * (q_raw * q_cos + q_rot * q_sin)).astype(BF16)
        k_raw = jnp.dot(ckvn, wk_ref[hh], preferred_element_type=F32) + kr
        sk = lax.rsqrt(jnp.sum(k_raw * k_raw, axis=-1, keepdims=True) * inv_qk + NORM_EPS)
        k_ref[0, hh] = (sk * (k_raw * k_cos + k_sin)).astype(BF16)
        vt = lax.dot_general(wv_ref[hh], ckvn, (((1,), (1,)), ((), ())), preferred_element_type=F32)
        v_ref[0, hh, 0] = jnp.concatenate([vt, ones_rows], axis=0).astype(BF16)


def _rot_cols(w):
    half = MLA_ROPE // 2
    z = jnp.zeros_like(w[..., :MLA_NOPE])
    return jnp.concatenate([z, -w[..., MLA_NOPE + half:MLA_QK], w[..., MLA_NOPE:MLA_NOPE + half]], axis=-1)


def _pad_last(w, n):
    return jnp.pad(w, [(0, 0)] * (w.ndim - 1) + [(0, n - w.shape[-1])])


def _inproj(x2, pos2, g_mix, w_in, g_cq, w_uq, g_ckv, w_uk, w_uv, g_qn, g_kn, bsz, seq, tm, kv_blk):
    t = x2.shape[0]
    nt = seq // tm
    per_kv = kv_blk // tm
    half = MLA_ROPE // 2
    w_kr = w_in[:, MLA_Q_RANK + MLA_KV_RANK:MLA_IN]
    zeros64 = jnp.zeros((D_MODEL, MLA_NOPE), F32)
    zeros32 = jnp.zeros((D_MODEL, LANES - MLA_QK), F32)
    w_kr_rot = jnp.concatenate([-w_kr[:, half:], w_kr[:, :half]], axis=-1)
    w_mla = jnp.concatenate([w_in[:, :MLA_Q_RANK + MLA_KV_RANK], zeros64, w_kr, zeros32,
                             zeros64, w_kr_rot, zeros32], axis=-1).astype(BF16)
    w_rw = w_in[:, MLA_IN:].astype(BF16)
    wq = jnp.transpose(w_uq, (1, 0, 2))
    wq_rot = _pad_last(_rot_cols(wq), QK_PAD).astype(BF16)
    wq = _pad_last(wq, QK_PAD).astype(BF16)
    wk = _pad_last(jnp.transpose(w_uk, (1, 0, 2)), QK_PAD).astype(BF16)
    wv = jnp.transpose(w_uv, (1, 2, 0)).astype(BF16)
    inv_freq = 1.0 / (ROPE_THETA ** (jnp.arange(half, dtype=F32) / half))
    freq = jnp.concatenate([jnp.zeros((MLA_NOPE,), F32), inv_freq, inv_freq,
                            jnp.zeros((LANES - MLA_QK,), F32)])[None, :]
    swap = lambda g: jnp.concatenate([jnp.zeros((MLA_NOPE,), F32), g[MLA_NOPE + half:MLA_QK],
                                      g[MLA_NOPE:MLA_NOPE + half], jnp.zeros((LANES - MLA_QK,), F32)])[None, :]
    gq = _pad_last(g_qn[None, :], QK_PAD)
    gk = _pad_last(g_kn[None, :], QK_PAD)
    gq_rot = swap(g_qn)
    gk_rot = swap(g_kn)

    full = lambda a: pl.BlockSpec(a.shape, lambda i: (0,) * a.ndim)
    head_map = lambda i: (i // nt, 0, i % nt, 0)
    args = (x2, pos2, g_mix[None, :], w_mla, w_rw, freq, g_cq[None, :], g_ckv[None, :],
            wq, wq_rot, wk, wv, gq, gq_rot, gk, gk_rot)
    in_specs = [pl.BlockSpec((tm, D_MODEL), lambda i: (i, 0)),
                pl.BlockSpec((tm, 1), lambda i: (i, 0))] + [full(a) for a in args[2:]]
    return pl.pallas_call(
        _inproj_kernel,
        grid=(t // tm,),
        in_specs=in_specs,
        out_specs=[pl.BlockSpec((tm, RWKV_IN), lambda i: (i, 0)),
                   pl.BlockSpec((1, MLA_HEADS, tm, QK_PAD), head_map),
                   pl.BlockSpec((1, MLA_HEADS, tm, QK_PAD), head_map),
                   pl.BlockSpec((1, MLA_HEADS, 1, VT_ROWS, tm),
                                lambda i: (i // nt, 0, (i % nt) // per_kv, 0, (i % nt) % per_kv))],
        out_shape=[jax.ShapeDtypeStruct((t, RWKV_IN), F32),
                   jax.ShapeDtypeStruct((bsz, MLA_HEADS, seq, QK_PAD), BF16),
                   jax.ShapeDtypeStruct((bsz, MLA_HEADS, seq, QK_PAD), BF16),
                   jax.ShapeDtypeStruct((bsz, MLA_HEADS, seq // kv_blk, VT_ROWS, kv_blk), BF16)],
        compiler_params=_params(("parallel",)),
        name="inproj",
    )(*args)


def _attn_kernel(q_ref, k_ref, vt_ref, o_ref, s_scr):
    t = pl.program_id(2)
    nh = q_ref.shape[1]
    kvb = vt_ref.shape[4]
    qb = kvb

    def q_tile(w):
        return [q_ref[0, h, w * qb:(w + 1) * qb, :] for h in range(nh)]

    def scores_to(qs, g, slot):
        start = pl.multiple_of(g * kvb, kvb)
        for h in range(nh):
            s_scr[slot, h] = lax.dot_general(k_ref[0, h, pl.ds(start, kvb), :], qs[h], (((1,), (1,)), ((), ())),
                                             preferred_element_type=F32)

    def consume(qi, g, slot, carry, masked):
        ss = [s_scr[slot, h] for h in range(nh)]
        if masked:
            key = g * kvb + lax.broadcasted_iota(jnp.int32, (kvb, qb), 0)
            qry = qi * qb + lax.broadcasted_iota(jnp.int32, (kvb, qb), 1)
            ss = [jnp.where(key <= qry, s, -jnp.inf) for s in ss]
        hw = qb // 2
        out = []
        for h in range(nh):
            m, acc = carry[h]
            m_new, acc_new = [], []
            for c in range(2):
                cs = slice(c * hw, (c + 1) * hw)
                s_c = ss[h][:, cs]
                m_c = jnp.maximum(m[:, cs], jnp.max(s_c, axis=0, keepdims=True))
                p_c = jnp.exp2((s_c - m_c).astype(BF16))
                alpha = jnp.exp2(m[:, cs] - m_c)
                m_new.append(m_c)
                acc_new.append(alpha * acc[:, cs] + jnp.dot(vt_ref[0, h, g], p_c, preferred_element_type=F32))
            out.append((jnp.concatenate(m_new, axis=1), jnp.concatenate(acc_new, axis=1)))
        return tuple(out)

    def trips(qs, qi, cur, nxt, carry):
        def two_groups(j, c):
            scores_to(qs, 2 * j + 1, nxt)
            c = consume(qi, 2 * j, cur, c, False)
            scores_to(qs, 2 * j + 2, cur)
            return consume(qi, 2 * j + 1, nxt, c, False)

        return lax.fori_loop(0, t, two_groups, carry)

    def finish(carry, w):
        ot = jnp.concatenate([acc[:MLA_V] * (1.0 / acc[MLA_V:MLA_V + 1]) for _, acc in carry], axis=0)
        o_ref[0, w * qb:(w + 1) * qb, :] = ot.T

    init = tuple((jnp.full((1, qb), -1e30, F32), jnp.zeros((VT_ROWS, qb), F32)) for _ in range(nh))
    q_a, q_b = q_tile(0), q_tile(1)
    tile_a, tile_b = 2 * t, 2 * t + 1

    scores_to(q_a, 0, 0)
    carry = trips(q_a, tile_a, 0, 1, init)
    scores_to(q_b, 0, 1)
    finish(consume(tile_a, 2 * t, 0, carry, True), 0)
    carry = trips(q_b, tile_b, 1, 0, init)
    scores_to(q_b, 2 * t + 1, 0)
    carry = consume(tile_b, 2 * t, 1, carry, False)
    finish(consume(tile_b, 2 * t + 1, 0, carry, True), 1)


def _attn(q, k, vt):
    bsz, nh, seq, _ = q.shape
    per_step = ATT_HEADS
    ngrp, kvb = vt.shape[2], vt.shape[4]
    assert seq % (2 * kvb) == 0
    return pl.pallas_call(
        _attn_kernel,
        grid=(bsz, nh // per_step, seq // (2 * kvb)),
        in_specs=[pl.BlockSpec((1, per_step, 2 * kvb, QK_PAD), lambda b, h, i: (b, h, i, 0)),
                  pl.BlockSpec((1, per_step, seq, QK_PAD), lambda b, h, i: (b, h, 0, 0)),
                  pl.BlockSpec((1, per_step, ngrp, VT_ROWS, kvb), lambda b, h, i: (b, h, 0, 0, 0))],
        out_specs=pl.BlockSpec((1, 2 * kvb, per_step * MLA_V), lambda b, h, i: (b, i, h)),
        out_shape=jax.ShapeDtypeStruct((bsz, seq, nh * MLA_V), F32),
        scratch_shapes=[pltpu.VMEM((2, per_step, kvb, kvb), F32)],
        compiler_params=_params(("parallel", "parallel", "arbitrary")),
        name="attn",
    )(q, k, vt)


def _rwprep_kernel(feat_ref, prev_ref, mu_ref, w0_ref, w2_ref, a0_ref, a2_ref, g2_ref, kk_ref, ka_ref,
                   bd_ref, r_o, k_o, v_o, kk_o, lr_o, ld_o, g_o, *, nt):
    i = pl.program_id(0)
    feat = feat_ref[...]
    tm = feat.shape[0]
    first = jnp.where(i % nt == 0, 0.0, 1.0)
    prev_row = prev_ref[7:8, :] * first
    rolled = pltpu.roll(feat, 1, axis=0)
    rid = lax.broadcasted_iota(jnp.int32, (tm, 1), 0)
    prev = jnp.where(rid == 0, prev_row, rolled)
    feat = feat + (prev - feat) * mu_ref[...]

    r = feat[:, :RWKV_W]
    k = feat[:, RWKV_W:2 * RWKV_W]
    v = feat[:, 2 * RWKV_W:3 * RWKV_W]
    wa = feat[:, 3 * RWKV_W:3 * RWKV_W + LANES]
    gl = feat[:, 3 * RWKV_W + LANES:]
    lane = lax.broadcasted_iota(jnp.int32, (tm, LANES), 1)
    wl_in = jnp.where(lane < DECAY_LORA, jnp.tanh(wa), 0.0)
    al_in = jnp.where(lane < DECAY_LORA, 0.0, wa)
    wpre = w0_ref[...] + _dot(wl_in, w2_ref[...])
    w = -(jnp.maximum(-wpre, 0.0) + jnp.log(1.0 + jnp.exp(-jnp.abs(wpre)))) - 0.5
    logd = -jnp.exp(w)
    a = jax.nn.sigmoid(a0_ref[...] + _dot(al_in, a2_ref[...]))
    g = _dot(jax.nn.sigmoid(gl), g2_ref[...])
    kk = k * kk_ref[...]
    ss = _dot_split(kk * kk, bd_ref[...], split_lhs=True)
    r_o[...] = r
    k_o[...] = k * (1.0 + (a - 1.0) * ka_ref[...])
    v_o[...] = v
    kk_o[...] = kk * lax.rsqrt(ss + 1e-12)
    lr_o[...] = a
    ld_o[...] = logd
    g_o[...] = g


def _rwprep(feat, rw_mu, rw_w0, rw_w2, rw_a0, rw_a2, rw_g2, rw_k_k, rw_k_a, bsz, seq, tm):
    t = feat.shape[0]
    nt = seq // tm
    w2p = jnp.concatenate([rw_w2, jnp.zeros((LANES - DECAY_LORA, RWKV_W), F32)], axis=0).astype(BF16)
    a2p = jnp.concatenate([jnp.zeros((DECAY_LORA, RWKV_W), F32), rw_a2], axis=0).astype(BF16)
    args = (feat, feat, rw_mu[None, :], rw_w0[None, :], w2p, rw_a0[None, :], a2p, rw_g2.astype(BF16),
            rw_k_k[None, :], rw_k_a[None, :], _head_indicator(RWKV_W))
    full = lambda a: pl.BlockSpec(a.shape, lambda i: (0,) * a.ndim)
    rows8 = tm // 8
    in_specs = [pl.BlockSpec((tm, RWKV_IN), lambda i: (i, 0)),
                pl.BlockSpec((8, RWKV_IN), lambda i: (jnp.maximum(i * rows8 - 1, 0), 0))]
    in_specs += [full(a) for a in args[2:]]
    hshape = jax.ShapeDtypeStruct((t, RWKV_W), F32)
    hspec = pl.BlockSpec((tm, RWKV_W), lambda i: (i, 0))
    return pl.pallas_call(
        functools.partial(_rwprep_kernel, nt=nt),
        grid=(t // tm,),
        in_specs=in_specs,
        out_specs=[hspec] * 7,
        out_shape=[hshape] * 7,
        compiler_params=_params(("parallel",)),
        name="rwprep",
    )(*args)


def _rwscan_kernel(r_ref, k_ref, v_ref, kk_ref, lr_ref, ld_ref, g_ref, rk_ref, lng_ref, lnb_ref,
                   tri_ref, bd_ref, o_ref, state, *, chunk):
    c = pl.program_id(1)

    @pl.when(c == 0)
    def _():
        state[...] = jnp.zeros_like(state)

    C = chunk
    nrows = r_ref.shape[0]
    pairs_per_row = RWKV_W // LANES
    npairs = nrows * pairs_per_row
    wide = lambda ref: jnp.concatenate([ref[b] for b in range(nrows)], axis=-1)
    tiled = lambda ref: jnp.concatenate([ref[...]] * nrows, axis=-1)
    r = wide(r_ref)
    k = wide(k_ref)
    v = wide(v_ref)
    kk = wide(kk_ref)
    ld = wide(ld_ref)
    lr = wide(lr_ref)
    gate = wide(g_ref)
    rk = tiled(rk_ref)
    lng = tiled(lng_ref)
    lnb = tiled(lnb_ref)
    bd = bd_ref[...]

    cum = _dot_split(tri_ref[...], ld, split_lhs=False)
    mid = C // 2 - 1
    cmid = cum[mid:mid + 1, :]
    cend = cum[C - 1:C, :]
    e_pos = jnp.exp(cum - cmid)
    e_prev = jnp.exp(cum - ld - cmid)
    e_neg = jnp.exp(cmid - cum)
    g_mid = jnp.exp(cmid)
    g_end_rel = jnp.exp(cend - cmid)
    g_end = jnp.exp(cend)
    rt = r * e_pos
    at = -kk * e_prev
    bt = kk * lr * e_neg
    kt = k * e_neg
    at_g = (at * g_mid).astype(BF16)
    rt_g = (rt * g_mid).astype(BF16)
    art = jnp.concatenate([at, rt], axis=0)
    bk = jnp.concatenate([bt, kt], axis=0).astype(BF16)
    bkg = jnp.concatenate([bt * g_end_rel, kt * g_end_rel], axis=0).astype(BF16)
    vb = v.astype(BF16)

    row = lax.broadcasted_iota(jnp.int32, (C, C), 0)
    col = lax.broadcasted_iota(jnp.int32, (C, C), 1)
    strict = row > col
    eye = (row == col).astype(F32)
    row2 = lax.broadcasted_iota(jnp.int32, (C, 2 * C), 0)
    col2 = lax.broadcasted_iota(jnp.int32, (C, 2 * C), 1)
    incl2 = row2 >= jnp.where(col2 >= C, col2 - C, col2)
    low2 = lax.broadcasted_iota(jnp.int32, (2 * C, LANES), 1) < RWKV_N
    low = lax.broadcasted_iota(jnp.int32, (C, LANES), 1) < RWKV_N
    srow = lax.broadcasted_iota(jnp.int32, (LANES, LANES), 0)
    scol = lax.broadcasted_iota(jnp.int32, (LANES, LANES), 1)
    same_head = (srow < RWKV_N) == (scol < RWKV_N)
    diag = srow == scol
    pair = lambda a, p: a[:, p * LANES:(p + 1) * LANES]

    a_ab, a_ak, a_rbk = [], [], []
    for p in range(npairs):
        art_p = pair(art, p)
        bk_p = pair(bk, p)
        for half in range(2):
            lhs = jnp.where(low2, art_p, 0.0) if half == 0 else jnp.where(low2, 0.0, art_p)
            big = _dot_nt(lhs, bk_p)
            a_ab.append(jnp.where(strict, big[:C, :C], 0.0))
            a_ak.append(jnp.where(strict, big[:C, C:], 0.0).astype(BF16))
            a_rbk.append(jnp.where(incl2, big[C:, :], 0.0).astype(BF16))

    ps = [a.astype(BF16) for a in a_ab]
    tinv = [eye + a for a in a_ab]
    n = 2
    while n < C:
        sq = [jnp.dot(pb, pb, preferred_element_type=F32) for pb in ps]
        ps = [s.astype(BF16) for s in sq]
        tinv = [t + jnp.dot(t.astype(BF16), pb, preferred_element_type=F32) for t, pb in zip(tinv, ps)]
        n *= 2
    tinv = [t.astype(BF16) for t in tinv]

    s0 = [state[p] for p in range(npairs)]
    s0b = [s.astype(BF16) for s in s0]
    sel = lambda h0, h1: jnp.where(low, h0, h1)
    mm = lambda a, b: jnp.dot(a, b, preferred_element_type=F32)
    xs = [mm(pair(at_g, p), s0b[p]) + sel(mm(a_ak[2 * p], pair(vb, p)), mm(a_ak[2 * p + 1], pair(vb, p)))
          for p in range(npairs)]
    xb = [x.astype(BF16) for x in xs]
    us = [sel(mm(tinv[2 * p], xb[p]), mm(tinv[2 * p + 1], xb[p])) for p in range(npairs)]
    uv = [jnp.concatenate([us[p].astype(BF16), pair(vb, p)], axis=0) for p in range(npairs)]
    ys = [mm(pair(rt_g, p), s0b[p]) + sel(mm(a_rbk[2 * p], uv[p]), mm(a_rbk[2 * p + 1], uv[p]))
          for p in range(npairs)]
    for p in range(npairs):
        upd = lax.dot_general(pair(bkg, p), uv[p], (((0,), (0,)), ((), ())), preferred_element_type=F32)
        g_col = jnp.sum(jnp.where(diag, pair(g_end, p), 0.0), axis=-1, keepdims=True)
        state[p] = s0[p] * g_col + jnp.where(same_head, upd, 0.0)

    inv_n = 1.0 / RWKV_N
    for p in range(npairs):
        sl = slice(p * LANES, (p + 1) * LANES)
        y = ys[p]
        mean = _dot_split(y, bd, split_lhs=True) * inv_n
        yc = y - mean
        var = _dot_split(yc * yc, bd, split_lhs=True) * inv_n
        yn = yc * lax.rsqrt(var + LNX_EPS) * lng[:, sl] + lnb[:, sl]
        bonus = _dot_split(pair(r, p) * pair(k, p) * rk[:, sl], bd, split_lhs=True) * pair(v, p)
        bb, pp = divmod(p, pairs_per_row)
        o_ref[bb, :, pp * LANES:(pp + 1) * LANES] = (yn + bonus) * gate[:, sl]


def _rwscan(r, k, v, kk, lr, ld, g, rw_r_k, rw_lnx_g, rw_lnx_b, bsz, seq, chunk):
    width = r.shape[-1]
    tri = jnp.asarray(np.tril(np.ones((chunk, chunk), np.float32))).astype(BF16)
    bd = _head_indicator(LANES)
    seq3 = lambda a: a.reshape(bsz, seq, width)
    nrows = RWSCAN_ROWS if bsz % RWSCAN_ROWS == 0 else 1
    cspec = pl.BlockSpec((nrows, chunk, width), lambda b, c: (b, c, 0))
    small = pl.BlockSpec((1, width), lambda b, c: (0, 0))
    return pl.pallas_call(
        functools.partial(_rwscan_kernel, chunk=chunk),
        grid=(bsz // nrows, seq // chunk),
        in_specs=[cspec] * 7 + [small] * 3 + [pl.BlockSpec((chunk, chunk), lambda b, c: (0, 0)),
                                              pl.BlockSpec((LANES, LANES), lambda b, c: (0, 0))],
        out_specs=cspec,
        out_shape=jax.ShapeDtypeStruct((bsz, seq, width), F32),
        scratch_shapes=[pltpu.VMEM((nrows * width // LANES, LANES, LANES), F32)],
        compiler_params=_params(("parallel", "arbitrary")),
        name="rwscan",
    )(seq3(r), seq3(k), seq3(v), seq3(kk), seq3(lr), seq3(ld), seq3(g), rw_r_k.reshape(1, width),
      rw_lnx_g[None, :], rw_lnx_b[None, :], tri, bd)


def _outproj_kernel(x_ref, mla_ref, rw_ref, gmo_ref, wo_ref, gffn_ref, wr_ref, br_ref,
                    x1_ref, comb_ref):
    tm = x_ref.shape[0]
    m = mla_ref[...]
    mn = m * lax.rsqrt(jnp.mean(m * m, axis=-1, keepdims=True) + NORM_EPS) * gmo_ref[...]
    mix = jnp.concatenate([mn.astype(BF16), rw_ref[...].astype(BF16)], axis=-1)
    acc = x_ref[...] + jnp.dot(mix, wo_ref[...], preferred_element_type=F32)
    x1_ref[...] = acc
    xn = acc * lax.rsqrt(jnp.mean(acc * acc, axis=-1, keepdims=True) + NORM_EPS) * gffn_ref[...]

    logits = _dot_bf16x3(xn, wr_ref[...]) + br_ref[...]
    lane = lax.broadcasted_iota(jnp.int32, (tm, LANES), 1)
    neg = -jnp.inf
    big = jnp.int32(1 << 20)
    is_g = (lane >= N_EXPERTS) & (lane < N_EXPERTS + N_GROUPS)
    glog = jnp.where(is_g, logits, neg)
    gmax = jnp.max(glog, axis=-1, keepdims=True)
    gsel = jnp.min(jnp.where(glog == gmax, lane, big), axis=-1, keepdims=True) - N_EXPERTS
    gprob = 1.0 / jnp.sum(jnp.exp(glog - gmax), axis=-1, keepdims=True)
    lo = gsel * EXPERTS_PER_GROUP
    in_group = (lane >= lo) & (lane < lo + EXPERTS_PER_GROUP)
    elog = jnp.where(in_group, logits, neg)
    v1 = jnp.max(elog, axis=-1, keepdims=True)
    i1 = jnp.min(jnp.where(elog == v1, lane, big), axis=-1, keepdims=True)
    elog2 = jnp.where(lane == i1, neg, elog)
    v2 = jnp.max(elog2, axis=-1, keepdims=True)
    i2 = jnp.min(jnp.where(elog2 == v2, lane, big), axis=-1, keepdims=True)
    e2 = jnp.exp(v2 - v1)
    w1 = gprob / (1.0 + e2)
    w2 = gprob * e2 / (1.0 + e2)
    comb_ref[...] = jnp.where(lane == 0, i1.astype(F32), jnp.where(lane == 1, i2.astype(F32),
                              jnp.where(lane == 2, w1, jnp.where(lane == 3, w2, 0.0))))


def _outproj(x2, mla, rw, g_mla_out, w_o, g_ffn, w_group, b_group, w_expert, b_expert, seq, tm):
    t = x2.shape[0]
    wo = w_o.astype(BF16)
    pad = LANES - N_EXPERTS - N_GROUPS
    wr = jnp.concatenate([w_expert, w_group, jnp.zeros((D_MODEL, pad), F32)], axis=-1)
    br = jnp.concatenate([b_expert, b_group, jnp.zeros((pad,), F32)])[None, :]
    gmo = g_mla_out[None, :]
    full = lambda a: pl.BlockSpec(a.shape, lambda i: (0,) * a.ndim)
    row = pl.BlockSpec((tm, D_MODEL), lambda i: (i, 0))
    return pl.pallas_call(
        _outproj_kernel,
        grid=(t // tm,),
        in_specs=[row,
                  pl.BlockSpec((tm, MLA_W), lambda i: (i, 0)),
                  pl.BlockSpec((tm, RWKV_W), lambda i: (i, 0)),
                  full(gmo), full(wo), pl.BlockSpec((1, D_MODEL), lambda i: (0, 0)), full(wr), full(br)],
        out_specs=[row, pl.BlockSpec((tm, LANES), lambda i: (i, 0))],
        out_shape=[jax.ShapeDtypeStruct((t, D_MODEL), F32),
                   jax.ShapeDtypeStruct((t, LANES), F32)],
        compiler_params=_params(("parallel",)),
        name="outproj",
    )(x2, mla, rw, gmo, wo, g_ffn[None, :], wr, br)


def _moeplan_kernel(route_ref, tri_ref, triu_ref, posw_ref, nb_ref, roff_ref):
    rt = route_ref[...]
    tb = rt.shape[0]
    lane = lax.broadcasted_iota(jnp.int32, (tb, LANES), 1)
    oh1 = lane == rt[:, 0:1].astype(jnp.int32)
    oh2 = lane == rt[:, 1:2].astype(jnp.int32)
    oh = jnp.where(oh1, 1.0, jnp.where(oh2, 1.0, 0.0))
    excl = jnp.dot(tri_ref[...], oh.astype(BF16), preferred_element_type=F32)
    cnt = jnp.sum(oh, axis=0, keepdims=True)
    nb = jnp.floor((cnt + (MOE_RB - 0.5)) * (1.0 / MOE_RB))
    boff = jnp.dot(jnp.broadcast_to(nb, (8, LANES)).astype(BF16), triu_ref[...],
                   preferred_element_type=F32)[0:1]
    roff = boff * MOE_RB
    base = roff + excl
    pos1 = jnp.sum(jnp.where(oh1, base, 0.0), axis=-1, keepdims=True)
    pos2 = jnp.sum(jnp.where(oh2, base, 0.0), axis=-1, keepdims=True)
    posw_ref[...] = jnp.where(lane == 0, pos1, jnp.where(lane == 1, pos2, rt))
    nb_ref[0] = nb.astype(jnp.int32)
    roff_ref[0] = roff.astype(jnp.int32)


def _moeplan(route, tb):
    t = route.shape[0]
    nt = t // tb
    tri = jnp.asarray(np.tril(np.ones((tb, tb), np.float32), -1)).astype(BF16)
    triu = jnp.asarray(np.triu(np.ones((LANES, LANES), np.float32), 1)).astype(BF16)
    small = pl.BlockSpec((1, 1, LANES), lambda i: (i, 0, 0))
    return pl.pallas_call(
        _moeplan_kernel,
        grid=(nt,),
        in_specs=[pl.BlockSpec((tb, LANES), lambda i: (i, 0)),
                  pl.BlockSpec((tb, tb), lambda i: (0, 0)),
                  pl.BlockSpec((LANES, LANES), lambda i: (0, 0))],
        out_specs=[pl.BlockSpec((tb, LANES), lambda i: (i, 0)), small, small],
        out_shape=[jax.ShapeDtypeStruct((t, LANES), F32),
                   jax.ShapeDtypeStruct((nt, 1, LANES), jnp.int32),
                   jax.ShapeDtypeStruct((nt, 1, LANES), jnp.int32)],
        compiler_params=_params(("parallel",)),
        name="moeplan",
    )(route, tri, triu)


def _moe_kernel(pos_ref, nb_ref, roff_ref, x1_ref, posw_ref, gffn_ref, wg_hbm, wu_hbm, wd_hbm, o_ref,
                xn, xs, g1, g2, wg_buf, wu_buf, wd_buf, sem):
    i = pl.program_id(0)
    nt = pl.num_programs(0)
    tb = x1_ref.shape[0]
    chunk = g1.shape[0]
    dummy_r0 = xs.shape[0] - MOE_RB
    group = MOE_EXPERTS_PER_STEP
    ngroups = N_EXPERTS // group

    def weight_copies(e, slot):
        return (pltpu.make_async_copy(wg_hbm.at[e], wg_buf.at[slot], sem.at[slot, 0]),
                pltpu.make_async_copy(wu_hbm.at[e], wu_buf.at[slot], sem.at[slot, 1]),
                pltpu.make_async_copy(wd_hbm.at[e], wd_buf.at[slot], sem.at[slot, 2]))

    ahead = MOE_WEIGHT_BUFS - 1

    def start_group(n):
        first = lax.rem(n, ngroups) * group
        ring = lax.rem(n, MOE_WEIGHT_BUFS) * group
        for j in range(group):
            for c in weight_copies(first + j, ring + j):
                c.start()

    @pl.when(i == 0)
    def _():
        for n in range(ahead):
            start_group(n)
        xs[...] = jnp.zeros_like(xs)

    for c in range(tb // chunk):
        sl = slice(c * chunk, (c + 1) * chunk)
        x1 = x1_ref[sl, :]
        xn[sl, :] = x1 * lax.rsqrt(jnp.mean(x1 * x1, axis=-1, keepdims=True) + NORM_EPS) * gffn_ref[...]

    def put(tok, carry):
        row = xn[pl.ds(tok, 1), :]
        xs[pl.ds(pos_ref[2 * tok], 1), :] = row
        xs[pl.ds(pos_ref[2 * tok + 1], 1), :] = row
        return carry

    lax.fori_loop(0, tb, put, 0, unroll=8)

    def expert_group(p, carry):
        n = i * ngroups + p
        base = lax.rem(n, MOE_WEIGHT_BUFS) * group

        @pl.when(n + ahead < nt * ngroups)
        def _():
            start_group(n + ahead)

        for j in range(group):
            for c in weight_copies(p * group + j, base + j):
                c.wait()
        nbs = [nb_ref[0, 0, p * group + j] for j in range(group)]
        roffs = [roff_ref[0, 0, p * group + j] for j in range(group)]
        trips = functools.reduce(jnp.maximum, nbs)

        def block(b, c2):
            r0s = [pl.multiple_of(jnp.where(b < nbs[j], roffs[j] + b * MOE_RB, dummy_r0), 8)
                   for j in range(group)]
            x = [xs[pl.ds(r0, MOE_RB), :].astype(BF16) for r0 in r0s]
            hg = [jnp.dot(x[j], wg_buf[base + j], preferred_element_type=F32) for j in range(group)]
            hu = [jnp.dot(x[j], wu_buf[base + j], preferred_element_type=F32) for j in range(group)]
            hid = [(hg[j] * jax.nn.sigmoid(hg[j]) * hu[j]).astype(BF16) for j in range(group)]
            y = [jnp.dot(hid[j], wd_buf[base + j], preferred_element_type=F32) for j in range(group)]
            for j in range(group):
                xs[pl.ds(r0s[j], MOE_RB), :] = y[j]
            return c2

        lax.fori_loop(0, trips, block, 0)
        return carry

    lax.fori_loop(0, ngroups, expert_group, 0)

    for c in range(tb // chunk):
        def take(j, carry, c=c):
            tok = c * chunk + j
            g1[pl.ds(j, 1), :] = xs[pl.ds(pos_ref[2 * tok], 1), :]
            g2[pl.ds(j, 1), :] = xs[pl.ds(pos_ref[2 * tok + 1], 1), :]
            return carry

        lax.fori_loop(0, chunk, take, 0, unroll=8)
        sl = slice(c * chunk, (c + 1) * chunk)
        o_ref[sl, :] = x1_ref[sl, :] + (posw_ref[sl, 2:3] * g1[...] + posw_ref[sl, 3:4] * g2[...])


def _moe(x1, route, g_ffn, w_gate, w_up, w_down, tb):
    t = x1.shape[0]
    nt = t // tb
    chunk = min(tb, 256)
    rows = 2 * tb + N_EXPERTS * MOE_RB + MOE_RB
    nslot = MOE_WEIGHT_BUFS * MOE_EXPERTS_PER_STEP
    posw, nb, roff = _moeplan(route, tb)
    pos = posw[:, 0:2].astype(jnp.int32).reshape(2 * t)
    smem3 = lambda: pl.BlockSpec((1, 1, LANES), lambda i: (i, 0, 0), memory_space=pltpu.SMEM)
    hbm = lambda: pl.BlockSpec(memory_space=pl.ANY)
    return pl.pallas_call(
        _moe_kernel,
        grid=(nt,),
        in_specs=[pl.BlockSpec((2 * tb,), lambda i: (i,), memory_space=pltpu.SMEM), smem3(), smem3(),
                  pl.BlockSpec((tb, D_MODEL), lambda i: (i, 0)),
                  pl.BlockSpec((tb, LANES), lambda i: (i, 0)),
                  pl.BlockSpec((1, D_MODEL), lambda i: (0, 0)),
                  hbm(), hbm(), hbm()],
        out_specs=pl.BlockSpec((tb, D_MODEL), lambda i: (i, 0)),
        out_shape=jax.ShapeDtypeStruct((t, D_MODEL), F32),
        scratch_shapes=[pltpu.VMEM((tb, D_MODEL), F32), pltpu.VMEM((rows, D_MODEL), F32),
                        pltpu.VMEM((chunk, D_MODEL), F32), pltpu.VMEM((chunk, D_MODEL), F32),
                        pltpu.VMEM((nslot, D_MODEL, D_EXPERT), BF16), pltpu.VMEM((nslot, D_MODEL, D_EXPERT), BF16),
                        pltpu.VMEM((nslot, D_EXPERT, D_MODEL), BF16), pltpu.SemaphoreType.DMA((nslot, 3))],
        compiler_params=_params(("arbitrary",)),
        name="moe",
    )(pos, nb, roff, x1, posw, g_ffn[None, :], w_gate.astype(BF16), w_up.astype(BF16), w_down.astype(BF16))


def _tile(seq, want):
    tm = min(seq, want)
    assert seq % tm == 0 and tm % 8 == 0
    return tm


def _layer(x, positions, g_mix, w_in, g_cq, w_uq, g_ckv, w_uk, w_uv, g_qn, g_kn, g_mla_out,
           rw_mu, rw_w0, rw_w2, rw_a0, rw_a2, rw_g2, rw_k_k, rw_k_a, rw_r_k, rw_lnx_g, rw_lnx_b,
           w_o, g_ffn, w_group, b_group, w_expert, b_expert, w_gate, w_up, w_down):
    bsz, seq, d = x.shape
    t = bsz * seq
    x2 = x.reshape(t, d)
    pos2 = positions.reshape(t, 1)
    kv_blk = _tile(seq, KV_BLK)
    feat, q, k, v = _inproj(x2, pos2, g_mix, w_in, g_cq, w_uq, g_ckv, w_uk, w_uv, g_qn, g_kn,
                            bsz, seq, _tile(kv_blk, 512), kv_blk)
    mla = _attn(q, k, v).reshape(t, MLA_W)
    r, k2, vv, kk, lr, ld, g = _rwprep(feat, rw_mu, rw_w0, rw_w2, rw_a0, rw_a2, rw_g2, rw_k_k, rw_k_a,
                                       bsz, seq, _tile(seq, 512))
    rw = _rwscan(r, k2, vv, kk, lr, ld, g, rw_r_k, rw_lnx_g, rw_lnx_b, bsz, seq, _tile(seq, 128))
    rw = rw.reshape(t, RWKV_W)
    x1, route = _outproj(x2, mla, rw, g_mla_out, w_o, g_ffn, w_group, b_group, w_expert, b_expert,
                         seq, _tile(seq, 512))
    out = _moe(x1, route, g_ffn, w_gate, w_up, w_down, _tile(t, MOE_TB))
    return out.reshape(bsz, seq, d)


def kernel(x, positions, g_mix, w_in, g_cq, w_uq, g_ckv, w_uk, w_uv, g_qn, g_kn, g_mla_out, rw_mu, rw_w0, rw_w2, rw_a0, rw_a2, rw_g2, rw_k_k, rw_k_a, rw_r_k, rw_lnx_g, rw_lnx_b, w_o, g_ffn, w_group, b_group, w_expert, b_expert, w_gate, w_up, w_down):
    for l in range(g_mix.shape[0]):
        x = _layer(x, positions, g_mix[l], w_in[l], g_cq[l], w_uq[l], g_ckv[l], w_uk[l], w_uv[l],
                   g_qn[l], g_kn[l], g_mla_out[l], rw_mu[l], rw_w0[l], rw_w2[l], rw_a0[l], rw_a2[l],
                   rw_g2[l], rw_k_k[l], rw_k_a[l], rw_r_k[l], rw_lnx_g[l], rw_lnx_b[l], w_o[l],
                   g_ffn[l], w_group[l], b_group[l], w_expert[l], b_expert[l], w_gate[l], w_up[l],
                   w_down[l])
    return x
```
